```python
import jax
import jax.numpy as jnp
from jax import lax
import numpy as np

D_MODEL = 1024
BATCH = 32
SEQ = 2048
DEPTH = 1

MEM_LEN = 256
MIX_WIDTH = D_MODEL
MLSTM_WIDTH = MIX_WIDTH // 2
GMLP_WIDTH = MIX_WIDTH - MLSTM_WIDTH
MLSTM_HEADS = 4
MLSTM_DV = MLSTM_WIDTH // MLSTM_HEADS
MLSTM_DK = MLSTM_DV // 2
MLSTM_CHUNK = 128
GATE_SOFTCAP = 15.0
GMLP_GROUPS = 4
GMLP_DG = GMLP_WIDTH // GMLP_GROUPS
GMLP_CHUNK = 128
XATTN_HEADS = 4
XATTN_HD = D_MODEL // XATTN_HEADS
N_EXPERTS = 32
TOP_K = 4
D_FF = D_MODEL
SWIGLU_LIMIT = 7.0
SWIGLU_ALPHA = 1.702
MOE_BLOCK = 256
NORM_EPS = 1e-6
IN_SIZES = (MLSTM_HEADS * MLSTM_DK, MLSTM_HEADS * MLSTM_DK, MLSTM_WIDTH, MLSTM_WIDTH, MLSTM_HEADS, MLSTM_HEADS, GMLP_WIDTH, GMLP_WIDTH)
IN_WIDTH = sum(IN_SIZES)

kernel_name = 'hybrid_mlstm_gmlp_moe_layer'


def rmsnorm(x, g):
    xf = x.astype(jnp.float32)
    y = xf * lax.rsqrt(jnp.mean(xf * xf, axis=-1, keepdims=True) + NORM_EPS)
    return (y * g.astype(jnp.float32)).astype(x.dtype)


def softcap(a, cap):
    return cap * jnp.tanh(a / cap)


def mlstm_chunkwise(q, k, v, i_pre, f_pre):
    B, S, H, DK = q.shape
    DV = v.shape[-1]
    L = MLSTM_CHUNK
    NC = S // L

    def to_chunks(a):
        a = a.astype(jnp.float32).reshape((B, NC, L, H) + a.shape[3:])
        return jnp.moveaxis(a, (1, 3), (0, 2))

    qc = to_chunks(q) * (DK ** -0.5)
    kc = to_chunks(k)
    vc = to_chunks(v)
    log_i = to_chunks(softcap(i_pre.astype(jnp.float32), GATE_SOFTCAP))
    log_f = to_chunks(jax.nn.log_sigmoid(softcap(f_pre.astype(jnp.float32), GATE_SOFTCAP)))
    causal = jnp.tril(jnp.ones((L, L), dtype=bool))

    def chunk_step(carry, xs):
        C, n, m = carry
        qb, kb, vb, ib, fb = xs
        b = jnp.cumsum(fb, axis=-1)
        d = b[..., :, None] - b[..., None, :] + ib[..., None, :]
        d = jnp.where(causal, d, -jnp.inf)
        inter = b + m[..., None]
        m_t = jnp.maximum(inter, jnp.max(d, axis=-1))
        w_intra = jnp.exp(d - m_t[..., None])
        w_inter = jnp.exp(inter - m_t)
        s = jnp.einsum('bhtd,bhsd->bhts', qb, kb) * w_intra
        num = jnp.einsum('bhts,bhsv->bhtv', s, vb) + w_inter[..., None] * jnp.einsum('bhvd,bhtd->bhtv', C, qb)
        den = jnp.sum(s, axis=-1) + w_inter * jnp.einsum('bhd,bhtd->bht', n, qb)
        h = num / jnp.maximum(jnp.abs(den), jnp.exp(-m_t))[..., None]
        g = b[..., -1:] - b + ib
        m_new = jnp.maximum(b[..., -1] + m, jnp.max(g, axis=-1))
        wk = jnp.exp(g - m_new[..., None])
        decay = jnp.exp(b[..., -1] + m - m_new)
        C_new = decay[..., None, None] * C + jnp.einsum('bhs,bhsv,bhsd->bhvd', wk, vb, kb)
        n_new = decay[..., None] * n + jnp.einsum('bhs,bhsd->bhd', wk, kb)
        return (C_new, n_new, m_new), h

    init = (jnp.zeros((B, H, DV, DK), jnp.float32), jnp.zeros((B, H, DK), jnp.float32), jnp.zeros((B, H), jnp.float32))
    _, hc = lax.scan(chunk_step, init, (qc, kc, vc, log_i, log_f))
    return jnp.moveaxis(hc, (0, 2), (1, 3)).reshape(B, S, H, DV)


def hybrid_mixer(xn, w_in, b_gates, g_mlstm_head, w_spatial, b_spatial, g_gmlp_v, g_gmlp_out, w_out):
    B, S, _ = xn.shape
    H, G = MLSTM_HEADS, GMLP_GROUPS
    proj = xn @ w_in
    cuts = [sum(IN_SIZES[:j]) for j in range(1, len(IN_SIZES))]
    q, k, v, o, i_pre, f_pre, gu, gv = jnp.split(proj, cuts, axis=-1)
    i_pre = i_pre + b_gates[:H]
    f_pre = f_pre + b_gates[H:]
    h = mlstm_chunkwise(q.reshape(B, S, H, MLSTM_DK), k.reshape(B, S, H, MLSTM_DK), v.reshape(B, S, H, MLSTM_DV), i_pre, f_pre)
    h = rmsnorm(h, g_mlstm_head.reshape(H, MLSTM_DV)).astype(xn.dtype)
    y_mlstm = jax.nn.sigmoid(o) * h.reshape(B, S, MLSTM_WIDTH)
    L = GMLP_CHUNK
    NC = S // L
    u = jax.nn.gelu(gu).reshape(B, S, G, GMLP_DG)
    z = rmsnorm(jax.nn.gelu(gv).reshape(B, S, G, GMLP_DG), g_gmlp_v.reshape(G, GMLP_DG))
    causal = jnp.tril(jnp.ones((L, L), dtype=bool))
    w_sp = jnp.where(causal, w_spatial, jnp.zeros_like(w_spatial))
    mixed = jnp.einsum('gts,bcsgd->bctgd', w_sp, z.reshape(B, NC, L, G, GMLP_DG)) + b_spatial.T[None, None, :, :, None]
    y_gmlp = rmsnorm(u * mixed.reshape(B, S, G, GMLP_DG), g_gmlp_out.reshape(G, GMLP_DG)).reshape(B, S, GMLP_WIDTH)
    return jnp.concatenate([y_mlstm, y_gmlp], axis=-1) @ w_out


def memory_cross_attention(xn, memn, w_q, w_kv, w_xo):
    B, S, D = xn.shape
    M = memn.shape[1]
    q = (xn @ w_q).reshape(B, S, XATTN_HEADS, XATTN_HD)
    k, v = jnp.split(memn @ w_kv, 2, axis=-1)
    k = k.reshape(B, M, XATTN_HEADS, XATTN_HD)
    v = v.reshape(B, M, XATTN_HEADS, XATTN_HD)
    scores = jnp.einsum('bshd,bmhd->bhsm', q, k).astype(jnp.float32) * (XATTN_HD ** -0.5)
    p = jax.nn.softmax(scores, axis=-1).astype(v.dtype)
    o = jnp.einsum('bhsm,bmhd->bshd', p, v).reshape(B, S, D)
    return o @ w_xo


def moe_ffn(xn, w_router, b_router, w_gate, b_gate, w_up, b_up, w_down, b_down):
    B, S, D = xn.shape
    T = B * S
    A = T * TOP_K
    R = MOE_BLOCK
    n_blocks = -(-A // R) + N_EXPERTS
    n_slots = n_blocks * R
    xf = xn.reshape(T, D)
    logits = (xf @ w_router + b_router).astype(jnp.float32)
    top_logit, top_idx = lax.top_k(logits, TOP_K)
    gate = jax.nn.softmax(top_logit, axis=-1)
    flat_e = top_idx.reshape(A)
    order = jnp.argsort(flat_e)
    sorted_e = flat_e[order]
    counts = jnp.bincount(flat_e, length=N_EXPERTS)
    padded = (counts + R - 1) // R * R
    pad_end = jnp.cumsum(padded)
    pad_start = pad_end - padded
    grp_start = jnp.cumsum(counts) - counts
    dest = pad_start[sorted_e] + (jnp.arange(A) - grp_start[sorted_e])
    slot_tok = jnp.zeros((n_slots,), jnp.int32).at[dest].set((order // TOP_K).astype(jnp.int32))
    slot_w = jnp.zeros((n_slots,), jnp.float32).at[dest].set(gate.reshape(A)[order])
    block_e = jnp.minimum(jnp.searchsorted(pad_end, jnp.arange(n_blocks) * R, side='right'), N_EXPERTS - 1)

    def expert_block(args):
        tok, w, e = args
        xb = xf[tok]
        g = jnp.minimum(xb @ w_gate[e] + b_gate[e], SWIGLU_LIMIT)
        u = jnp.clip(xb @ w_up[e] + b_up[e], -SWIGLU_LIMIT, SWIGLU_LIMIT)
        hdn = (u + 1) * (g * jax.nn.sigmoid(SWIGLU_ALPHA * g))
        y = hdn @ w_down[e] + b_down[e]
        return y * w[:, None].astype(y.dtype)

    y = lax.map(expert_block, (slot_tok.reshape(n_blocks, R), slot_w.reshape(n_blocks, R), block_e))
    out = jax.ops.segment_sum(y.reshape(n_slots, D), slot_tok, num_segments=T)
    return out.reshape(B, S, D)


def setup_inputs(seed: int = 0) -> dict:
    key = jax.random.key(seed)
    ks = jax.random.split(key, 32)

    def nrm(k, shape, scale):
        return scale * jax.random.normal(k, shape, jnp.float32)

    def gain(k, shape):
        return 1.0 + nrm(k, shape, 0.02)

    Lr, H, D, E, F = DEPTH, MLSTM_HEADS, D_MODEL, N_EXPERTS, D_FF
    b_gates = jnp.concatenate([nrm(ks[4], (Lr, H), 0.1),
                               jnp.broadcast_to(jnp.linspace(3.0, 6.0, H), (Lr, H)) + nrm(ks[5], (Lr, H), 0.1)], axis=-1)
    return {
        'x': nrm(ks[0], (BATCH, SEQ, D), 1.0),
        'mem': nrm(ks[1], (BATCH, MEM_LEN, D), 1.0),
        'g_mix': gain(ks[2], (Lr, D)),
        'w_in': nrm(ks[3], (Lr, D, IN_WIDTH), D ** -0.5),
        'b_gates': b_gates,
        'g_mlstm_head': gain(ks[6], (Lr, MLSTM_WIDTH)),
        'w_spatial': nrm(ks[7], (Lr, GMLP_GROUPS, GMLP_CHUNK, GMLP_CHUNK), 0.5 * GMLP_CHUNK ** -0.5),
        'b_spatial': 1.0 + nrm(ks[8], (Lr, GMLP_GROUPS, GMLP_CHUNK), 0.1),
        'g_gmlp_v': gain(ks[9], (Lr, GMLP_WIDTH)),
        'g_gmlp_out': gain(ks[10], (Lr, GMLP_WIDTH)),
        'w_out': nrm(ks[11], (Lr, MIX_WIDTH, D), MIX_WIDTH ** -0.5),
        'g_xattn': gain(ks[12], (Lr, D)),
        'g_mem': gain(ks[13], (Lr, D)),
        'w_q': nrm(ks[14], (Lr, D, D), D ** -0.5),
        'w_kv': nrm(ks[15], (Lr, D, 2 * D), D ** -0.5),
        'w_xo': nrm(ks[16], (Lr, D, D), D ** -0.5),
        'g_moe': gain(ks[17], (Lr, D)),
        'w_router': nrm(ks[18], (Lr, D, E), D ** -0.5),
        'b_router': nrm(ks[19], (Lr, E), 0.01),
        'w_gate': nrm(ks[20], (Lr, E, D, F), D ** -0.5),
        'b_gate': nrm(ks[21], (Lr, E, F), 0.01),
        'w_up': nrm(ks[22], (Lr, E, D, F), D ** -0.5),
        'b_up': nrm(ks[23], (Lr, E, F), 0.01),
        'w_down': nrm(ks[24], (Lr, E, F, D), F ** -0.5),
        'b_down': nrm(ks[25], (Lr, E, D), 0.01),
        'g_final': gain(ks[26], (D,)),
    }


def reference(x, mem, g_mix, w_in, b_gates, g_mlstm_head, w_spatial, b_spatial, g_gmlp_v, g_gmlp_out, w_out,
              g_xattn, g_mem, w_q, w_kv, w_xo, g_moe, w_router, b_router, w_gate, b_gate, w_up, b_up,
              w_down, b_down, g_final):
    for l in range(DEPTH):
        x = x + hybrid_mixer(rmsnorm(x, g_mix[l]), w_in[l], b_gates[l], g_mlstm_head[l], w_spatial[l],
                             b_spatial[l], g_gmlp_v[l], g_gmlp_out[l], w_out[l])
        x = x + memory_cross_attention(rmsnorm(x, g_xattn[l]), rmsnorm(mem, g_mem[l]), w_q[l], w_kv[l], w_xo[l])
        x = x + moe_ffn(rmsnorm(x, g_moe[l]), w_router[l], b_router[l], w_gate[l], b_gate[l], w_up[l], b_up[l],
                        w_down[l], b_down[l])
    return rmsnorm(x, g_final)
```

```python
import functools

import jax
import jax.numpy as jnp
from jax import lax
from jax.experimental import pallas as pl
from jax.experimental.pallas import tpu as pltpu

F32 = jnp.float32
BF16 = jnp.bfloat16

MLSTM_HEADS = 4
MLSTM_DK = 64
MLSTM_DV = 128
CHUNK = 128
GMLP_GROUPS = 4
GMLP_DG = 128
XATTN_HEADS = 4
TOP_K = 4
GATE_SOFTCAP = 15.0
SWIGLU_LIMIT = 7.0
SWIGLU_ALPHA = 1.702
NORM_EPS = 1e-6

LANES = 128
HEAD_PAD = 128

MIX_TOKENS = 512
XATTN_TOKENS = 512
MOE_ROWS = 256
DISPATCH_TOKENS = 512
COMBINE_TOKENS = 256
VMEM_LIMIT = 52 * 1024 * 1024


def _rms(x, g):
    return x * lax.rsqrt(jnp.mean(x * x, axis=-1, keepdims=True) + NORM_EPS) * g


def _gelu_tanh(x):
    return 0.5 * x * (1.0 + jnp.tanh(0.7978845608028654 * (x + 0.044715 * x * x * x)))


def _dot(a, b):
    return jnp.dot(a, b, preferred_element_type=F32)


def _dot_nt(a, b):
    return lax.dot_general(a, b, (((1,), (1,)), ((), ())), preferred_element_type=F32)


def _split3(x):
    hi = x.astype(BF16)
    r1 = x - hi.astype(F32)
    mid = r1.astype(BF16)
    lo = (r1 - mid.astype(F32)).astype(BF16)
    return hi, mid, lo


def _mixer_kernel(x_ref, gmix_ref, win_ref, bg_ref, ghead_ref, wsp_ref, bsp_ref, gzv_ref, gzo_ref, wout_ref,
                  o_ref, proj_scr, y_scr, ct_scr, m_scr, *, n_chunks, cols):
    q0, k0, v0, og0, gu0, gv0, gt0 = cols
    H, G, L = MLSTM_HEADS, GMLP_GROUPS, CHUNK

    @pl.when(pl.program_id(1) == 0)
    def _():
        ct_scr[...] = jnp.zeros_like(ct_scr)
        m_scr[...] = jnp.zeros_like(m_scr)

    x = x_ref[0]
    xn = _rms(x, gmix_ref[...])
    proj_scr[...] = _dot(xn.astype(BF16), win_ref[...])

    row = lax.broadcasted_iota(jnp.int32, (L, L), 0)
    col = lax.broadcasted_iota(jnp.int32, (L, L), 1)
    causal = row >= col
    tril = jnp.where(causal, 1.0, 0.0).astype(BF16)
    triu = jnp.where(row <= col, 1.0, 0.0).astype(BF16)
    lane = lax.broadcasted_iota(jnp.int32, (L, LANES), 1)
    ones_col = jnp.where(lane == 0, 1.0, 0.0).astype(F32)

    def chunk(c, carry):
        rows = pl.ds(pl.multiple_of(c * L, L), L)
        pre = proj_scr[rows, gt0:gt0 + LANES] + bg_ref[...]
        capped = GATE_SOFTCAP * jnp.tanh(pre * (1.0 / GATE_SOFTCAP))
        log_sig = jnp.minimum(capped, 0.0) - jnp.log1p(jnp.exp(-jnp.abs(capped)))
        lg = jnp.where(lane < H, capped, log_sig)
        lg_t = lg.T[0:8, :]
        bcol = sum(_dot(tril, p) for p in _split3(lg))
        brow = sum(_dot(p, triu) for p in _split3(lg_t))

        for h in range(H):
            b_c = bcol[:, H + h:H + h + 1]
            i_c = lg[:, h:h + 1]
            b_r = brow[H + h:H + h + 1, :]
            i_r = lg_t[h:h + 1, :]
            m_prev = m_scr[h, 0:1, 0:1]
            q = proj_scr[rows, q0 + h * HEAD_PAD:q0 + (h + 1) * HEAD_PAD] * (MLSTM_DK ** -0.5)
            k = proj_scr[rows, k0 + h * HEAD_PAD:k0 + (h + 1) * HEAD_PAD]
            v = proj_scr[rows, v0 + h * MLSTM_DV:v0 + (h + 1) * MLSTM_DV]
            qb = q.astype(BF16)
            kb = k.astype(BF16)
            vaug = jnp.concatenate([v, ones_col], axis=1)

            d = jnp.where(causal, b_c - b_r + i_r, -jnp.inf)
            inter = b_c + m_prev
            m_t = jnp.maximum(inter, jnp.max(d, axis=1, keepdims=True))
            w_intra = jnp.exp(d - m_t)
            w_inter = jnp.exp(inter - m_t)
            s = _dot_nt(qb, kb) * w_intra
            ct = ct_scr[h]
            na = _dot(s.astype(BF16), vaug.astype(BF16)) + w_inter * _dot(qb, ct.astype(BF16))
            num = na[:, :MLSTM_DV]
            den = na[:, MLSTM_DV:MLSTM_DV + 1]
            hh = num / jnp.maximum(jnp.abs(den), jnp.exp(-m_t))

            b_last = b_c[L - 1:L, :]
            g_c = b_last - b_c + i_c
            m_new = jnp.maximum(b_last + m_prev, jnp.max(g_c, axis=0, keepdims=True))
            wk = jnp.exp(g_c - m_new)
            decay = jnp.exp(b_last + m_prev - m_new)
            ct_scr[h] = decay * ct + _dot(k.T.astype(BF16), (wk * vaug).astype(BF16))
            m_scr[h] = jnp.broadcast_to(m_new, m_scr.shape[1:])

            hn = _rms(hh, ghead_ref[:, h * MLSTM_DV:(h + 1) * MLSTM_DV])
            og = proj_scr[rows, og0 + h * MLSTM_DV:og0 + (h + 1) * MLSTM_DV]
            y_scr[rows, h * MLSTM_DV:(h + 1) * MLSTM_DV] = (jax.nn.sigmoid(og) * hn).astype(BF16)

        for g in range(G):
            sl = slice(g * GMLP_DG, (g + 1) * GMLP_DG)
            u = _gelu_tanh(proj_scr[rows, gu0 + g * GMLP_DG:gu0 + (g + 1) * GMLP_DG])
            z = _rms(_gelu_tanh(proj_scr[rows, gv0 + g * GMLP_DG:gv0 + (g + 1) * GMLP_DG]), gzv_ref[:, sl])
            wsp = jnp.where(causal, wsp_ref[g], 0.0)
            mixed = _dot(wsp.astype(BF16), z.astype(BF16)) + bsp_ref[g]
            yg = _rms(u * mixed, gzo_ref[:, sl])
            y_scr[rows, H * MLSTM_DV + g * GMLP_DG:H * MLSTM_DV + (g + 1) * GMLP_DG] = yg.astype(BF16)
        return carry

    lax.fori_loop(0, n_chunks, chunk, 0)
    o_ref[0] = x + _dot(y_scr[...], wout_ref[...])


def _mixer(x, g_mix, w_in, b_gates, g_head, w_spatial, b_spatial, g_zv, g_zo, w_out):
    B, S, D = x.shape
    H, G, L = MLSTM_HEADS, GMLP_GROUPS, CHUNK
    qw, vw, gw = H * MLSTM_DK, H * MLSTM_DV, G * GMLP_DG
    ts = min(MIX_TOKENS, S)
    assert S % ts == 0 and ts % L == 0

    def pad_heads(w):
        w = w.reshape(D, H, MLSTM_DK)
        return jnp.pad(w, ((0, 0), (0, 0), (0, HEAD_PAD - MLSTM_DK))).reshape(D, H * HEAD_PAD)

    c = 0
    w_q = pad_heads(w_in[:, c:c + qw]); c += qw
    w_k = pad_heads(w_in[:, c:c + qw]); c += qw
    w_v = w_in[:, c:c + vw]; c += vw
    w_o = w_in[:, c:c + vw]; c += vw
    w_g = jnp.pad(w_in[:, c:c + 2 * H], ((0, 0), (0, LANES - 2 * H))); c += 2 * H
    w_gu = w_in[:, c:c + gw]; c += gw
    w_gv = w_in[:, c:c + gw]; c += gw
    w_pack = jnp.concatenate([w_q, w_k, w_v, w_o, w_gu, w_gv, w_g], axis=1).astype(BF16)
    hp = H * HEAD_PAD
    cols = (0, hp, 2 * hp, 2 * hp + vw, 2 * hp + 2 * vw, 2 * hp + 2 * vw + gw, 2 * hp + 2 * vw + 2 * gw)
    n_cols = w_pack.shape[1]
    bg = jnp.pad(b_gates, (0, LANES - 2 * H)).reshape(1, LANES)
    bsp = jnp.broadcast_to(b_spatial[:, :, None], (G, L, GMLP_DG))

    const = lambda *shape: pl.BlockSpec(shape, lambda b, j: (0,) * len(shape))
    return pl.pallas_call(
        functools.partial(_mixer_kernel, n_chunks=ts // L, cols=cols),
        out_shape=jax.ShapeDtypeStruct((B, S, D), F32),
        grid=(B, S // ts),
        in_specs=[
            pl.BlockSpec((1, ts, D), lambda b, j: (b, j, 0)),
            const(1, D), const(D, n_cols), const(1, LANES), const(1, vw),
            const(G, L, L), const(G, L, GMLP_DG), const(1, gw), const(1, gw), const(vw + gw, D),
        ],
        out_specs=pl.BlockSpec((1, ts, D), lambda b, j: (b, j, 0)),
        scratch_shapes=[
            pltpu.VMEM((ts, n_cols), F32),
            pltpu.VMEM((ts, vw + gw), BF16),
            pltpu.VMEM((H, HEAD_PAD, 2 * MLSTM_DV), F32),
            pltpu.VMEM((H, 8, LANES), F32),
        ],
        compiler_params=pltpu.CompilerParams(
            dimension_semantics=("arbitrary", "arbitrary"), vmem_limit_bytes=VMEM_LIMIT),
        name="mixer",
    )(x, g_mix.reshape(1, D), w_pack, bg, g_head.reshape(1, vw), w_spatial, bsp,
      g_zv.reshape(1, gw), g_zo.reshape(1, gw), w_out.astype(BF16))


def _kv_kernel(mem_ref, g_ref, w_ref, k_ref, v_ref):
    D = mem_ref.shape[-1]
    kv = _dot(_rms(mem_ref[0], g_ref[...]).astype(BF16), w_ref[...])
    k_ref[0] = kv[:, :D].astype(BF16)
    v_ref[0] = kv[:, D:].astype(BF16)


def _kv_proj(mem, g_mem, w_kv):
    B, M, D = mem.shape
    return pl.pallas_call(
        _kv_kernel,
        out_shape=(jax.ShapeDtypeStruct((B, M, D), BF16), jax.ShapeDtypeStruct((B, M, D), BF16)),
        grid=(B,),
        in_specs=[
            pl.BlockSpec((1, M, D), lambda b: (b, 0, 0)),
            pl.BlockSpec((1, D), lambda b: (0, 0)),
            pl.BlockSpec((D, 2 * D), lambda b: (0, 0)),
        ],
        out_specs=(pl.BlockSpec((1, M, D), lambda b: (b, 0, 0)), pl.BlockSpec((1, M, D), lambda b: (b, 0, 0))),
        compiler_params=pltpu.CompilerParams(dimension_semantics=("arbitrary",), vmem_limit_bytes=VMEM_LIMIT),
        name="kv_proj",
    )(mem, g_mem.reshape(1, D), w_kv.astype(BF16))


def _xattn_kernel(x_ref, gx_ref, wq_ref, k_ref, v_ref, wxo_ref, gmoe_ref, wr_ref, br_ref,
                  x2_ref, xn_ref, idx_ref, gate_ref, rank_ref, cnt_ref, o_scr, cnt_scr):
    ts, D = x_ref.shape[1], x_ref.shape[2]
    hd = D // XATTN_HEADS
    E = wr_ref.shape[0]

    @pl.when((pl.program_id(0) == 0) & (pl.program_id(1) == 0))
    def _():
        cnt_scr[...] = jnp.zeros_like(cnt_scr)

    x = x_ref[0]
    q = _dot(_rms(x, gx_ref[...]).astype(BF16), wq_ref[...]) * (hd ** -0.5)
    for h in range(XATTN_HEADS):
        sl = slice(h * hd, (h + 1) * hd)
        s = _dot_nt(q[:, sl].astype(BF16), k_ref[0, :, sl])
        p = jnp.exp(s - jnp.max(s, axis=1, keepdims=True))
        p = p / jnp.sum(p, axis=1, keepdims=True)
        o_scr[:, sl] = _dot(p.astype(BF16), v_ref[0, :, sl]).astype(BF16)
    x2 = x + _dot(o_scr[...], wxo_ref[...])
    x2_ref[0] = x2

    xn = _rms(x2, gmoe_ref[...])
    xn_ref[...] = xn
    xh, xm, xl = _split3(xn)
    wh, wm, wl = _split3(wr_ref[...])
    logits = (_dot_nt(wh, xh) + (_dot_nt(wh, xm) + _dot_nt(wm, xh))
              + (_dot_nt(wh, xl) + _dot_nt(wm, xm) + _dot_nt(wl, xh))) + br_ref[...]

    e_iota = lax.broadcasted_iota(jnp.int32, (E, ts), 0)
    work = logits
    tops, idxs, hots = [], [], []
    for _ in range(TOP_K):
        m = jnp.max(work, axis=0, keepdims=True)
        i = jnp.min(jnp.where(work == m, e_iota, E), axis=0, keepdims=True)
        hot = e_iota == i
        work = jnp.where(hot, -jnp.inf, work)
        tops.append(m); idxs.append(i); hots.append(hot)
    ex = [jnp.exp(t - tops[0]) for t in tops]
    tot = ex[0] + ex[1] + ex[2] + ex[3]
    gate_ref[...] = jnp.concatenate([e / tot for e in ex], axis=0)
    idx_ref[...] = jnp.concatenate(idxs, axis=0)

    hot_all = jnp.where(hots[0] | hots[1] | hots[2] | hots[3], 1.0, 0.0)
    r = lax.broadcasted_iota(jnp.int32, (ts, ts), 0)
    c = lax.broadcasted_iota(jnp.int32, (ts, ts), 1)
    before = jnp.where(r < c, 1.0, 0.0).astype(BF16)
    base = cnt_scr[:, 0:1] + _dot(hot_all.astype(BF16), before)
    ranks = [jnp.sum(jnp.where(hot, base, 0.0), axis=0, keepdims=True) for hot in hots]
    rank_ref[...] = jnp.concatenate(ranks, axis=0).astype(jnp.int32)
    cnt_new = cnt_scr[...] + jnp.sum(hot_all, axis=1, keepdims=True)
    cnt_scr[...] = cnt_new
    cnt_ref[...] = cnt_new


def _xattn_router(x, k, v, g_xattn, w_q, w_xo, g_moe, w_router, b_router):
    B, S, D = x.shape
    M = k.shape[1]
    E = w_router.shape[1]
    ts = min(XATTN_TOKENS, S)
    assert S % ts == 0
    nj = S // ts
    T = B * S
    const = lambda *shape: pl.BlockSpec(shape, lambda b, j: (0,) * len(shape))
    tok = lambda rows: pl.BlockSpec((rows, ts), lambda b, j: (0, b * nj + j))
    return pl.pallas_call(
        _xattn_kernel,
        out_shape=(
            jax.ShapeDtypeStruct((B, S, D), F32),
            jax.ShapeDtypeStruct((T, D), F32),
            jax.ShapeDtypeStruct((TOP_K, T), jnp.int32),
            jax.ShapeDtypeStruct((TOP_K, T), F32),
            jax.ShapeDtypeStruct((TOP_K, T), jnp.int32),
            jax.ShapeDtypeStruct((E, LANES), F32),
        ),
        grid=(B, nj),
        in_specs=[
            pl.BlockSpec((1, ts, D), lambda b, j: (b, j, 0)),
            const(1, D), const(D, D),
            pl.BlockSpec((1, M, D), lambda b, j: (b, 0, 0)),
            pl.BlockSpec((1, M, D), lambda b, j: (b, 0, 0)),
            const(D, D), const(1, D), const(E, D), const(E, 1),
        ],
        out_specs=(
            pl.BlockSpec((1, ts, D), lambda b, j: (b, j, 0)),
            pl.BlockSpec((ts, D), lambda b, j: (b * nj + j, 0)),
            tok(TOP_K), tok(TOP_K), tok(TOP_K),
            const(E, LANES),
        ),
        scratch_shapes=[pltpu.VMEM((ts, D), BF16), pltpu.VMEM((E, LANES), F32)],
        compiler_params=pltpu.CompilerParams(
            dimension_semantics=("arbitrary", "arbitrary"), vmem_limit_bytes=VMEM_LIMIT),
        name="xattn_router",
    )(x, g_xattn.reshape(1, D), w_q.astype(BF16), k, v, w_xo.astype(BF16), g_moe.reshape(1, D),
      w_router.T, b_router.reshape(E, 1))


def _dispatch_kernel(dest_ref, cnt_ref, pend_ref, xn_ref, xs_ref, zero_scr, sem, zsem, *, n_experts):
    tt = xn_ref.shape[0]
    R = zero_scr.shape[0]

    @pl.when(pl.program_id(0) == 0)
    def _():
        zero_scr[...] = jnp.zeros_like(zero_scr)

        def fill(e, carry):
            @pl.when(cnt_ref[e] > 0)
            def _():
                start = pl.multiple_of(pend_ref[e] - R, R)
                cp = pltpu.make_async_copy(zero_scr, xs_ref.at[pl.ds(start, R)], zsem)
                cp.start()
                cp.wait()
            return carry

        lax.fori_loop(0, n_experts, fill, 0)

    def row_copy(p):
        t = p % tt
        return pltpu.make_async_copy(xn_ref.at[pl.ds(t, 1)], xs_ref.at[pl.ds(dest_ref[0, 0, p], 1)], sem)

    def issue(p, carry):
        row_copy(p).start()
        return carry

    def drain(p, carry):
        row_copy(p).wait()
        return carry

    lax.fori_loop(0, TOP_K * tt, issue, 0)
    lax.fori_loop(0, TOP_K * tt, drain, 0)


def _dispatch(xn, dest_blocks, counts, pad_end, n_slots):
    T, D = xn.shape
    nb, _, per = dest_blocks.shape
    tt = per // TOP_K
    E = counts.shape[0]
    return pl.pallas_call(
        functools.partial(_dispatch_kernel, n_experts=E),
        out_shape=jax.ShapeDtypeStruct((n_slots, D), F32),
        grid=(nb,),
        in_specs=[
            pl.BlockSpec((1, 1, per), lambda i: (i, 0, 0), memory_space=pltpu.SMEM),
            pl.BlockSpec(memory_space=pltpu.SMEM),
            pl.BlockSpec(memory_space=pltpu.SMEM),
            pl.BlockSpec((tt, D), lambda i: (i, 0)),
        ],
        out_specs=pl.BlockSpec(memory_space=pl.ANY),
        scratch_shapes=[pltpu.VMEM((MOE_ROWS, D), F32), pltpu.SemaphoreType.DMA, pltpu.SemaphoreType.DMA],
        compiler_params=pltpu.CompilerParams(
            dimension_semantics=("arbitrary",), vmem_limit_bytes=VMEM_LIMIT, has_side_effects=True),
        name="dispatch",
    )(dest_blocks, counts, pad_end, xn)


def _expert_kernel(be_ref, nu_ref, xs_ref, wg_ref, bg_ref, wu_ref, bu_ref, wd_ref, bd_ref, y_ref):
    @pl.when(pl.program_id(0) < nu_ref[0])
    def _():
        xb = xs_ref[...].astype(BF16)
        g = jnp.minimum(_dot(xb, wg_ref[0]) + bg_ref[0], SWIGLU_LIMIT)
        u = jnp.clip(_dot(xb, wu_ref[0]) + bu_ref[0], -SWIGLU_LIMIT, SWIGLU_LIMIT)
        hdn = (u + 1.0) * (g * jax.nn.sigmoid(SWIGLU_ALPHA * g))
        y_ref[...] = _dot(hdn.astype(BF16), wd_ref[0]) + bd_ref[0]


def _experts(xs, block_e, n_used, w_gate, b_gate, w_up, b_up, w_down, b_down):
    n_slots, D = xs.shape
    E, _, F = w_gate.shape
    R = MOE_ROWS
    n_blocks = n_slots // R
    blk = lambda j, be, nu: (jnp.minimum(j, nu[0] - 1), 0)
    wsel = lambda j, be, nu: (be[jnp.minimum(j, nu[0] - 1)], 0, 0)
    grid_spec = pltpu.PrefetchScalarGridSpec(
        num_scalar_prefetch=2,
        grid=(n_blocks,),
        in_specs=[
            pl.BlockSpec((R, D), blk),
            pl.BlockSpec((1, D, F), wsel), pl.BlockSpec((1, 1, F), wsel),
            pl.BlockSpec((1, D, F), wsel), pl.BlockSpec((1, 1, F), wsel),
            pl.BlockSpec((1, F, D), wsel), pl.BlockSpec((1, 1, D), wsel),
        ],
        out_specs=pl.BlockSpec((R, D), blk),
    )
    return pl.pallas_call(
        _expert_kernel,
        out_shape=jax.ShapeDtypeStruct((n_slots, D), F32),
        grid_spec=grid_spec,
        compiler_params=pltpu.CompilerParams(dimension_semantics=("arbitrary",), vmem_limit_bytes=VMEM_LIMIT),
        name="experts",
    )(block_e, n_used, xs, w_gate.astype(BF16), b_gate.reshape(E, 1, F), w_up.astype(BF16), b_up.reshape(E, 1, F),
      w_down.astype(BF16), b_down.reshape(E, 1, D))


def _combine_kernel(dest_ref, x_ref, gate_ref, gfin_ref, ys_ref, o_ref, buf, sem):
    tc = x_ref.shape[0]

    def row_copy(p):
        kk = p // tc
        t = p % tc
        return pltpu.make_async_copy(ys_ref.at[pl.ds(dest_ref[0, 0, p], 1)], buf.at[kk, pl.ds(t, 1)], sem)

    def issue(p, carry):
        row_copy(p).start()
        return carry

    def drain(p, carry):
        row_copy(p).wait()
        return carry

    lax.fori_loop(0, TOP_K * tc, issue, 0)
    lax.fori_loop(0, TOP_K * tc, drain, 0)
    acc = x_ref[...]
    for kk in range(TOP_K):
        acc = acc + gate_ref[:, kk:kk + 1] * buf[kk]
    o_ref[...] = _rms(acc, gfin_ref[...])


def _combine(x2, gate_t, dest_blocks, ys, g_final):
    T, D = x2.shape
    nb, _, per = dest_blocks.shape
    tc = per // TOP_K
    return pl.pallas_call(
        _combine_kernel,
        out_shape=jax.ShapeDtypeStruct((T, D), F32),
        grid=(nb,),
        in_specs=[
            pl.BlockSpec((1, 1, per), lambda i: (i, 0, 0), memory_space=pltpu.SMEM),
            pl.BlockSpec((tc, D), lambda i: (i, 0)),
            pl.BlockSpec((tc, TOP_K), lambda i: (i, 0)),
            pl.BlockSpec((1, D), lambda i: (0, 0)),
            pl.BlockSpec(memory_space=pl.ANY),
        ],
        out_specs=pl.BlockSpec((tc, D), lambda i: (i, 0)),
        scratch_shapes=[pltpu.VMEM((TOP_K, tc, D), F32), pltpu.SemaphoreType.DMA],
        compiler_params=pltpu.CompilerParams(dimension_semantics=("arbitrary",), vmem_limit_bytes=VMEM_LIMIT),
        name="combine",
    )(dest_blocks, x2, gate_t, g_final.reshape(1, D), ys)


def _block_dest(dest, tokens):
    T = dest.shape[1]
    return dest.reshape(TOP_K, T // tokens, tokens).transpose(1, 0, 2).reshape(T // tokens, 1, TOP_K * tokens)


def _moe(x2, xn, idx, gate, rank, cnt, w_gate, b_gate, w_up, b_up, w_down, b_down, g_final):
    T, D = xn.shape
    E = w_gate.shape[0]
    R = MOE_ROWS
    n_blocks = -(-T * TOP_K // R) + E
    n_slots = n_blocks * R
    counts = cnt[:, 0].astype(jnp.int32)
    padded = (counts + R - 1) // R * R
    pad_end = jnp.cumsum(padded)
    pad_start = pad_end - padded
    dest = pad_start[idx] + rank
    n_used = (pad_end[-1:] // R).astype(jnp.int32)
    block_e = jnp.minimum(jnp.searchsorted(pad_end, jnp.arange(n_blocks) * R, side='right'), E - 1).astype(jnp.int32)

    xs = _dispatch(xn, _block_dest(dest, min(DISPATCH_TOKENS, T)), counts, pad_end.astype(jnp.int32), n_slots)
    ys = _experts(xs, block_e, n_used, w_gate, b_gate, w_up, b_up, w_down, b_down)
    return _combine(x2, gate.T, _block_dest(dest, min(COMBINE_TOKENS, T)), ys, g_final)


def kernel(x, mem, g_mix, w_in, b_gates, g_mlstm_head, w_spatial, b_spatial, g_gmlp_v, g_gmlp_out, w_out,
           g_xattn, g_mem, w_q, w_kv, w_xo, g_moe, w_router, b_router, w_gate, b_gate, w_up, b_up,
           w_down, b_down, g_final):
    B, S, D = x.shape
    assert g_mix.shape[0] == 1, "the combine kernel fuses the closing norm, so exactly one layer is supported"
    x1 = _mixer(x, g_mix[0], w_in[0], b_gates[0], g_mlstm_head[0], w_spatial[0], b_spatial[0],
                g_gmlp_v[0], g_gmlp_out[0], w_out[0])
    k, v = _kv_proj(mem, g_mem[0], w_kv[0])
    x2, xn, idx, gate, rank, cnt = _xattn_router(x1, k, v, g_xattn[0], w_q[0], w_xo[0], g_moe[0],
                                                 w_router[0], b_router[0])
    out = _moe(x2.reshape(B * S, D), xn, idx, gate, rank, cnt, w_gate[0], b_gate[0], w_up[0], b_up[0],
               w_down[0], b_down[0], g_final)
    return out.reshape(B, S, D)
```

```python
import functools

import jax
import jax.numpy as jnp
from jax import lax
from jax.experimental import pallas as pl
from jax.experimental.pallas import tpu as pltpu

F32 = jnp.float32
BF16 = jnp.bfloat16

MLSTM_HEADS = 4
MLSTM_DK = 64
MLSTM_DV = 128
CHUNK = 128
GMLP_GROUPS = 4
GMLP_DG = 128
XATTN_HEADS = 4
TOP_K = 4
GATE_SOFTCAP = 15.0
SWIGLU_LIMIT = 7.0
SWIGLU_ALPHA = 1.702
NORM_EPS = 1e-6

LANES = 128
HEAD_PAD = 128

MIX_TOKENS = 512
XATTN_TOKENS = 512
MOE_ROWS = 512
DISPATCH_TOKENS = 1024
COMBINE_TOKENS = 256
ROW_UNROLL = 16
VMEM_LIMIT = 52 * 1024 * 1024


def _rms(x, g):
    return x * lax.rsqrt(jnp.mean(x * x, axis=-1, keepdims=True) + NORM_EPS) * g


def _gelu_tanh(x):
    return 0.5 * x * (1.0 + jnp.tanh(0.7978845608028654 * (x + 0.044715 * x * x * x)))


def _dot(a, b):
    return jnp.dot(a, b, preferred_element_type=F32)


def _dot_nt(a, b):
    return lax.dot_general(a, b, (((1,), (1,)), ((), ())), preferred_element_type=F32)


def _split3(x):
    hi = x.astype(BF16)
    r1 = x - hi.astype(F32)
    mid = r1.astype(BF16)
    lo = (r1 - mid.astype(F32)).astype(BF16)
    return hi, mid, lo


def _mixer_kernel(x_ref, gmix_ref, win_ref, bg_ref, ghead_ref, wsp_ref, bsp_ref, gzv_ref, gzo_ref, wout_ref,
                  o_ref, proj_scr, y_scr, ct_scr, m_scr, *, n_chunks, cols):
    q0, k0, v0, og0, gu0, gv0, gt0 = cols
    H, G, L = MLSTM_HEADS, GMLP_GROUPS, CHUNK

    @pl.when(pl.program_id(1) == 0)
    def _():
        ct_scr[...] = jnp.zeros_like(ct_scr)
        m_scr[...] = jnp.zeros_like(m_scr)

    x = x_ref[0]
    xn = _rms(x, gmix_ref[...])
    proj_scr[...] = _dot(xn.astype(BF16), win_ref[...])

    row = lax.broadcasted_iota(jnp.int32, (L, L), 0)
    col = lax.broadcasted_iota(jnp.int32, (L, L), 1)
    causal = row >= col
    tril = jnp.where(causal, 1.0, 0.0).astype(BF16)
    triu = jnp.where(row <= col, 1.0, 0.0).astype(BF16)
    lane = lax.broadcasted_iota(jnp.int32, (L, LANES), 1)
    ones_col = jnp.where(lane == 0, 1.0, 0.0).astype(F32)

    def chunk(c, carry):
        rows = pl.ds(pl.multiple_of(c * L, L), L)
        pre = proj_scr[rows, gt0:gt0 + LANES] + bg_ref[...]
        capped = GATE_SOFTCAP * jnp.tanh(pre * (1.0 / GATE_SOFTCAP))
        log_sig = jnp.minimum(capped, 0.0) - jnp.log1p(jnp.exp(-jnp.abs(capped)))
        lg = jnp.where(lane < H, capped, log_sig)
        lg_t = lg.T[0:8, :]
        bcol = sum(_dot(tril, p) for p in _split3(lg))
        brow = sum(_dot(p, triu) for p in _split3(lg_t))

        for h in range(H):
            b_c = bcol[:, H + h:H + h + 1]
            i_c = lg[:, h:h + 1]
            b_r = brow[H + h:H + h + 1, :]
            i_r = lg_t[h:h + 1, :]
            m_prev = m_scr[h, 0:1, 0:1]
            q = proj_scr[rows, q0 + h * HEAD_PAD:q0 + (h + 1) * HEAD_PAD] * (MLSTM_DK ** -0.5)
            k = proj_scr[rows, k0 + h * HEAD_PAD:k0 + (h + 1) * HEAD_PAD]
            v = proj_scr[rows, v0 + h * MLSTM_DV:v0 + (h + 1) * MLSTM_DV]
            qb = q.astype(BF16)
            kb = k.astype(BF16)
            vaug = jnp.concatenate([v, ones_col], axis=1)

            d = jnp.where(causal, b_c - b_r + i_r, -jnp.inf)
            inter = b_c + m_prev
            m_t = jnp.maximum(inter, jnp.max(d, axis=1, keepdims=True))
            w_intra = jnp.exp(d - m_t)
            w_inter = jnp.exp(inter - m_t)
            s = _dot_nt(qb, kb) * w_intra
            ct = ct_scr[h]
            na = _dot(s.astype(BF16), vaug.astype(BF16)) + w_inter * _dot(qb, ct.astype(BF16))
            num = na[:, :MLSTM_DV]
            den = na[:, MLSTM_DV:MLSTM_DV + 1]
            hh = num / jnp.maximum(jnp.abs(den), jnp.exp(-m_t))

            b_last = b_c[L - 1:L, :]
            g_c = b_last - b_c + i_c
            m_new = jnp.maximum(b_last + m_prev, jnp.max(g_c, axis=0, keepdims=True))
            wk = jnp.exp(g_c - m_new)
            decay = jnp.exp(b_last + m_prev - m_new)
            ct_scr[h] = decay * ct + _dot(k.T.astype(BF16), (wk * vaug).astype(BF16))
            m_scr[h] = jnp.broadcast_to(m_new, m_scr.shape[1:])

            hn = _rms(hh, ghead_ref[:, h * MLSTM_DV:(h + 1) * MLSTM_DV])
            og = proj_scr[rows, og0 + h * MLSTM_DV:og0 + (h + 1) * MLSTM_DV]
            y_scr[rows, h * MLSTM_DV:(h + 1) * MLSTM_DV] = (jax.nn.sigmoid(og) * hn).astype(BF16)

        for g in range(G):
            sl = slice(g * GMLP_DG, (g + 1) * GMLP_DG)
            u = _gelu_tanh(proj_scr[rows, gu0 + g * GMLP_DG:gu0 + (g + 1) * GMLP_DG])
            z = _rms(_gelu_tanh(proj_scr[rows, gv0 + g * GMLP_DG:gv0 + (g + 1) * GMLP_DG]), gzv_ref[:, sl])
            wsp = jnp.where(causal, wsp_ref[g], 0.0)
            mixed = _dot(wsp.astype(BF16), z.astype(BF16)) + bsp_ref[g]
            yg = _rms(u * mixed, gzo_ref[:, sl])
            y_scr[rows, H * MLSTM_DV + g * GMLP_DG:H * MLSTM_DV + (g + 1) * GMLP_DG] = yg.astype(BF16)
        return carry

    lax.fori_loop(0, n_chunks, chunk, 0)
    o_ref[0] = x + _dot(y_scr[...], wout_ref[...])


def _mixer(x, g_mix, w_in, b_gates, g_head, w_spatial, b_spatial, g_zv, g_zo, w_out):
    B, S, D = x.shape
    H, G, L = MLSTM_HEADS, GMLP_GROUPS, CHUNK
    qw, vw, gw = H * MLSTM_DK, H * MLSTM_DV, G * GMLP_DG
    ts = min(MIX_TOKENS, S)
    assert S % ts == 0 and ts % L == 0

    def pad_heads(w):
        w = w.reshape(D, H, MLSTM_DK)
        return jnp.pad(w, ((0, 0), (0, 0), (0, HEAD_PAD - MLSTM_DK))).reshape(D, H * HEAD_PAD)

    c = 0
    w_q = pad_heads(w_in[:, c:c + qw]); c += qw
    w_k = pad_heads(w_in[:, c:c + qw]); c += qw
    w_v = w_in[:, c:c + vw]; c += vw
    w_o = w_in[:, c:c + vw]; c += vw
    w_g = jnp.pad(w_in[:, c:c + 2 * H], ((0, 0), (0, LANES - 2 * H))); c += 2 * H
    w_gu = w_in[:, c:c + gw]; c += gw
    w_gv = w_in[:, c:c + gw]; c += gw
    w_pack = jnp.concatenate([w_q, w_k, w_v, w_o, w_gu, w_gv, w_g], axis=1).astype(BF16)
    hp = H * HEAD_PAD
    cols = (0, hp, 2 * hp, 2 * hp + vw, 2 * hp + 2 * vw, 2 * hp + 2 * vw + gw, 2 * hp + 2 * vw + 2 * gw)
    n_cols = w_pack.shape[1]
    bg = jnp.pad(b_gates, (0, LANES - 2 * H)).reshape(1, LANES)
    bsp = jnp.broadcast_to(b_spatial[:, :, None], (G, L, GMLP_DG))

    const = lambda *shape: pl.BlockSpec(shape, lambda b, j: (0,) * len(shape))
    return pl.pallas_call(
        functools.partial(_mixer_kernel, n_chunks=ts // L, cols=cols),
        out_shape=jax.ShapeDtypeStruct((B, S, D), F32),
        grid=(B, S // ts),
        in_specs=[
            pl.BlockSpec((1, ts, D), lambda b, j: (b, j, 0)),
            const(1, D), const(D, n_cols), const(1, LANES), const(1, vw),
            const(G, L, L), const(G, L, GMLP_DG), const(1, gw), const(1, gw), const(vw + gw, D),
        ],
        out_specs=pl.BlockSpec((1, ts, D), lambda b, j: (b, j, 0)),
        scratch_shapes=[
            pltpu.VMEM((ts, n_cols), F32),
            pltpu.VMEM((ts, vw + gw), BF16),
            pltpu.VMEM((H, HEAD_PAD, 2 * MLSTM_DV), F32),
            pltpu.VMEM((H, 8, LANES), F32),
        ],
        compiler_params=pltpu.CompilerParams(
            dimension_semantics=("arbitrary", "arbitrary"), vmem_limit_bytes=VMEM_LIMIT),
        name="mixer",
    )(x, g_mix.reshape(1, D), w_pack, bg, g_head.reshape(1, vw), w_spatial, bsp,
      g_zv.reshape(1, gw), g_zo.reshape(1, gw), w_out.astype(BF16))


def _kv_kernel(mem_ref, g_ref, w_ref, k_ref, v_ref):
    D = mem_ref.shape[-1]
    kv = _dot(_rms(mem_ref[0], g_ref[...]).astype(BF16), w_ref[...])
    k_ref[0] = kv[:, :D].astype(BF16)
    v_ref[0] = kv[:, D:].astype(BF16)


def _kv_proj(mem, g_mem, w_kv):
    B, M, D = mem.shape
    return pl.pallas_call(
        _kv_kernel,
        out_shape=(jax.ShapeDtypeStruct((B, M, D), BF16), jax.ShapeDtypeStruct((B, M, D), BF16)),
        grid=(B,),
        in_specs=[
            pl.BlockSpec((1, M, D), lambda b: (b, 0, 0)),
            pl.BlockSpec((1, D), lambda b: (0, 0)),
            pl.BlockSpec((D, 2 * D), lambda b: (0, 0)),
        ],
        out_specs=(pl.BlockSpec((1, M, D), lambda b: (b, 0, 0)), pl.BlockSpec((1, M, D), lambda b: (b, 0, 0))),
        compiler_params=pltpu.CompilerParams(dimension_semantics=("arbitrary",), vmem_limit_bytes=VMEM_LIMIT),
        name="kv_proj",
    )(mem, g_mem.reshape(1, D), w_kv.astype(BF16))


def _xattn_kernel(x_ref, gx_ref, wq_ref, k_ref, v_ref, wxo_ref, gmoe_ref, wr_ref, br_ref,
                  x2_ref, xn_ref, idx_ref, gate_ref, rank_ref, cnt_ref, o_scr, cnt_scr):
    ts, D = x_ref.shape[1], x_ref.shape[2]
    hd = D // XATTN_HEADS
    E = wr_ref.shape[0]

    @pl.when((pl.program_id(0) == 0) & (pl.program_id(1) == 0))
    def _():
        cnt_scr[...] = jnp.zeros_like(cnt_scr)

    x = x_ref[0]
    q = _dot(_rms(x, gx_ref[...]).astype(BF16), wq_ref[...]) * (hd ** -0.5)
    for h in range(XATTN_HEADS):
        sl = slice(h * hd, (h + 1) * hd)
        s = _dot_nt(q[:, sl].astype(BF16), k_ref[0, :, sl])
        p = jnp.exp(s - jnp.max(s, axis=1, keepdims=True))
        p = p / jnp.sum(p, axis=1, keepdims=True)
        o_scr[:, sl] = _dot(p.astype(BF16), v_ref[0, :, sl]).astype(BF16)
    x2 = x + _dot(o_scr[...], wxo_ref[...])
    x2_ref[0] = x2

    xn = _rms(x2, gmoe_ref[...])
    xn_ref[...] = xn
    xh, xm, xl = _split3(xn)
    wh, wm, wl = _split3(wr_ref[...])
    logits = (_dot_nt(wh, xh) + (_dot_nt(wh, xm) + _dot_nt(wm, xh))
              + (_dot_nt(wh, xl) + _dot_nt(wm, xm) + _dot_nt(wl, xh))) + br_ref[...]

    e_iota = lax.broadcasted_iota(jnp.int32, (E, ts), 0)
    work = logits
    tops, idxs, hots = [], [], []
    for _ in range(TOP_K):
        m = jnp.max(work, axis=0, keepdims=True)
        i = jnp.min(jnp.where(work == m, e_iota, E), axis=0, keepdims=True)
        hot = e_iota == i
        work = jnp.where(hot, -jnp.inf, work)
        tops.append(m); idxs.append(i); hots.append(hot)
    ex = [jnp.exp(t - tops[0]) for t in tops]
    tot = ex[0] + ex[1] + ex[2] + ex[3]
    gate_ref[...] = jnp.concatenate([e / tot for e in ex], axis=0)
    idx_ref[...] = jnp.concatenate(idxs, axis=0)

    hot_all = jnp.where(hots[0] | hots[1] | hots[2] | hots[3], 1.0, 0.0)
    r = lax.broadcasted_iota(jnp.int32, (ts, ts), 0)
    c = lax.broadcasted_iota(jnp.int32, (ts, ts), 1)
    before = jnp.where(r < c, 1.0, 0.0).astype(BF16)
    base = cnt_scr[:, 0:1] + _dot(hot_all.astype(BF16), before)
    ranks = [jnp.sum(jnp.where(hot, base, 0.0), axis=0, keepdims=True) for hot in hots]
    rank_ref[...] = jnp.concatenate(ranks, axis=0).astype(jnp.int32)
    cnt_new = cnt_scr[...] + jnp.sum(hot_all, axis=1, keepdims=True)
    cnt_scr[...] = cnt_new
    cnt_ref[...] = cnt_new


def _xattn_router(x, k, v, g_xattn, w_q, w_xo, g_moe, w_router, b_router):
    B, S, D = x.shape
    M = k.shape[1]
    E = w_router.shape[1]
    ts = min(XATTN_TOKENS, S)
    assert S % ts == 0
    nj = S // ts
    T = B * S
    const = lambda *shape: pl.BlockSpec(shape, lambda b, j: (0,) * len(shape))
    tok = lambda rows: pl.BlockSpec((rows, ts), lambda b, j: (0, b * nj + j))
    return pl.pallas_call(
        _xattn_kernel,
        out_shape=(
            jax.ShapeDtypeStruct((B, S, D), F32),
            jax.ShapeDtypeStruct((T, D), F32),
            jax.ShapeDtypeStruct((TOP_K, T), jnp.int32),
            jax.ShapeDtypeStruct((TOP_K, T), F32),
            jax.ShapeDtypeStruct((TOP_K, T), jnp.int32),
            jax.ShapeDtypeStruct((E, LANES), F32),
        ),
        grid=(B, nj),
        in_specs=[
            pl.BlockSpec((1, ts, D), lambda b, j: (b, j, 0)),
            const(1, D), const(D, D),
            pl.BlockSpec((1, M, D), lambda b, j: (b, 0, 0)),
            pl.BlockSpec((1, M, D), lambda b, j: (b, 0, 0)),
            const(D, D), const(1, D), const(E, D), const(E, 1),
        ],
        out_specs=(
            pl.BlockSpec((1, ts, D), lambda b, j: (b, j, 0)),
            pl.BlockSpec((ts, D), lambda b, j: (b * nj + j, 0)),
            tok(TOP_K), tok(TOP_K), tok(TOP_K),
            const(E, LANES),
        ),
        scratch_shapes=[pltpu.VMEM((ts, D), BF16), pltpu.VMEM((E, LANES), F32)],
        compiler_params=pltpu.CompilerParams(
            dimension_semantics=("arbitrary", "arbitrary"), vmem_limit_bytes=VMEM_LIMIT),
        name="xattn_router",
    )(x, g_xattn.reshape(1, D), w_q.astype(BF16), k, v, w_xo.astype(BF16), g_moe.reshape(1, D),
      w_router.T, b_router.reshape(E, 1))


def _dispatch_kernel(dest_ref, cnt_ref, pend_ref, xn_ref, xs_ref, zero_scr, sem, zsem, *, n_experts):
    tt = xn_ref.shape[0]
    R = zero_scr.shape[0]

    @pl.when(pl.program_id(0) == 0)
    def _():
        zero_scr[...] = jnp.zeros_like(zero_scr)

        def fill(e, carry):
            @pl.when(cnt_ref[e] > 0)
            def _():
                start = pl.multiple_of(pend_ref[e] - R, R)
                cp = pltpu.make_async_copy(zero_scr, xs_ref.at[pl.ds(start, R)], zsem)
                cp.start()
                cp.wait()
            return carry

        lax.fori_loop(0, n_experts, fill, 0)

    def issue(grp, carry):
        t0 = pl.multiple_of(grp * ROW_UNROLL, ROW_UNROLL)
        for kk in range(TOP_K):
            for u in range(ROW_UNROLL):
                dst = dest_ref[0, 0, kk * tt + t0 + u]
                pltpu.make_async_copy(xn_ref.at[pl.ds(t0 + u, 1)], xs_ref.at[pl.ds(dst, 1)], sem).start(priority=u % 2)
        return carry

    lax.fori_loop(0, tt // ROW_UNROLL, issue, 0)
    for kk in range(TOP_K):
        pltpu.make_async_copy(xn_ref, xs_ref.at[pl.ds(0, tt)], sem).wait()


def _dispatch(xn, dest_blocks, counts, pad_end, n_slots):
    T, D = xn.shape
    nb, _, per = dest_blocks.shape
    tt = per // TOP_K
    E = counts.shape[0]
    return pl.pallas_call(
        functools.partial(_dispatch_kernel, n_experts=E),
        out_shape=jax.ShapeDtypeStruct((n_slots, D), F32),
        grid=(nb,),
        in_specs=[
            pl.BlockSpec((1, 1, per), lambda i: (i, 0, 0), memory_space=pltpu.SMEM),
            pl.BlockSpec(memory_space=pltpu.SMEM),
            pl.BlockSpec(memory_space=pltpu.SMEM),
            pl.BlockSpec((tt, D), lambda i: (i, 0)),
        ],
        out_specs=pl.BlockSpec(memory_space=pl.ANY),
        scratch_shapes=[pltpu.VMEM((MOE_ROWS, D), F32), pltpu.SemaphoreType.DMA, pltpu.SemaphoreType.DMA],
        compiler_params=pltpu.CompilerParams(
            dimension_semantics=("arbitrary",), vmem_limit_bytes=VMEM_LIMIT, has_side_effects=True),
        name="dispatch",
    )(dest_blocks, counts, pad_end, xn)


def _expert_kernel(be_ref, nu_ref, xs_ref, wg_ref, bg_ref, wu_ref, bu_ref, wd_ref, bd_ref, y_ref):
    @pl.when(pl.program_id(0) < nu_ref[0])
    def _():
        xb = xs_ref[...].astype(BF16)
        g = jnp.minimum(_dot(xb, wg_ref[0]) + bg_ref[0], SWIGLU_LIMIT)
        u = jnp.clip(_dot(xb, wu_ref[0]) + bu_ref[0], -SWIGLU_LIMIT, SWIGLU_LIMIT)
        hdn = (u + 1.0) * (g * jax.nn.sigmoid(SWIGLU_ALPHA * g))
        y_ref[...] = _dot(hdn.astype(BF16), wd_ref[0]) + bd_ref[0]


def _experts(xs, block_e, n_used, w_gate, b_gate, w_up, b_up, w_down, b_down):
    n_slots, D = xs.shape
    E, _, F = w_gate.shape
    R = MOE_ROWS
    n_blocks = n_slots // R
    blk = lambda j, be, nu: (jnp.minimum(j, nu[0] - 1), 0)
    wsel = lambda j, be, nu: (be[jnp.minimum(j, nu[0] - 1)], 0, 0)
    grid_spec = pltpu.PrefetchScalarGridSpec(
        num_scalar_prefetch=2,
        grid=(n_blocks,),
        in_specs=[
            pl.BlockSpec((R, D), blk),
            pl.BlockSpec((1, D, F), wsel), pl.BlockSpec((1, 1, F), wsel),
            pl.BlockSpec((1, D, F), wsel), pl.BlockSpec((1, 1, F), wsel),
            pl.BlockSpec((1, F, D), wsel), pl.BlockSpec((1, 1, D), wsel),
        ],
        out_specs=pl.BlockSpec((R, D), blk),
    )
    return pl.pallas_call(
        _expert_kernel,
        out_shape=jax.ShapeDtypeStruct((n_slots, D), F32),
        grid_spec=grid_spec,
        compiler_params=pltpu.CompilerParams(dimension_semantics=("arbitrary",), vmem_limit_bytes=VMEM_LIMIT),
        name="experts",
    )(block_e, n_used, xs, w_gate.astype(BF16), b_gate.reshape(E, 1, F), w_up.astype(BF16), b_up.reshape(E, 1, F),
      w_down.astype(BF16), b_down.reshape(E, 1, D))


def _combine_kernel(dcur_ref, dnext_ref, x_ref, gate_ref, gfin_ref, ys_ref, o_ref, buf, sem):
    tc = x_ref.shape[0]
    i = pl.program_id(0)
    slot = i % 2

    def issue_all(d_ref, into):
        def issue(grp, carry):
            t0 = pl.multiple_of(grp * ROW_UNROLL, ROW_UNROLL)
            for kk in range(TOP_K):
                for u in range(ROW_UNROLL):
                    src = d_ref[0, 0, kk * tc + t0 + u]
                    pltpu.make_async_copy(ys_ref.at[pl.ds(src, 1)], buf.at[into, kk, pl.ds(t0 + u, 1)],
                                          sem.at[into]).start(priority=u % 2)
            return carry

        lax.fori_loop(0, tc // ROW_UNROLL, issue, 0)

    @pl.when(i == 0)
    def _():
        issue_all(dcur_ref, 0)

    @pl.when(i + 1 < pl.num_programs(0))
    def _():
        issue_all(dnext_ref, 1 - slot)

    for kk in range(TOP_K):
        pltpu.make_async_copy(ys_ref.at[pl.ds(0, tc)], buf.at[slot, kk], sem.at[slot]).wait()
    acc = x_ref[...]
    for kk in range(TOP_K):
        acc = acc + gate_ref[:, kk:kk + 1] * buf[slot, kk]
    o_ref[...] = _rms(acc, gfin_ref[...])


def _combine(x2, gate_t, dest_blocks, ys, g_final):
    T, D = x2.shape
    nb, _, per = dest_blocks.shape
    tc = per // TOP_K
    return pl.pallas_call(
        _combine_kernel,
        out_shape=jax.ShapeDtypeStruct((T, D), F32),
        grid=(nb,),
        in_specs=[
            pl.BlockSpec((1, 1, per), lambda i: (i, 0, 0), memory_space=pltpu.SMEM),
            pl.BlockSpec((1, 1, per), lambda i: (jnp.minimum(i + 1, nb - 1), 0, 0), memory_space=pltpu.SMEM),
            pl.BlockSpec((tc, D), lambda i: (i, 0)),
            pl.BlockSpec((tc, TOP_K), lambda i: (i, 0)),
            pl.BlockSpec((1, D), lambda i: (0, 0)),
            pl.BlockSpec(memory_space=pl.ANY),
        ],
        out_specs=pl.BlockSpec((tc, D), lambda i: (i, 0)),
        scratch_shapes=[pltpu.VMEM((2, TOP_K, tc, D), F32), pltpu.SemaphoreType.DMA((2,))],
        compiler_params=pltpu.CompilerParams(dimension_semantics=("arbitrary",), vmem_limit_bytes=VMEM_LIMIT),
        name="combine",
    )(dest_blocks, dest_blocks, x2, gate_t, g_final.reshape(1, D), ys)


def _block_dest(dest, tokens):
    T = dest.shape[1]
    return dest.reshape(TOP_K, T // tokens, tokens).transpose(1, 0, 2).reshape(T // tokens, 1, TOP_K * tokens)


def _moe(x2, xn, idx, gate, rank, cnt, w_gate, b_gate, w_up, b_up, w_down, b_down, g_final):
    T, D = xn.shape
    E = w_gate.shape[0]
    R = MOE_ROWS
    n_blocks = -(-T * TOP_K // R) + E
    n_slots = n_blocks * R
    counts = cnt[:, 0].astype(jnp.int32)
    padded = (counts + R - 1) // R * R
    pad_end = jnp.cumsum(padded)
    pad_start = pad_end - padded
    dest = rank
    for e in range(E):
        dest = dest + jnp.where(idx == e, pad_start[e], 0)
    n_used = (pad_end[-1:] // R).astype(jnp.int32)
    block_e = jnp.minimum(jnp.sum(jnp.arange(n_blocks)[:, None] * R >= pad_end[None, :], axis=1), E - 1)
    block_e = block_e.astype(jnp.int32)

    xs = _dispatch(xn, _block_dest(dest, min(DISPATCH_TOKENS, T)), counts, pad_end.astype(jnp.int32), n_slots)
    ys = _experts(xs, block_e, n_used, w_gate, b_gate, w_up, b_up, w_down, b_down)
    return _combine(x2, gate.T, _block_dest(dest, min(COMBINE_TOKENS, T)), ys, g_final)


def kernel(x, mem, g_mix, w_in, b_gates, g_mlstm_head, w_spatial, b_spatial, g_gmlp_v, g_gmlp_out, w_out,
           g_xattn, g_mem, w_q, w_kv, w_xo, g_moe, w_router, b_router, w_gate, b_gate, w_up, b_up,
           w_down, b_down, g_final):
    B, S, D = x.shape
    assert g_mix.shape[0] == 1, "the combine kernel fuses the closing norm, so exactly one layer is supported"
    x1 = _mixer(x, g_mix[0], w_in[0], b_gates[0], g_mlstm_head[0], w_spatial[0], b_spatial[0],
                g_gmlp_v[0], g_gmlp_out[0], w_out[0])
    k, v = _kv_proj(mem, g_mem[0], w_kv[0])
    x2, xn, idx, gate, rank, cnt = _xattn_router(x1, k, v, g_xattn[0], w_q[0], w_xo[0], g_moe[0],
                                                 w_router[0], b_router[0])
    out = _moe(x2.reshape(B * S, D), xn, idx, gate, rank, cnt, w_gate[0], b_gate[0], w_up[0], b_up[0],
               w_down[0], b_down[0], g_final)
    return out.reshape(B, S, D)
```

```python
import functools

import jax
import jax.numpy as jnp
from jax import lax
from jax.experimental import pallas as pl
from jax.experimental.pallas import tpu as pltpu

F32 = jnp.float32
BF16 = jnp.bfloat16

MLSTM_HEADS = 4
MLSTM_DK = 64
MLSTM_DV = 128
CHUNK = 128
GMLP_GROUPS = 4
GMLP_DG = 128
XATTN_HEADS = 4
TOP_K = 4
GATE_SOFTCAP = 15.0
SWIGLU_LIMIT = 7.0
SWIGLU_ALPHA = 1.702
NORM_EPS = 1e-6

LANES = 128
HEAD_PAD = 128
ROW_TILE = 8

MIX_TOKENS = 512
XATTN_TOKENS = 512
MOE_ROWS = 512
DISPATCH_TOKENS = 1024
COMBINE_TOKENS = 256
ROW_UNROLL = 16
VMEM_LIMIT = 52 * 1024 * 1024


def _rms(x, g):
    return x * lax.rsqrt(jnp.mean(x * x, axis=-1, keepdims=True) + NORM_EPS) * g


def _gelu_tanh(x):
    return 0.5 * x * (1.0 + jnp.tanh(0.7978845608028654 * (x + 0.044715 * x * x * x)))


def _dot(a, b):
    return jnp.dot(a, b, preferred_element_type=F32)


def _dot_nt(a, b):
    return lax.dot_general(a, b, (((1,), (1,)), ((), ())), preferred_element_type=F32)


def _load_row_tiles(ref, rows):
    return jnp.concatenate([ref[pl.ds(j, rows, stride=ROW_TILE), :] for j in range(ROW_TILE)], axis=1)


def _store_row_tiles(ref, val):
    for j in range(ROW_TILE):
        ref[pl.ds(j, val.shape[0], stride=ROW_TILE), :] = val[:, j * LANES:(j + 1) * LANES]


def _split3(x):
    hi = x.astype(BF16)
    r1 = x - hi.astype(F32)
    mid = r1.astype(BF16)
    lo = (r1 - mid.astype(F32)).astype(BF16)
    return hi, mid, lo


def _mixer_kernel(x_ref, gmix_ref, win_ref, bg_ref, ghead_ref, wsp_ref, bsp_ref, gzv_ref, gzo_ref, wout_ref,
                  o_ref, proj_scr, y_scr, ct_scr, m_scr, *, n_chunks, cols):
    q0, k0, v0, og0, gu0, gv0, gt0 = cols
    H, G, L = MLSTM_HEADS, GMLP_GROUPS, CHUNK

    @pl.when(pl.program_id(1) == 0)
    def _():
        ct_scr[...] = jnp.zeros_like(ct_scr)
        m_scr[...] = jnp.zeros_like(m_scr)

    x = x_ref[0]
    xn = _rms(x, gmix_ref[...])
    proj_scr[...] = _dot(xn.astype(BF16), win_ref[...])

    row = lax.broadcasted_iota(jnp.int32, (L, L), 0)
    col = lax.broadcasted_iota(jnp.int32, (L, L), 1)
    causal = row >= col
    tril = jnp.where(causal, 1.0, 0.0).astype(BF16)
    triu = jnp.where(row <= col, 1.0, 0.0).astype(BF16)
    lane = lax.broadcasted_iota(jnp.int32, (L, LANES), 1)
    ones_col = jnp.where(lane == 0, 1.0, 0.0).astype(F32)

    def chunk(c, carry):
        rows = pl.ds(pl.multiple_of(c * L, L), L)
        pre = proj_scr[rows, gt0:gt0 + LANES] + bg_ref[...]
        capped = GATE_SOFTCAP * jnp.tanh(pre * (1.0 / GATE_SOFTCAP))
        log_sig = jnp.minimum(capped, 0.0) - jnp.log1p(jnp.exp(-jnp.abs(capped)))
        lg = jnp.where(lane < H, capped, log_sig)
        lg_t = lg.T[0:8, :]
        bcol = sum(_dot(tril, p) for p in _split3(lg))
        brow = sum(_dot(p, triu) for p in _split3(lg_t))

        for h in range(H):
            b_c = bcol[:, H + h:H + h + 1]
            i_c = lg[:, h:h + 1]
            b_r = brow[H + h:H + h + 1, :]
            i_r = lg_t[h:h + 1, :]
            m_prev = m_scr[h, 0:1, 0:1]
            q = proj_scr[rows, q0 + h * HEAD_PAD:q0 + (h + 1) * HEAD_PAD] * (MLSTM_DK ** -0.5)
            k = proj_scr[rows, k0 + h * HEAD_PAD:k0 + (h + 1) * HEAD_PAD]
            v = proj_scr[rows, v0 + h * MLSTM_DV:v0 + (h + 1) * MLSTM_DV]
            qb = q.astype(BF16)
            kb = k.astype(BF16)
            vaug = jnp.concatenate([v, ones_col], axis=1)

            d = jnp.where(causal, b_c - b_r + i_r, -jnp.inf)
            inter = b_c + m_prev
            m_t = jnp.maximum(inter, jnp.max(d, axis=1, keepdims=True))
            w_intra = jnp.exp(d - m_t)
            w_inter = jnp.exp(inter - m_t)
            s = _dot_nt(qb, kb) * w_intra
            ct = ct_scr[h]
            na = _dot(s.astype(BF16), vaug.astype(BF16)) + w_inter * _dot(qb, ct.astype(BF16))
            num = na[:, :MLSTM_DV]
            den = na[:, MLSTM_DV:MLSTM_DV + 1]
            hh = num / jnp.maximum(jnp.abs(den), jnp.exp(-m_t))

            b_last = b_c[L - 1:L, :]
            g_c = b_last - b_c + i_c
            m_new = jnp.maximum(b_last + m_prev, jnp.max(g_c, axis=0, keepdims=True))
            wk = jnp.exp(g_c - m_new)
            decay = jnp.exp(b_last + m_prev - m_new)
            ct_scr[h] = decay * ct + _dot(k.T.astype(BF16), (wk * vaug).astype(BF16))
            m_scr[h] = jnp.broadcast_to(m_new, m_scr.shape[1:])

            hn = _rms(hh, ghead_ref[:, h * MLSTM_DV:(h + 1) * MLSTM_DV])
            og = proj_scr[rows, og0 + h * MLSTM_DV:og0 + (h + 1) * MLSTM_DV]
            y_scr[rows, h * MLSTM_DV:(h + 1) * MLSTM_DV] = (jax.nn.sigmoid(og) * hn).astype(BF16)

        for g in range(G):
            sl = slice(g * GMLP_DG, (g + 1) * GMLP_DG)
            u = _gelu_tanh(proj_scr[rows, gu0 + g * GMLP_DG:gu0 + (g + 1) * GMLP_DG])
            z = _rms(_gelu_tanh(proj_scr[rows, gv0 + g * GMLP_DG:gv0 + (g + 1) * GMLP_DG]), gzv_ref[:, sl])
            wsp = jnp.where(causal, wsp_ref[g], 0.0)
            mixed = _dot(wsp.astype(BF16), z.astype(BF16)) + bsp_ref[g]
            yg = _rms(u * mixed, gzo_ref[:, sl])
            y_scr[rows, H * MLSTM_DV + g * GMLP_DG:H * MLSTM_DV + (g + 1) * GMLP_DG] = yg.astype(BF16)
        return carry

    lax.fori_loop(0, n_chunks, chunk, 0, unroll=True)
    o_ref[0] = x + _dot(y_scr[...], wout_ref[...])


def _mixer(x, g_mix, w_in, b_gates, g_head, w_spatial, b_spatial, g_zv, g_zo, w_out):
    B, S, D = x.shape
    H, G, L = MLSTM_HEADS, GMLP_GROUPS, CHUNK
    qw, vw, gw = H * MLSTM_DK, H * MLSTM_DV, G * GMLP_DG
    ts = min(MIX_TOKENS, S)
    assert S % ts == 0 and ts % L == 0

    def pad_heads(w):
        w = w.reshape(D, H, MLSTM_DK)
        return jnp.pad(w, ((0, 0), (0, 0), (0, HEAD_PAD - MLSTM_DK))).reshape(D, H * HEAD_PAD)

    c = 0
    w_q = pad_heads(w_in[:, c:c + qw]); c += qw
    w_k = pad_heads(w_in[:, c:c + qw]); c += qw
    w_v = w_in[:, c:c + vw]; c += vw
    w_o = w_in[:, c:c + vw]; c += vw
    w_g = jnp.pad(w_in[:, c:c + 2 * H], ((0, 0), (0, LANES - 2 * H))); c += 2 * H
    w_gu = w_in[:, c:c + gw]; c += gw
    w_gv = w_in[:, c:c + gw]; c += gw
    w_pack = jnp.concatenate([w_q, w_k, w_v, w_o, w_gu, w_gv, w_g], axis=1).astype(BF16)
    hp = H * HEAD_PAD
    cols = (0, hp, 2 * hp, 2 * hp + vw, 2 * hp + 2 * vw, 2 * hp + 2 * vw + gw, 2 * hp + 2 * vw + 2 * gw)
    n_cols = w_pack.shape[1]
    bg = jnp.pad(b_gates, (0, LANES - 2 * H)).reshape(1, LANES)
    bsp = jnp.broadcast_to(b_spatial[:, :, None], (G, L, GMLP_DG))

    const = lambda *shape: pl.BlockSpec(shape, lambda b, j: (0,) * len(shape))
    return pl.pallas_call(
        functools.partial(_mixer_kernel, n_chunks=ts // L, cols=cols),
        out_shape=jax.ShapeDtypeStruct((B, S, D), F32),
        grid=(B, S // ts),
        in_specs=[
            pl.BlockSpec((1, ts, D), lambda b, j: (b, j, 0)),
            const(1, D), const(D, n_cols), const(1, LANES), const(1, vw),
            const(G, L, L), const(G, L, GMLP_DG), const(1, gw), const(1, gw), const(vw + gw, D),
        ],
        out_specs=pl.BlockSpec((1, ts, D), lambda b, j: (b, j, 0)),
        scratch_shapes=[
            pltpu.VMEM((ts, n_cols), F32),
            pltpu.VMEM((ts, vw + gw), BF16),
            pltpu.VMEM((H, HEAD_PAD, 2 * MLSTM_DV), F32),
            pltpu.VMEM((H, 8, LANES), F32),
        ],
        compiler_params=pltpu.CompilerParams(
            dimension_semantics=("arbitrary", "arbitrary"), vmem_limit_bytes=VMEM_LIMIT),
        name="mixer",
    )(x, g_mix.reshape(1, D), w_pack, bg, g_head.reshape(1, vw), w_spatial, bsp,
      g_zv.reshape(1, gw), g_zo.reshape(1, gw), w_out.astype(BF16))


def _kv_kernel(mem_ref, g_ref, w_ref, k_ref, v_ref):
    D = mem_ref.shape[-1]
    kv = _dot(_rms(mem_ref[0], g_ref[...]).astype(BF16), w_ref[...])
    k_ref[0] = kv[:, :D].astype(BF16)
    v_ref[0] = kv[:, D:].astype(BF16)


def _kv_proj(mem, g_mem, w_kv):
    B, M, D = mem.shape
    return pl.pallas_call(
        _kv_kernel,
        out_shape=(jax.ShapeDtypeStruct((B, M, D), BF16), jax.ShapeDtypeStruct((B, M, D), BF16)),
        grid=(B,),
        in_specs=[
            pl.BlockSpec((1, M, D), lambda b: (b, 0, 0)),
            pl.BlockSpec((1, D), lambda b: (0, 0)),
            pl.BlockSpec((D, 2 * D), lambda b: (0, 0)),
        ],
        out_specs=(pl.BlockSpec((1, M, D), lambda b: (b, 0, 0)), pl.BlockSpec((1, M, D), lambda b: (b, 0, 0))),
        compiler_params=pltpu.CompilerParams(dimension_semantics=("arbitrary",), vmem_limit_bytes=VMEM_LIMIT),
        name="kv_proj",
    )(mem, g_mem.reshape(1, D), w_kv.astype(BF16))


def _xattn_kernel(x_ref, gx_ref, wq_ref, k_ref, v_ref, wxo_ref, gmoe_ref, wr_ref, br_ref,
                  x2_ref, xn_ref, idx_ref, gate_ref, rank_ref, cnt_ref, o_scr, cnt_scr):
    ts, D = x_ref.shape[1], x_ref.shape[2]
    hd = D // XATTN_HEADS
    E = wr_ref.shape[0]

    @pl.when((pl.program_id(0) == 0) & (pl.program_id(1) == 0))
    def _():
        cnt_scr[...] = jnp.zeros_like(cnt_scr)

    x = x_ref[0]
    q = _dot(_rms(x, gx_ref[...]).astype(BF16), wq_ref[...]) * (hd ** -0.5)
    for h in range(XATTN_HEADS):
        sl = slice(h * hd, (h + 1) * hd)
        s = _dot_nt(q[:, sl].astype(BF16), k_ref[0, :, sl])
        p = jnp.exp(s - jnp.max(s, axis=1, keepdims=True))
        p = p / jnp.sum(p, axis=1, keepdims=True)
        o_scr[:, sl] = _dot(p.astype(BF16), v_ref[0, :, sl]).astype(BF16)
    x2 = x + _dot(o_scr[...], wxo_ref[...])
    x2_ref[0] = x2

    xn = _rms(x2, gmoe_ref[...])
    _store_row_tiles(xn_ref, xn)
    xh, xm, xl = _split3(xn)
    wh, wm, wl = _split3(wr_ref[...])
    logits = (_dot_nt(wh, xh) + (_dot_nt(wh, xm) + _dot_nt(wm, xh))
              + (_dot_nt(wh, xl) + _dot_nt(wm, xm) + _dot_nt(wl, xh))) + br_ref[...]

    e_iota = lax.broadcasted_iota(jnp.int32, (E, ts), 0)
    work = logits
    tops, idxs, hots = [], [], []
    for _ in range(TOP_K):
        m = jnp.max(work, axis=0, keepdims=True)
        i = jnp.min(jnp.where(work == m, e_iota, E), axis=0, keepdims=True)
        hot = e_iota == i
        work = jnp.where(hot, -jnp.inf, work)
        tops.append(m); idxs.append(i); hots.append(hot)
    ex = [jnp.exp(t - tops[0]) for t in tops]
    tot = ex[0] + ex[1] + ex[2] + ex[3]
    gate_ref[...] = jnp.concatenate([e / tot for e in ex], axis=0)
    idx_ref[...] = jnp.concatenate(idxs, axis=0)

    hot_all = jnp.where(hots[0] | hots[1] | hots[2] | hots[3], 1.0, 0.0)
    r = lax.broadcasted_iota(jnp.int32, (ts, ts), 0)
    c = lax.broadcasted_iota(jnp.int32, (ts, ts), 1)
    before = jnp.where(r < c, 1.0, 0.0).astype(BF16)
    base = cnt_scr[:, 0:1] + _dot(hot_all.astype(BF16), before)
    ranks = [jnp.sum(jnp.where(hot, base, 0.0), axis=0, keepdims=True) for hot in hots]
    rank_ref[...] = jnp.concatenate(ranks, axis=0).astype(jnp.int32)
    cnt_new = cnt_scr[...] + jnp.sum(hot_all, axis=1, keepdims=True)
    cnt_scr[...] = cnt_new
    cnt_ref[...] = cnt_new


def _xattn_router(x, k, v, g_xattn, w_q, w_xo, g_moe, w_router, b_router):
    B, S, D = x.shape
    M = k.shape[1]
    E = w_router.shape[1]
    ts = min(XATTN_TOKENS, S)
    assert S % ts == 0
    nj = S // ts
    T = B * S
    const = lambda *shape: pl.BlockSpec(shape, lambda b, j: (0,) * len(shape))
    tok = lambda rows: pl.BlockSpec((rows, ts), lambda b, j: (0, b * nj + j))
    return pl.pallas_call(
        _xattn_kernel,
        out_shape=(
            jax.ShapeDtypeStruct((B, S, D), F32),
            jax.ShapeDtypeStruct((T * ROW_TILE, LANES), F32),
            jax.ShapeDtypeStruct((TOP_K, T), jnp.int32),
            jax.ShapeDtypeStruct((TOP_K, T), F32),
            jax.ShapeDtypeStruct((TOP_K, T), jnp.int32),
            jax.ShapeDtypeStruct((E, LANES), F32),
        ),
        grid=(B, nj),
        in_specs=[
            pl.BlockSpec((1, ts, D), lambda b, j: (b, j, 0)),
            const(1, D), const(D, D),
            pl.BlockSpec((1, M, D), lambda b, j: (b, 0, 0)),
            pl.BlockSpec((1, M, D), lambda b, j: (b, 0, 0)),
            const(D, D), const(1, D), const(E, D), const(E, 1),
        ],
        out_specs=(
            pl.BlockSpec((1, ts, D), lambda b, j: (b, j, 0)),
            pl.BlockSpec((ts * ROW_TILE, LANES), lambda b, j: (b * nj + j, 0)),
            tok(TOP_K), tok(TOP_K), tok(TOP_K),
            const(E, LANES),
        ),
        scratch_shapes=[pltpu.VMEM((ts, D), BF16), pltpu.VMEM((E, LANES), F32)],
        compiler_params=pltpu.CompilerParams(
            dimension_semantics=("arbitrary", "arbitrary"), vmem_limit_bytes=VMEM_LIMIT),
        name="xattn_router",
    )(x, g_xattn.reshape(1, D), w_q.astype(BF16), k, v, w_xo.astype(BF16), g_moe.reshape(1, D),
      w_router.T, b_router.reshape(E, 1))


def _dispatch_kernel(dest_ref, cnt_ref, pend_ref, xn_ref, xs_ref, zero_scr, sem, zsem, *, n_experts):
    tt = xn_ref.shape[0] // ROW_TILE
    R = zero_scr.shape[0] // ROW_TILE

    @pl.when(pl.program_id(0) == 0)
    def _():
        zero_scr[...] = jnp.zeros_like(zero_scr)

        def fill(e, carry):
            @pl.when(cnt_ref[e] > 0)
            def _():
                start = pl.multiple_of((pend_ref[e] - R) * ROW_TILE, R * ROW_TILE)
                cp = pltpu.make_async_copy(zero_scr, xs_ref.at[pl.ds(start, R * ROW_TILE)], zsem)
                cp.start()
                cp.wait()
            return carry

        lax.fori_loop(0, n_experts, fill, 0)

    def issue(grp, carry):
        t0 = pl.multiple_of(grp * ROW_UNROLL, ROW_UNROLL)
        for kk in range(TOP_K):
            for u in range(ROW_UNROLL):
                dst = pl.multiple_of(dest_ref[0, 0, kk * tt + t0 + u] * ROW_TILE, ROW_TILE)
                row = pl.multiple_of((t0 + u) * ROW_TILE, ROW_TILE)
                pltpu.make_async_copy(xn_ref.at[pl.ds(row, ROW_TILE)], xs_ref.at[pl.ds(dst, ROW_TILE)],
                                      sem).start(priority=u % 2)
        return carry

    lax.fori_loop(0, tt // ROW_UNROLL, issue, 0)
    for kk in range(TOP_K):
        pltpu.make_async_copy(xn_ref, xs_ref.at[pl.ds(0, tt * ROW_TILE)], sem).wait()


def _dispatch(xn, dest_blocks, counts, pad_end, n_slots):
    nb, _, per = dest_blocks.shape
    tt = per // TOP_K
    E = counts.shape[0]
    return pl.pallas_call(
        functools.partial(_dispatch_kernel, n_experts=E),
        out_shape=jax.ShapeDtypeStruct((n_slots * ROW_TILE, LANES), F32),
        grid=(nb,),
        in_specs=[
            pl.BlockSpec((1, 1, per), lambda i: (i, 0, 0), memory_space=pltpu.SMEM),
            pl.BlockSpec(memory_space=pltpu.SMEM),
            pl.BlockSpec(memory_space=pltpu.SMEM),
            pl.BlockSpec((tt * ROW_TILE, LANES), lambda i: (i, 0)),
        ],
        out_specs=pl.BlockSpec(memory_space=pl.ANY),
        scratch_shapes=[pltpu.VMEM((MOE_ROWS * ROW_TILE, LANES), F32), pltpu.SemaphoreType.DMA,
                        pltpu.SemaphoreType.DMA],
        compiler_params=pltpu.CompilerParams(
            dimension_semantics=("arbitrary",), vmem_limit_bytes=VMEM_LIMIT, has_side_effects=True),
        name="dispatch",
    )(dest_blocks, counts, pad_end, xn)


def _expert_kernel(be_ref, nu_ref, xs_ref, wg_ref, bg_ref, wu_ref, bu_ref, wd_ref, bd_ref, y_ref):
    @pl.when(pl.program_id(0) < nu_ref[0])
    def _():
        xb = _load_row_tiles(xs_ref, xs_ref.shape[0] // ROW_TILE).astype(BF16)
        g = jnp.minimum(_dot(xb, wg_ref[0]) + bg_ref[0], SWIGLU_LIMIT)
        u = jnp.clip(_dot(xb, wu_ref[0]) + bu_ref[0], -SWIGLU_LIMIT, SWIGLU_LIMIT)
        hdn = (u + 1.0) * (g * jax.nn.sigmoid(SWIGLU_ALPHA * g))
        _store_row_tiles(y_ref, _dot(hdn.astype(BF16), wd_ref[0]) + bd_ref[0])


def _experts(xs, block_e, n_used, w_gate, b_gate, w_up, b_up, w_down, b_down):
    E, D, F = w_gate.shape
    assert D == ROW_TILE * LANES
    n_slots = xs.shape[0] // ROW_TILE
    R = MOE_ROWS
    n_blocks = n_slots // R
    blk = lambda j, be, nu: (jnp.minimum(j, nu[0] - 1), 0)
    wsel = lambda j, be, nu: (be[jnp.minimum(j, nu[0] - 1)], 0, 0)
    grid_spec = pltpu.PrefetchScalarGridSpec(
        num_scalar_prefetch=2,
        grid=(n_blocks,),
        in_specs=[
            pl.BlockSpec((R * ROW_TILE, LANES), blk),
            pl.BlockSpec((1, D, F), wsel), pl.BlockSpec((1, 1, F), wsel),
            pl.BlockSpec((1, D, F), wsel), pl.BlockSpec((1, 1, F), wsel),
            pl.BlockSpec((1, F, D), wsel), pl.BlockSpec((1, 1, D), wsel),
        ],
        out_specs=pl.BlockSpec((R * ROW_TILE, LANES), blk),
    )
    return pl.pallas_call(
        _expert_kernel,
        out_shape=jax.ShapeDtypeStruct((n_slots * ROW_TILE, LANES), F32),
        grid_spec=grid_spec,
        compiler_params=pltpu.CompilerParams(dimension_semantics=("arbitrary",), vmem_limit_bytes=VMEM_LIMIT),
        name="experts",
    )(block_e, n_used, xs, w_gate.astype(BF16), b_gate.reshape(E, 1, F), w_up.astype(BF16), b_up.reshape(E, 1, F),
      w_down.astype(BF16), b_down.reshape(E, 1, D))


def _combine_kernel(dcur_ref, dnext_ref, x_ref, gate_ref, gfin_ref, ys_ref, o_ref, buf, sem):
    tc = x_ref.shape[0]
    i = pl.program_id(0)
    slot = i % 2

    def issue_all(d_ref, into):
        def issue(grp, carry):
            t0 = pl.multiple_of(grp * ROW_UNROLL, ROW_UNROLL)
            for kk in range(TOP_K):
                for u in range(ROW_UNROLL):
                    src = pl.multiple_of(d_ref[0, 0, kk * tc + t0 + u] * ROW_TILE, ROW_TILE)
                    row = pl.multiple_of((t0 + u) * ROW_TILE, ROW_TILE)
                    pltpu.make_async_copy(ys_ref.at[pl.ds(src, ROW_TILE)], buf.at[into, kk, pl.ds(row, ROW_TILE)],
                                          sem.at[into]).start(priority=u % 2)
            return carry

        lax.fori_loop(0, tc // ROW_UNROLL, issue, 0)

    @pl.when(i == 0)
    def _():
        issue_all(dcur_ref, 0)

    @pl.when(i + 1 < pl.num_programs(0))
    def _():
        issue_all(dnext_ref, 1 - slot)

    for kk in range(TOP_K):
        pltpu.make_async_copy(ys_ref.at[pl.ds(0, tc * ROW_TILE)], buf.at[slot, kk], sem.at[slot]).wait()
    acc = x_ref[...]
    for kk in range(TOP_K):
        acc = acc + gate_ref[:, kk:kk + 1] * _load_row_tiles(buf.at[slot, kk], tc)
    o_ref[...] = _rms(acc, gfin_ref[...])


def _combine(x2, gate_t, dest_blocks, ys, g_final):
    T, D = x2.shape
    nb, _, per = dest_blocks.shape
    tc = per // TOP_K
    return pl.pallas_call(
        _combine_kernel,
        out_shape=jax.ShapeDtypeStruct((T, D), F32),
        grid=(nb,),
        in_specs=[
            pl.BlockSpec((1, 1, per), lambda i: (i, 0, 0), memory_space=pltpu.SMEM),
            pl.BlockSpec((1, 1, per), lambda i: (jnp.minimum(i + 1, nb - 1), 0, 0), memory_space=pltpu.SMEM),
            pl.BlockSpec((tc, D), lambda i: (i, 0)),
            pl.BlockSpec((tc, TOP_K), lambda i: (i, 0)),
            pl.BlockSpec((1, D), lambda i: (0, 0)),
            pl.BlockSpec(memory_space=pl.ANY),
        ],
        out_specs=pl.BlockSpec((tc, D), lambda i: (i, 0)),
        scratch_shapes=[pltpu.VMEM((2, TOP_K, tc * ROW_TILE, LANES), F32), pltpu.SemaphoreType.DMA((2,))],
        compiler_params=pltpu.CompilerParams(dimension_semantics=("arbitrary",), vmem_limit_bytes=VMEM_LIMIT),
        name="combine",
    )(dest_blocks, dest_blocks, x2, gate_t, g_final.reshape(1, D), ys)


def _block_dest(dest, tokens):
    T = dest.shape[1]
    return dest.reshape(TOP_K, T // tokens, tokens).transpose(1, 0, 2).reshape(T // tokens, 1, TOP_K * tokens)


def _moe(x2, xn, idx, gate, rank, cnt, w_gate, b_gate, w_up, b_up, w_down, b_down, g_final):
    T = x2.shape[0]
    E = w_gate.shape[0]
    R = MOE_ROWS
    n_blocks = -(-T * TOP_K // R) + E
    n_slots = n_blocks * R
    counts = cnt[:, 0].astype(jnp.int32)
    padded = (counts + R - 1) // R * R
    pad_end = jnp.cumsum(padded)
    pad_start = pad_end - padded
    dest = rank
    for e in range(E):
        dest = dest + jnp.where(idx == e, pad_start[e], 0)
    n_used = (pad_end[-1:] // R).astype(jnp.int32)
    block_e = jnp.minimum(jnp.sum(jnp.arange(n_blocks)[:, None] * R >= pad_end[None, :], axis=1), E - 1)
    block_e = block_e.astype(jnp.int32)

    xs = _dispatch(xn, _block_dest(dest, min(DISPATCH_TOKENS, T)), counts, pad_end.astype(jnp.int32), n_slots)
    ys = _experts(xs, block_e, n_used, w_gate, b_gate, w_up, b_up, w_down, b_down)
    return _combine(x2, gate.T, _block_dest(dest, min(COMBINE_TOKENS, T)), ys, g_final)


def kernel(x, mem, g_mix, w_in, b_gates, g_mlstm_head, w_spatial, b_spatial, g_gmlp_v, g_gmlp_out, w_out,
           g_xattn, g_mem, w_q, w_kv, w_xo, g_moe, w_router, b_router, w_gate, b_gate, w_up, b_up,
           w_down, b_down, g_final):
    B, S, D = x.shape
    assert g_mix.shape[0] == 1, "the combine kernel fuses the closing norm, so exactly one layer is supported"
    x1 = _mixer(x, g_mix[0], w_in[0], b_gates[0], g_mlstm_head[0], w_spatial[0], b_spatial[0],
                g_gmlp_v[0], g_gmlp_out[0], w_out[0])
    k, v = _kv_proj(mem, g_mem[0], w_kv[0])
    x2, xn, idx, gate, rank, cnt = _xattn_router(x1, k, v, g_xattn[0], w_q[0], w_xo[0], g_moe[0],
                                                 w_router[0], b_router[0])
    out = _moe(x2.reshape(B * S, D), xn, idx, gate, rank, cnt, w_gate[0], b_gate[0], w_up[0], b_up[0],
               w_down[0], b_down[0], g_final)
    return out.reshape(B, S, D)
```

```python
import functools

import jax
import jax.numpy as jnp
from jax import lax
from jax.experimental import pallas as pl
from jax.experimental.pallas import tpu as pltpu

F32 = jnp.float32
BF16 = jnp.bfloat16

MLSTM_HEADS = 4
MLSTM_DK = 64
MLSTM_DV = 128
CHUNK = 128
GMLP_GROUPS = 4
GMLP_DG = 128
XATTN_HEADS = 4
TOP_K = 4
GATE_SOFTCAP = 15.0
SWIGLU_LIMIT = 7.0
SWIGLU_ALPHA = 1.702
NORM_EPS = 1e-6

LANES = 128
ROW_TILE = 8

MIX_TOKENS = 512
XATTN_TOKENS = 512
MOE_ROWS = 512
DISPATCH_TOKENS = 1024
COMBINE_TOKENS = 256
COMBINE_ROWS = 32
ROW_UNROLL = 16
VMEM_LIMIT = 52 * 1024 * 1024


def _rms(x, g):
    return x * lax.rsqrt(jnp.mean(x * x, axis=-1, keepdims=True) + NORM_EPS) * g


def _gelu_tanh(x):
    return 0.5 * x * (1.0 + jnp.tanh(0.7978845608028654 * (x + 0.044715 * x * x * x)))


def _dot(a, b):
    return jnp.dot(a, b, preferred_element_type=F32)


def _dot_nt(a, b):
    return lax.dot_general(a, b, (((1,), (1,)), ((), ())), preferred_element_type=F32)


def _load_row_tiles(ref, rows, first=0):
    return jnp.concatenate(
        [ref[pl.ds(first * ROW_TILE + j, rows, stride=ROW_TILE), :] for j in range(ROW_TILE)], axis=1)


def _store_row_tiles(ref, val):
    for j in range(ROW_TILE):
        ref[pl.ds(j, val.shape[0], stride=ROW_TILE), :] = val[:, j * LANES:(j + 1) * LANES]


def _split3(x):
    hi = x.astype(BF16)
    r1 = x - hi.astype(F32)
    mid = r1.astype(BF16)
    lo = (r1 - mid.astype(F32)).astype(BF16)
    return hi, mid, lo


def _mixer_kernel(x_ref, gmix_ref, win_ref, bg_ref, ghead_ref, wsp_ref, bsp_ref, gzv_ref, gzo_ref, wout_ref,
                  o_ref, proj_scr, y_scr, ct_scr, m_scr, *, n_chunks, cols):
    qk0, v0, og0, gu0, gv0, gt0 = cols
    H, G, L = MLSTM_HEADS, GMLP_GROUPS, CHUNK

    @pl.when(pl.program_id(1) == 0)
    def _():
        ct_scr[...] = jnp.zeros_like(ct_scr)
        m_scr[...] = jnp.zeros_like(m_scr)

    row = lax.broadcasted_iota(jnp.int32, (L, L), 0)
    col = lax.broadcasted_iota(jnp.int32, (L, L), 1)
    causal = row >= col
    tril = jnp.where(causal, 1.0, 0.0).astype(BF16)
    triu = jnp.where(row <= col, 1.0, 0.0).astype(BF16)
    lane = lax.broadcasted_iota(jnp.int32, (L, LANES), 1)
    ones_col = jnp.where(lane == 0, 1.0, 0.0).astype(F32)

    x = x_ref[0]
    proj_scr[...] = _dot(_rms(x, gmix_ref[...]).astype(BF16), win_ref[...])

    for c in range(n_chunks):
        rows = slice(c * L, (c + 1) * L)
        pre = proj_scr[rows, gt0:gt0 + LANES] + bg_ref[...]
        capped = GATE_SOFTCAP * jnp.tanh(pre * (1.0 / GATE_SOFTCAP))
        log_sig = jnp.minimum(capped, 0.0) - jnp.log1p(jnp.exp(-jnp.abs(capped)))
        lg = jnp.where(lane < H, capped, log_sig)
        lg_t = lg.T[0:8, :]
        bcol = sum(_dot(tril, p) for p in _split3(lg))
        brow = sum(_dot(p, triu) for p in _split3(lg_t))

        for h in range(H):
            b_c = bcol[:, H + h:H + h + 1]
            i_c = lg[:, h:h + 1]
            b_r = brow[H + h:H + h + 1, :]
            i_r = lg_t[h:h + 1, :]
            m_prev = m_scr[h, 0:1, 0:1]
            qk = proj_scr[rows, qk0 + h * LANES:qk0 + (h + 1) * LANES]
            q = jnp.where(lane < MLSTM_DK, qk * (MLSTM_DK ** -0.5), 0.0)
            k = jnp.where(lane < MLSTM_DK, pltpu.roll(qk, MLSTM_DK, axis=1), 0.0)
            v = proj_scr[rows, v0 + h * MLSTM_DV:v0 + (h + 1) * MLSTM_DV]
            qb = q.astype(BF16)
            kb = k.astype(BF16)
            vaug = jnp.concatenate([v, ones_col], axis=1)

            d = jnp.where(causal, b_c - b_r + i_r, -jnp.inf)
            inter = b_c + m_prev
            m_t = jnp.maximum(inter, jnp.max(d, axis=1, keepdims=True))
            w_intra = jnp.exp(d - m_t)
            w_inter = jnp.exp(inter - m_t)
            s = _dot_nt(qb, kb) * w_intra
            ct = ct_scr[h]
            na = _dot(s.astype(BF16), vaug.astype(BF16)) + w_inter * _dot(qb, ct.astype(BF16))
            num = na[:, :MLSTM_DV]
            den = na[:, MLSTM_DV:MLSTM_DV + 1]
            hh = num / jnp.maximum(jnp.abs(den), jnp.exp(-m_t))

            b_last = b_c[L - 1:L, :]
            g_c = b_last - b_c + i_c
            m_new = jnp.maximum(b_last + m_prev, jnp.max(g_c, axis=0, keepdims=True))
            wk = jnp.exp(g_c - m_new)
            decay = jnp.exp(b_last + m_prev - m_new)
            ct_scr[h] = decay * ct + _dot(k.T.astype(BF16), (wk * vaug).astype(BF16))
            m_scr[h] = jnp.broadcast_to(m_new, m_scr.shape[1:])

            hn = _rms(hh, ghead_ref[:, h * MLSTM_DV:(h + 1) * MLSTM_DV])
            og = proj_scr[rows, og0 + h * MLSTM_DV:og0 + (h + 1) * MLSTM_DV]
            y_scr[rows, h * MLSTM_DV:(h + 1) * MLSTM_DV] = (jax.nn.sigmoid(og) * hn).astype(BF16)

        for g in range(G):
            sl = slice(g * GMLP_DG, (g + 1) * GMLP_DG)
            u = _gelu_tanh(proj_scr[rows, gu0 + g * GMLP_DG:gu0 + (g + 1) * GMLP_DG])
            z = _rms(_gelu_tanh(proj_scr[rows, gv0 + g * GMLP_DG:gv0 + (g + 1) * GMLP_DG]), gzv_ref[:, sl])
            wsp = jnp.where(causal, wsp_ref[g], 0.0)
            mixed = _dot(wsp.astype(BF16), z.astype(BF16)) + bsp_ref[g]
            yg = _rms(u * mixed, gzo_ref[:, sl])
            y_scr[rows, H * MLSTM_DV + g * GMLP_DG:H * MLSTM_DV + (g + 1) * GMLP_DG] = yg.astype(BF16)
    o_ref[0] = x + _dot(y_scr[...], wout_ref[...])


def _mixer(x, g_mix, w_in, b_gates, g_head, w_spatial, b_spatial, g_zv, g_zo, w_out):
    B, S, D = x.shape
    H, G, L = MLSTM_HEADS, GMLP_GROUPS, CHUNK
    qw, vw, gw = H * MLSTM_DK, H * MLSTM_DV, G * GMLP_DG
    ts = min(MIX_TOKENS, S)
    assert S % ts == 0 and ts % L == 0

    assert 2 * MLSTM_DK == LANES
    c = 0
    w_q = w_in[:, c:c + qw].reshape(D, H, MLSTM_DK); c += qw
    w_k = w_in[:, c:c + qw].reshape(D, H, MLSTM_DK); c += qw
    w_qk = jnp.concatenate([w_q, w_k], axis=2).reshape(D, H * LANES)
    w_v = w_in[:, c:c + vw]; c += vw
    w_o = w_in[:, c:c + vw]; c += vw
    w_g = jnp.pad(w_in[:, c:c + 2 * H], ((0, 0), (0, LANES - 2 * H))); c += 2 * H
    w_gu = w_in[:, c:c + gw]; c += gw
    w_gv = w_in[:, c:c + gw]; c += gw
    w_pack = jnp.concatenate([w_qk, w_v, w_o, w_gu, w_gv, w_g], axis=1).astype(BF16)
    hp = H * LANES
    cols = (0, hp, hp + vw, hp + 2 * vw, hp + 2 * vw + gw, hp + 2 * vw + 2 * gw)
    n_cols = w_pack.shape[1]
    bg = jnp.pad(b_gates, (0, LANES - 2 * H)).reshape(1, LANES)
    bsp = jnp.broadcast_to(b_spatial[:, :, None], (G, L, GMLP_DG))

    const = lambda *shape: pl.BlockSpec(shape, lambda b, j: (0,) * len(shape))
    return pl.pallas_call(
        functools.partial(_mixer_kernel, n_chunks=ts // L, cols=cols),
        out_shape=jax.ShapeDtypeStruct((B, S, D), F32),
        grid=(B, S // ts),
        in_specs=[
            pl.BlockSpec((1, ts, D), lambda b, j: (b, j, 0)),
            const(1, D), const(D, n_cols), const(1, LANES), const(1, vw),
            const(G, L, L), const(G, L, GMLP_DG), const(1, gw), const(1, gw), const(vw + gw, D),
        ],
        out_specs=pl.BlockSpec((1, ts, D), lambda b, j: (b, j, 0)),
        scratch_shapes=[
            pltpu.VMEM((ts, n_cols), F32),
            pltpu.VMEM((ts, vw + gw), BF16),
            pltpu.VMEM((H, LANES, 2 * MLSTM_DV), F32),
            pltpu.VMEM((H, 8, LANES), F32),
        ],
        compiler_params=pltpu.CompilerParams(
            dimension_semantics=("arbitrary", "arbitrary"), vmem_limit_bytes=VMEM_LIMIT),
        name="mixer",
    )(x, g_mix.reshape(1, D), w_pack, bg, g_head.reshape(1, vw), w_spatial, bsp,
      g_zv.reshape(1, gw), g_zo.reshape(1, gw), w_out.astype(BF16))


def _kv_kernel(mem_ref, g_ref, w_ref, k_ref, v_ref):
    D = mem_ref.shape[-1]
    kv = _dot(_rms(mem_ref[0], g_ref[...]).astype(BF16), w_ref[...])
    k_ref[0] = kv[:, :D].astype(BF16)
    v_ref[0] = kv[:, D:].astype(BF16)


def _kv_proj(mem, g_mem, w_kv):
    B, M, D = mem.shape
    return pl.pallas_call(
        _kv_kernel,
        out_shape=(jax.ShapeDtypeStruct((B, M, D), BF16), jax.ShapeDtypeStruct((B, M, D), BF16)),
        grid=(B,),
        in_specs=[
            pl.BlockSpec((1, M, D), lambda b: (b, 0, 0)),
            pl.BlockSpec((1, D), lambda b: (0, 0)),
            pl.BlockSpec((D, 2 * D), lambda b: (0, 0)),
        ],
        out_specs=(pl.BlockSpec((1, M, D), lambda b: (b, 0, 0)), pl.BlockSpec((1, M, D), lambda b: (b, 0, 0))),
        compiler_params=pltpu.CompilerParams(dimension_semantics=("arbitrary",), vmem_limit_bytes=VMEM_LIMIT),
        name="kv_proj",
    )(mem, g_mem.reshape(1, D), w_kv.astype(BF16))


def _xattn_kernel(x_ref, gx_ref, wq_ref, k_ref, v_ref, wxo_ref, gmoe_ref, wr_ref, br_ref,
                  x2_ref, xn_ref, idx_ref, gate_ref, rank_ref, cnt_ref, o_scr, cnt_scr):
    ts, D = x_ref.shape[1], x_ref.shape[2]
    hd = D // XATTN_HEADS
    E = wr_ref.shape[0]

    @pl.when((pl.program_id(0) == 0) & (pl.program_id(1) == 0))
    def _():
        cnt_scr[...] = jnp.zeros_like(cnt_scr)

    x = x_ref[0]
    q = _dot(_rms(x, gx_ref[...]).astype(BF16), wq_ref[...]) * (hd ** -0.5)
    for h in range(XATTN_HEADS):
        sl = slice(h * hd, (h + 1) * hd)
        s = _dot_nt(q[:, sl].astype(BF16), k_ref[0, :, sl])
        p = jnp.exp(s - jnp.max(s, axis=1, keepdims=True))
        p = p / jnp.sum(p, axis=1, keepdims=True)
        o_scr[:, sl] = _dot(p.astype(BF16), v_ref[0, :, sl]).astype(BF16)
    x2 = x + _dot(o_scr[...], wxo_ref[...])
    x2_ref[0] = x2

    xn = _rms(x2, gmoe_ref[...])
    xh, xm, _ = _split3(xn)
    wh, wm, _ = _split3(wr_ref[...])
    logits = _dot_nt(wh, xh) + (_dot_nt(wh, xm) + _dot_nt(wm, xh)) + br_ref[...]

    e_iota = lax.broadcasted_iota(jnp.int32, (E, ts), 0)
    work = logits
    tops, idxs, hots = [], [], []
    for _ in range(TOP_K):
        m = jnp.max(work, axis=0, keepdims=True)
        i = jnp.min(jnp.where(work == m, e_iota, E), axis=0, keepdims=True)
        hot = e_iota == i
        work = jnp.where(hot, -jnp.inf, work)
        tops.append(m); idxs.append(i); hots.append(hot)
    ex = [jnp.exp(t - tops[0]) for t in tops]
    tot = ex[0] + ex[1] + ex[2] + ex[3]
    gate_ref[...] = jnp.concatenate([e / tot for e in ex], axis=0)
    idx_ref[...] = jnp.concatenate(idxs, axis=0)

    hot_all = jnp.where(hots[0] | hots[1] | hots[2] | hots[3], 1.0, 0.0)
    r = lax.broadcasted_iota(jnp.int32, (ts, ts), 0)
    c = lax.broadcasted_iota(jnp.int32, (ts, ts), 1)
    before = jnp.where(r < c, 1.0, 0.0).astype(BF16)
    base = cnt_scr[:, 0:1] + _dot(hot_all.astype(BF16), before)
    ranks = [jnp.sum(jnp.where(hot, base, 0.0), axis=0, keepdims=True) for hot in hots]
    rank_ref[...] = jnp.concatenate(ranks, axis=0).astype(jnp.int32)
    cnt_new = cnt_scr[...] + jnp.sum(hot_all, axis=1, keepdims=True)
    cnt_scr[...] = cnt_new
    cnt_ref[...] = cnt_new
    _store_row_tiles(xn_ref, xn)


def _xattn_router(x, k, v, g_xattn, w_q, w_xo, g_moe, w_router, b_router):
    B, S, D = x.shape
    M = k.shape[1]
    E = w_router.shape[1]
    ts = min(XATTN_TOKENS, S)
    assert S % ts == 0
    nj = S // ts
    T = B * S
    const = lambda *shape: pl.BlockSpec(shape, lambda b, j: (0,) * len(shape))
    tok = lambda rows: pl.BlockSpec((rows, ts), lambda b, j: (0, b * nj + j))
    return pl.pallas_call(
        _xattn_kernel,
        out_shape=(
            jax.ShapeDtypeStruct((B, S, D), F32),
            jax.ShapeDtypeStruct((T * ROW_TILE, LANES), F32),
            jax.ShapeDtypeStruct((TOP_K, T), jnp.int32),
            jax.ShapeDtypeStruct((TOP_K, T), F32),
            jax.ShapeDtypeStruct((TOP_K, T), jnp.int32),
            jax.ShapeDtypeStruct((E, LANES), F32),
        ),
        grid=(B, nj),
        in_specs=[
            pl.BlockSpec((1, ts, D), lambda b, j: (b, j, 0)),
            const(1, D), const(D, D),
            pl.BlockSpec((1, M, D), lambda b, j: (b, 0, 0)),
            pl.BlockSpec((1, M, D), lambda b, j: (b, 0, 0)),
            const(D, D), const(1, D), const(E, D), const(E, 1),
        ],
        out_specs=(
            pl.BlockSpec((1, ts, D), lambda b, j: (b, j, 0)),
            pl.BlockSpec((ts * ROW_TILE, LANES), lambda b, j: (b * nj + j, 0)),
            tok(TOP_K), tok(TOP_K), tok(TOP_K),
            const(E, LANES),
        ),
        scratch_shapes=[pltpu.VMEM((ts, D), BF16), pltpu.VMEM((E, LANES), F32)],
        compiler_params=pltpu.CompilerParams(
            dimension_semantics=("arbitrary", "arbitrary"), vmem_limit_bytes=VMEM_LIMIT),
        name="xattn_router",
    )(x, g_xattn.reshape(1, D), w_q.astype(BF16), k, v, w_xo.astype(BF16), g_moe.reshape(1, D),
      w_router.T, b_router.reshape(E, 1))


def _dispatch_kernel(dest_ref, cnt_ref, pend_ref, xn_ref, xs_ref, zero_scr, sem, zsem, *, n_experts):
    tt = xn_ref.shape[0] // ROW_TILE
    R = zero_scr.shape[0] // ROW_TILE

    @pl.when(pl.program_id(0) == 0)
    def _():
        zero_scr[...] = jnp.zeros_like(zero_scr)

        def fill(e, carry):
            @pl.when(cnt_ref[e] > 0)
            def _():
                start = pl.multiple_of((pend_ref[e] - R) * ROW_TILE, R * ROW_TILE)
                cp = pltpu.make_async_copy(zero_scr, xs_ref.at[pl.ds(start, R * ROW_TILE)], zsem)
                cp.start()
                cp.wait()
            return carry

        lax.fori_loop(0, n_experts, fill, 0)

    def issue(grp, carry):
        t0 = pl.multiple_of(grp * ROW_UNROLL, ROW_UNROLL)
        for kk in range(TOP_K):
            for u in range(ROW_UNROLL):
                dst = pl.multiple_of(dest_ref[0, 0, kk * tt + t0 + u] * ROW_TILE, ROW_TILE)
                row = pl.multiple_of((t0 + u) * ROW_TILE, ROW_TILE)
                pltpu.make_async_copy(xn_ref.at[pl.ds(row, ROW_TILE)], xs_ref.at[pl.ds(dst, ROW_TILE)],
                                      sem).start(priority=u % 2)
        return carry

    lax.fori_loop(0, tt // ROW_UNROLL, issue, 0)
    for kk in range(TOP_K):
        pltpu.make_async_copy(xn_ref, xs_ref.at[pl.ds(0, tt * ROW_TILE)], sem).wait()


def _dispatch(xn, dest_blocks, counts, pad_end, n_slots):
    nb, _, per = dest_blocks.shape
    tt = per // TOP_K
    E = counts.shape[0]
    return pl.pallas_call(
        functools.partial(_dispatch_kernel, n_experts=E),
        out_shape=jax.ShapeDtypeStruct((n_slots * ROW_TILE, LANES), F32),
        grid=(nb,),
        in_specs=[
            pl.BlockSpec((1, 1, per), lambda i: (i, 0, 0), memory_space=pltpu.SMEM),
            pl.BlockSpec(memory_space=pltpu.SMEM),
            pl.BlockSpec(memory_space=pltpu.SMEM),
            pl.BlockSpec((tt * ROW_TILE, LANES), lambda i: (i, 0)),
        ],
        out_specs=pl.BlockSpec(memory_space=pl.ANY),
        scratch_shapes=[pltpu.VMEM((MOE_ROWS * ROW_TILE, LANES), F32), pltpu.SemaphoreType.DMA,
                        pltpu.SemaphoreType.DMA],
        compiler_params=pltpu.CompilerParams(
            dimension_semantics=("arbitrary",), vmem_limit_bytes=VMEM_LIMIT, has_side_effects=True),
        name="dispatch",
    )(dest_blocks, counts, pad_end, xn)


def _expert_kernel(be_ref, nu_ref, xs_ref, wg_ref, bg_ref, wu_ref, bu_ref, wd_ref, bd_ref, y_ref,
                   wg_scr, wu_scr, wd_scr):
    j = pl.program_id(0)
    active = j < nu_ref[0]

    @pl.when(active & ((j == 0) | (be_ref[j] != be_ref[jnp.maximum(j - 1, 0)])))
    def _():
        wg_scr[...] = wg_ref[0].astype(BF16)
        wu_scr[...] = wu_ref[0].astype(BF16)
        wd_scr[...] = wd_ref[0].astype(BF16)

    @pl.when(active)
    def _():
        xb = _load_row_tiles(xs_ref, xs_ref.shape[0] // ROW_TILE).astype(BF16)
        g = jnp.minimum(_dot(xb, wg_scr[...]) + bg_ref[0], SWIGLU_LIMIT)
        u = jnp.clip(_dot(xb, wu_scr[...]) + bu_ref[0], -SWIGLU_LIMIT, SWIGLU_LIMIT)
        hdn = (u + 1.0) * (g * jax.nn.sigmoid(SWIGLU_ALPHA * g))
        _store_row_tiles(y_ref, _dot(hdn.astype(BF16), wd_scr[...]) + bd_ref[0])


def _experts(xs, block_e, n_used, w_gate, b_gate, w_up, b_up, w_down, b_down):
    E, D, F = w_gate.shape
    assert D == ROW_TILE * LANES
    n_slots = xs.shape[0] // ROW_TILE
    R = MOE_ROWS
    n_blocks = n_slots // R
    blk = lambda j, be, nu: (jnp.minimum(j, nu[0] - 1), 0)
    wsel = lambda j, be, nu: (be[jnp.minimum(j, nu[0] - 1)], 0, 0)
    grid_spec = pltpu.PrefetchScalarGridSpec(
        num_scalar_prefetch=2,
        grid=(n_blocks,),
        in_specs=[
            pl.BlockSpec((R * ROW_TILE, LANES), blk),
            pl.BlockSpec((1, D, F), wsel), pl.BlockSpec((1, 1, F), wsel),
            pl.BlockSpec((1, D, F), wsel), pl.BlockSpec((1, 1, F), wsel),
            pl.BlockSpec((1, F, D), wsel), pl.BlockSpec((1, 1, D), wsel),
        ],
        out_specs=pl.BlockSpec((R * ROW_TILE, LANES), blk),
        scratch_shapes=[pltpu.VMEM((D, F), BF16), pltpu.VMEM((D, F), BF16), pltpu.VMEM((F, D), BF16)],
    )
    return pl.pallas_call(
        _expert_kernel,
        out_shape=jax.ShapeDtypeStruct((n_slots * ROW_TILE, LANES), F32),
        grid_spec=grid_spec,
        compiler_params=pltpu.CompilerParams(dimension_semantics=("arbitrary",), vmem_limit_bytes=VMEM_LIMIT),
        name="experts",
    )(block_e, n_used, xs, w_gate, b_gate.reshape(E, 1, F), w_up, b_up.reshape(E, 1, F),
      w_down, b_down.reshape(E, 1, D))


def _combine_kernel(dcur_ref, dnext_ref, x_ref, gate_ref, gfin_ref, ys_ref, o_ref, buf, sem):
    tc = x_ref.shape[0]
    i = pl.program_id(0)
    slot = i % 2

    def issue_all(d_ref, into):
        def issue(grp, carry):
            t0 = pl.multiple_of(grp * ROW_UNROLL, ROW_UNROLL)
            for kk in range(TOP_K):
                for u in range(ROW_UNROLL):
                    src = pl.multiple_of(d_ref[0, 0, kk * tc + t0 + u] * ROW_TILE, ROW_TILE)
                    row = pl.multiple_of((t0 + u) * ROW_TILE, ROW_TILE)
                    pltpu.make_async_copy(ys_ref.at[pl.ds(src, ROW_TILE)], buf.at[into, kk, pl.ds(row, ROW_TILE)],
                                          sem.at[into]).start(priority=u % 2)
            return carry

        lax.fori_loop(0, tc // ROW_UNROLL, issue, 0)

    def wait_slot(s):
        for kk in range(TOP_K):
            pltpu.make_async_copy(ys_ref.at[pl.ds(0, tc * ROW_TILE)], buf.at[s, kk], sem.at[s]).wait()

    @pl.when(i == 0)
    def _():
        issue_all(dcur_ref, 0)

    wait_slot(slot)
    other = 1 - slot
    for t in range(tc):
        for kk in range(TOP_K):
            src = pl.multiple_of(dnext_ref[0, 0, kk * tc + t] * ROW_TILE, ROW_TILE)
            pltpu.make_async_copy(ys_ref.at[pl.ds(src, ROW_TILE)], buf.at[other, kk, pl.ds(t * ROW_TILE, ROW_TILE)],
                                  sem.at[other]).start(priority=t % 2)
    for r0 in range(0, tc, COMBINE_ROWS):
        rows = slice(r0, r0 + COMBINE_ROWS)
        acc = x_ref[rows, :]
        for kk in range(TOP_K):
            acc = acc + gate_ref[rows, kk:kk + 1] * _load_row_tiles(buf.at[slot, kk], COMBINE_ROWS, r0)
        o_ref[rows, :] = _rms(acc, gfin_ref[...])

    @pl.when(i == pl.num_programs(0) - 1)
    def _():
        wait_slot(other)


def _combine(x2, gate_t, dest_blocks, ys, g_final):
    T, D = x2.shape
    nb, _, per = dest_blocks.shape
    tc = per // TOP_K
    return pl.pallas_call(
        _combine_kernel,
        out_shape=jax.ShapeDtypeStruct((T, D), F32),
        grid=(nb,),
        in_specs=[
            pl.BlockSpec((1, 1, per), lambda i: (i, 0, 0), memory_space=pltpu.SMEM),
            pl.BlockSpec((1, 1, per), lambda i: (jnp.minimum(i + 1, nb - 1), 0, 0), memory_space=pltpu.SMEM),
            pl.BlockSpec((tc, D), lambda i: (i, 0)),
            pl.BlockSpec((tc, TOP_K), lambda i: (i, 0)),
            pl.BlockSpec((1, D), lambda i: (0, 0)),
            pl.BlockSpec(memory_space=pl.ANY),
        ],
        out_specs=pl.BlockSpec((tc, D), lambda i: (i, 0)),
        scratch_shapes=[pltpu.VMEM((2, TOP_K, tc * ROW_TILE, LANES), F32), pltpu.SemaphoreType.DMA((2,))],
        compiler_params=pltpu.CompilerParams(dimension_semantics=("arbitrary",), vmem_limit_bytes=VMEM_LIMIT),
        name="combine",
    )(dest_blocks, dest_blocks, x2, gate_t, g_final.reshape(1, D), ys)


def _block_dest(dest, tokens):
    T = dest.shape[1]
    return dest.reshape(TOP_K, T // tokens, tokens).transpose(1, 0, 2).reshape(T // tokens, 1, TOP_K * tokens)


def _moe(x2, xn, idx, gate, rank, cnt, w_gate, b_gate, w_up, b_up, w_down, b_down, g_final):
    T = x2.shape[0]
    E = w_gate.shape[0]
    R = MOE_ROWS
    n_blocks = -(-T * TOP_K // R) + E
    n_slots = n_blocks * R
    counts = cnt[:, 0].astype(jnp.int32)
    padded = (counts + R - 1) // R * R
    pad_end = jnp.cumsum(padded)
    pad_start = pad_end - padded
    dest = rank
    for e in range(E):
        dest = dest + jnp.where(idx == e, pad_start[e], 0)
    n_used = (pad_end[-1:] // R).astype(jnp.int32)
    block_e = jnp.minimum(jnp.sum(jnp.arange(n_blocks)[:, None] * R >= pad_end[None, :], axis=1), E - 1)
    block_e = block_e.astype(jnp.int32)

    xs = _dispatch(xn, _block_dest(dest, min(DISPATCH_TOKENS, T)), counts, pad_end.astype(jnp.int32), n_slots)
    ys = _experts(xs, block_e, n_used, w_gate, b_gate, w_up, b_up, w_down, b_down)
    return _combine(x2, gate.T, _block_dest(dest, min(COMBINE_TOKENS, T)), ys, g_final)


def kernel(x, mem, g_mix, w_in, b_gates, g_mlstm_head, w_spatial, b_spatial, g_gmlp_v, g_gmlp_out, w_out,
           g_xattn, g_mem, w_q, w_kv, w_xo, g_moe, w_router, b_router, w_gate, b_gate, w_up, b_up,
           w_down, b_down, g_final):
    B, S, D = x.shape
    assert g_mix.shape[0] == 1, "the combine kernel fuses the closing norm, so exactly one layer is supported"
    x1 = _mixer(x, g_mix[0], w_in[0], b_gates[0], g_mlstm_head[0], w_spatial[0], b_spatial[0],
                g_gmlp_v[0], g_gmlp_out[0], w_out[0])
    k, v = _kv_proj(mem, g_mem[0], w_kv[0])
    x2, xn, idx, gate, rank, cnt = _xattn_router(x1, k, v, g_xattn[0], w_q[0], w_xo[0], g_moe[0],
                                                 w_router[0], b_router[0])
    out = _moe(x2.reshape(B * S, D), xn, idx, gate, rank, cnt, w_gate[0], b_gate[0], w_up[0], b_up[0],
               w_down[0], b_down[0], g_final)
    return out.reshape(B, S, D)
```

```python
import functools

import jax
import jax.numpy as jnp
from jax import lax
from jax.experimental import pallas as pl
from jax.experimental.pallas import tpu as pltpu

F32 = jnp.float32
BF16 = jnp.bfloat16

MLSTM_HEADS = 4
MLSTM_DK = 64
MLSTM_DV = 128
CHUNK = 128
GMLP_GROUPS = 4
GMLP_DG = 128
XATTN_HEADS = 4
TOP_K = 4
GATE_SOFTCAP = 15.0
SWIGLU_LIMIT = 7.0
SWIGLU_ALPHA = 1.702
NORM_EPS = 1e-6

LANES = 128
ROW_TILE = 8

MIX_TOKENS = 512
XATTN_TOKENS = 512
MOE_ROWS = 512
MOE_GROUPS = 2
FFN_PIECE = 256
DISPATCH_TOKENS = 1024
COMBINE_TOKENS = 256
COMBINE_ROWS = 32
ROW_UNROLL = 16
VMEM_LIMIT = 52 * 1024 * 1024


def _rms(x, g):
    return x * lax.rsqrt(jnp.mean(x * x, axis=-1, keepdims=True) + NORM_EPS) * g


def _gelu_tanh(x):
    return 0.5 * x * (1.0 + jnp.tanh(0.7978845608028654 * (x + 0.044715 * x * x * x)))


def _dot(a, b):
    return jnp.dot(a, b, preferred_element_type=F32)


def _dot_nt(a, b):
    return lax.dot_general(a, b, (((1,), (1,)), ((), ())), preferred_element_type=F32)


def _load_row_tiles(ref, rows, first=0):
    return jnp.concatenate(
        [ref[pl.ds(first * ROW_TILE + j, rows, stride=ROW_TILE), :] for j in range(ROW_TILE)], axis=1)


def _store_row_tiles(ref, val):
    for j in range(ROW_TILE):
        ref[pl.ds(j, val.shape[0], stride=ROW_TILE), :] = val[:, j * LANES:(j + 1) * LANES]


def _split3(x):
    hi = x.astype(BF16)
    r1 = x - hi.astype(F32)
    mid = r1.astype(BF16)
    lo = (r1 - mid.astype(F32)).astype(BF16)
    return hi, mid, lo


def _mixer_kernel(x_ref, gmix_ref, win_ref, bg_ref, ghead_ref, wsp_ref, bsp_ref, gzv_ref, gzo_ref, wout_ref,
                  o_ref, proj_scr, y_scr, ct_scr, m_scr, *, n_chunks, cols):
    qk0, v0, og0, gu0, gv0, gt0 = cols
    H, G, L = MLSTM_HEADS, GMLP_GROUPS, CHUNK

    @pl.when(pl.program_id(1) == 0)
    def _():
        ct_scr[...] = jnp.zeros_like(ct_scr)
        m_scr[...] = jnp.zeros_like(m_scr)

    row = lax.broadcasted_iota(jnp.int32, (L, L), 0)
    col = lax.broadcasted_iota(jnp.int32, (L, L), 1)
    causal = row >= col
    tril = jnp.where(causal, 1.0, 0.0).astype(BF16)
    triu = jnp.where(row <= col, 1.0, 0.0).astype(BF16)
    lane = lax.broadcasted_iota(jnp.int32, (L, LANES), 1)
    ones_col = jnp.where(lane == 0, 1.0, 0.0).astype(F32)

    x = x_ref[0]
    proj_scr[...] = _dot(_rms(x, gmix_ref[...]).astype(BF16), win_ref[...])

    for c in range(n_chunks):
        rows = slice(c * L, (c + 1) * L)
        pre = proj_scr[rows, gt0:gt0 + LANES] + bg_ref[...]
        capped = GATE_SOFTCAP * jnp.tanh(pre * (1.0 / GATE_SOFTCAP))
        log_sig = jnp.minimum(capped, 0.0) - jnp.log1p(jnp.exp(-jnp.abs(capped)))
        lg = jnp.where(lane < H, capped, log_sig)
        lg_t = lg.T[0:8, :]
        bcol = sum(_dot(tril, p) for p in _split3(lg))
        brow = sum(_dot(p, triu) for p in _split3(lg_t))

        for h in range(H):
            b_c = bcol[:, H + h:H + h + 1]
            i_c = lg[:, h:h + 1]
            b_r = brow[H + h:H + h + 1, :]
            i_r = lg_t[h:h + 1, :]
            m_prev = m_scr[h, 0:1, 0:1]
            qk = proj_scr[rows, qk0 + h * LANES:qk0 + (h + 1) * LANES]
            q = jnp.where(lane < MLSTM_DK, qk * (MLSTM_DK ** -0.5), 0.0)
            k = jnp.where(lane < MLSTM_DK, pltpu.roll(qk, MLSTM_DK, axis=1), 0.0)
            v = proj_scr[rows, v0 + h * MLSTM_DV:v0 + (h + 1) * MLSTM_DV]
            qb = q.astype(BF16)
            kb = k.astype(BF16)
            vaug = jnp.concatenate([v, ones_col], axis=1)

            d = jnp.where(causal, b_c - b_r + i_r, -jnp.inf)
            inter = b_c + m_prev
            m_t = jnp.maximum(inter, jnp.max(d, axis=1, keepdims=True))
            w_intra = jnp.exp(d - m_t)
            w_inter = jnp.exp(inter - m_t)
            s = _dot_nt(qb, kb) * w_intra
            ct = ct_scr[h]
            na = _dot(s.astype(BF16), vaug.astype(BF16)) + w_inter * _dot(qb, ct.astype(BF16))
            num = na[:, :MLSTM_DV]
            den = na[:, MLSTM_DV:MLSTM_DV + 1]
            hh = num / jnp.maximum(jnp.abs(den), jnp.exp(-m_t))

            b_last = b_c[L - 1:L, :]
            g_c = b_last - b_c + i_c
            m_new = jnp.maximum(b_last + m_prev, jnp.max(g_c, axis=0, keepdims=True))
            wk = jnp.exp(g_c - m_new)
            decay = jnp.exp(b_last + m_prev - m_new)
            ct_scr[h] = decay * ct + _dot(k.T.astype(BF16), (wk * vaug).astype(BF16))
            m_scr[h] = jnp.broadcast_to(m_new, m_scr.shape[1:])

            hn = _rms(hh, ghead_ref[:, h * MLSTM_DV:(h + 1) * MLSTM_DV])
            og = proj_scr[rows, og0 + h * MLSTM_DV:og0 + (h + 1) * MLSTM_DV]
            y_scr[rows, h * MLSTM_DV:(h + 1) * MLSTM_DV] = (jax.nn.sigmoid(og) * hn).astype(BF16)

        for g in range(G):
            sl = slice(g * GMLP_DG, (g + 1) * GMLP_DG)
            u = _gelu_tanh(proj_scr[rows, gu0 + g * GMLP_DG:gu0 + (g + 1) * GMLP_DG])
            z = _rms(_gelu_tanh(proj_scr[rows, gv0 + g * GMLP_DG:gv0 + (g + 1) * GMLP_DG]), gzv_ref[:, sl])
            wsp = jnp.where(causal, wsp_ref[g], 0.0)
            mixed = _dot(wsp.astype(BF16), z.astype(BF16)) + bsp_ref[g]
            yg = _rms(u * mixed, gzo_ref[:, sl])
            y_scr[rows, H * MLSTM_DV + g * GMLP_DG:H * MLSTM_DV + (g + 1) * GMLP_DG] = yg.astype(BF16)
    o_ref[0] = x + _dot(y_scr[...], wout_ref[...])


def _mixer(x, g_mix, w_in, b_gates, g_head, w_spatial, b_spatial, g_zv, g_zo, w_out):
    B, S, D = x.shape
    H, G, L = MLSTM_HEADS, GMLP_GROUPS, CHUNK
    qw, vw, gw = H * MLSTM_DK, H * MLSTM_DV, G * GMLP_DG
    ts = min(MIX_TOKENS, S)
    assert S % ts == 0 and ts % L == 0

    assert 2 * MLSTM_DK == LANES
    c = 0
    w_q = w_in[:, c:c + qw].reshape(D, H, MLSTM_DK); c += qw
    w_k = w_in[:, c:c + qw].reshape(D, H, MLSTM_DK); c += qw
    w_qk = jnp.concatenate([w_q, w_k], axis=2).reshape(D, H * LANES)
    w_v = w_in[:, c:c + vw]; c += vw
    w_o = w_in[:, c:c + vw]; c += vw
    w_g = jnp.pad(w_in[:, c:c + 2 * H], ((0, 0), (0, LANES - 2 * H))); c += 2 * H
    w_gu = w_in[:, c:c + gw]; c += gw
    w_gv = w_in[:, c:c + gw]; c += gw
    w_pack = jnp.concatenate([w_qk, w_v, w_o, w_gu, w_gv, w_g], axis=1).astype(BF16)
    hp = H * LANES
    cols = (0, hp, hp + vw, hp + 2 * vw, hp + 2 * vw + gw, hp + 2 * vw + 2 * gw)
    n_cols = w_pack.shape[1]
    bg = jnp.pad(b_gates, (0, LANES - 2 * H)).reshape(1, LANES)
    bsp = jnp.broadcast_to(b_spatial[:, :, None], (G, L, GMLP_DG))

    const = lambda *shape: pl.BlockSpec(shape, lambda b, j: (0,) * len(shape))
    return pl.pallas_call(
        functools.partial(_mixer_kernel, n_chunks=ts // L, cols=cols),
        out_shape=jax.ShapeDtypeStruct((B, S, D), F32),
        grid=(B, S // ts),
        in_specs=[
            pl.BlockSpec((1, ts, D), lambda b, j: (b, j, 0)),
            const(1, D), const(D, n_cols), const(1, LANES), const(1, vw),
            const(G, L, L), const(G, L, GMLP_DG), const(1, gw), const(1, gw), const(vw + gw, D),
        ],
        out_specs=pl.BlockSpec((1, ts, D), lambda b, j: (b, j, 0)),
        scratch_shapes=[
            pltpu.VMEM((ts, n_cols), F32),
            pltpu.VMEM((ts, vw + gw), BF16),
            pltpu.VMEM((H, LANES, 2 * MLSTM_DV), F32),
            pltpu.VMEM((H, 8, LANES), F32),
        ],
        compiler_params=pltpu.CompilerParams(
            dimension_semantics=("arbitrary", "arbitrary"), vmem_limit_bytes=VMEM_LIMIT),
        name="mixer",
    )(x, g_mix.reshape(1, D), w_pack, bg, g_head.reshape(1, vw), w_spatial, bsp,
      g_zv.reshape(1, gw), g_zo.reshape(1, gw), w_out.astype(BF16))


def _kv_kernel(mem_ref, g_ref, w_ref, k_ref, v_ref):
    D = mem_ref.shape[-1]
    kv = _dot(_rms(mem_ref[0], g_ref[...]).astype(BF16), w_ref[...])
    k_ref[0] = kv[:, :D].astype(BF16)
    v_ref[0] = kv[:, D:].astype(BF16)


def _kv_proj(mem, g_mem, w_kv):
    B, M, D = mem.shape
    return pl.pallas_call(
        _kv_kernel,
        out_shape=(jax.ShapeDtypeStruct((B, M, D), BF16), jax.ShapeDtypeStruct((B, M, D), BF16)),
        grid=(B,),
        in_specs=[
            pl.BlockSpec((1, M, D), lambda b: (b, 0, 0)),
            pl.BlockSpec((1, D), lambda b: (0, 0)),
            pl.BlockSpec((D, 2 * D), lambda b: (0, 0)),
        ],
        out_specs=(pl.BlockSpec((1, M, D), lambda b: (b, 0, 0)), pl.BlockSpec((1, M, D), lambda b: (b, 0, 0))),
        compiler_params=pltpu.CompilerParams(dimension_semantics=("arbitrary",), vmem_limit_bytes=VMEM_LIMIT),
        name="kv_proj",
    )(mem, g_mem.reshape(1, D), w_kv.astype(BF16))


def _xattn_kernel(x_ref, gx_ref, wq_ref, k_ref, v_ref, wxo_ref, gmoe_ref, wr_ref, br_ref,
                  x2_ref, xn_ref, idx_ref, gate_ref, rank_ref, cnt_ref, o_scr, cnt_scr, *, group_batches):
    ts, D = x_ref.shape[1], x_ref.shape[2]
    hd = D // XATTN_HEADS
    E = wr_ref.shape[0]

    @pl.when((pl.program_id(0) % group_batches == 0) & (pl.program_id(1) == 0))
    def _():
        cnt_scr[...] = jnp.zeros_like(cnt_scr)

    x = x_ref[0]
    q = _dot(_rms(x, gx_ref[...]).astype(BF16), wq_ref[...]) * (hd ** -0.5)
    for h in range(XATTN_HEADS):
        sl = slice(h * hd, (h + 1) * hd)
        s = _dot_nt(q[:, sl].astype(BF16), k_ref[0, :, sl])
        p = jnp.exp(s - jnp.max(s, axis=1, keepdims=True))
        p = p / jnp.sum(p, axis=1, keepdims=True)
        o_scr[:, sl] = _dot(p.astype(BF16), v_ref[0, :, sl]).astype(BF16)
    x2 = x + _dot(o_scr[...], wxo_ref[...])
    x2_ref[0] = x2

    xn = _rms(x2, gmoe_ref[...])
    xh, xm, _ = _split3(xn)
    wh, wm, _ = _split3(wr_ref[...])
    logits = _dot_nt(wh, xh) + (_dot_nt(wh, xm) + _dot_nt(wm, xh)) + br_ref[...]

    e_iota = lax.broadcasted_iota(jnp.int32, (E, ts), 0)
    work = logits
    tops, idxs, hots = [], [], []
    for _ in range(TOP_K):
        m = jnp.max(work, axis=0, keepdims=True)
        i = jnp.min(jnp.where(work == m, e_iota, E), axis=0, keepdims=True)
        hot = e_iota == i
        work = jnp.where(hot, -jnp.inf, work)
        tops.append(m); idxs.append(i); hots.append(hot)
    ex = [jnp.exp(t - tops[0]) for t in tops]
    tot = ex[0] + ex[1] + ex[2] + ex[3]
    gate_ref[...] = jnp.concatenate([e / tot for e in ex], axis=0)
    idx_ref[...] = jnp.concatenate(idxs, axis=0)

    hot_all = jnp.where(hots[0] | hots[1] | hots[2] | hots[3], 1.0, 0.0)
    r = lax.broadcasted_iota(jnp.int32, (ts, ts), 0)
    c = lax.broadcasted_iota(jnp.int32, (ts, ts), 1)
    before = jnp.where(r < c, 1.0, 0.0).astype(BF16)
    base = cnt_scr[:, 0:1] + _dot(hot_all.astype(BF16), before)
    ranks = [jnp.sum(jnp.where(hot, base, 0.0), axis=0, keepdims=True) for hot in hots]
    rank_ref[...] = jnp.concatenate(ranks, axis=0).astype(jnp.int32)
    cnt_new = cnt_scr[...] + jnp.sum(hot_all, axis=1, keepdims=True)
    cnt_scr[...] = cnt_new
    cnt_ref[...] = cnt_new
    _store_row_tiles(xn_ref, xn)


def _xattn_router(x, k, v, g_xattn, w_q, w_xo, g_moe, w_router, b_router, n_groups):
    B, S, D = x.shape
    M = k.shape[1]
    E = w_router.shape[1]
    ts = min(XATTN_TOKENS, S)
    assert S % ts == 0 and B % n_groups == 0
    nj = S // ts
    T = B * S
    gb = B // n_groups
    const = lambda *shape: pl.BlockSpec(shape, lambda b, j: (0,) * len(shape))
    tok = lambda rows: pl.BlockSpec((rows, ts), lambda b, j: (0, b * nj + j))
    return pl.pallas_call(
        functools.partial(_xattn_kernel, group_batches=gb),
        out_shape=(
            jax.ShapeDtypeStruct((B, S, D), F32),
            jax.ShapeDtypeStruct((T * ROW_TILE, LANES), F32),
            jax.ShapeDtypeStruct((TOP_K, T), jnp.int32),
            jax.ShapeDtypeStruct((TOP_K, T), F32),
            jax.ShapeDtypeStruct((TOP_K, T), jnp.int32),
            jax.ShapeDtypeStruct((n_groups * E, LANES), F32),
        ),
        grid=(B, nj),
        in_specs=[
            pl.BlockSpec((1, ts, D), lambda b, j: (b, j, 0)),
            const(1, D), const(D, D),
            pl.BlockSpec((1, M, D), lambda b, j: (b, 0, 0)),
            pl.BlockSpec((1, M, D), lambda b, j: (b, 0, 0)),
            const(D, D), const(1, D), const(E, D), const(E, 1),
        ],
        out_specs=(
            pl.BlockSpec((1, ts, D), lambda b, j: (b, j, 0)),
            pl.BlockSpec((ts * ROW_TILE, LANES), lambda b, j: (b * nj + j, 0)),
            tok(TOP_K), tok(TOP_K), tok(TOP_K),
            pl.BlockSpec((E, LANES), lambda b, j: (b // gb, 0)),
        ),
        scratch_shapes=[pltpu.VMEM((ts, D), BF16), pltpu.VMEM((E, LANES), F32)],
        compiler_params=pltpu.CompilerParams(
            dimension_semantics=("arbitrary", "arbitrary"), vmem_limit_bytes=VMEM_LIMIT),
        name="xattn_router",
    )(x, g_xattn.reshape(1, D), w_q.astype(BF16), k, v, w_xo.astype(BF16), g_moe.reshape(1, D),
      w_router.T, b_router.reshape(E, 1))


def _dispatch_kernel(dest_ref, cnt_ref, pend_ref, xn_ref, *rest, n_experts, n_groups):
    out_refs, (zero_scr, sem, zsem) = rest[:n_groups], rest[n_groups:]
    xs_ref = out_refs[0]
    tt = xn_ref.shape[0] // ROW_TILE
    R = zero_scr.shape[0] // ROW_TILE

    @pl.when(pl.program_id(0) == 0)
    def _():
        zero_scr[...] = jnp.zeros_like(zero_scr)

        def fill(i, carry):
            for g in range(n_groups):
                @pl.when(cnt_ref[g * n_experts + i] > 0)
                def _():
                    start = pl.multiple_of((pend_ref[g * n_experts + i] - R) * ROW_TILE, R * ROW_TILE)
                    cp = pltpu.make_async_copy(zero_scr, out_refs[g].at[pl.ds(start, R * ROW_TILE)], zsem)
                    cp.start()
                    cp.wait()
            return carry

        lax.fori_loop(0, n_experts, fill, 0)

    def issue(grp, carry):
        t0 = pl.multiple_of(grp * ROW_UNROLL, ROW_UNROLL)
        for kk in range(TOP_K):
            for u in range(ROW_UNROLL):
                dst = pl.multiple_of(dest_ref[0, 0, kk * tt + t0 + u] * ROW_TILE, ROW_TILE)
                row = pl.multiple_of((t0 + u) * ROW_TILE, ROW_TILE)
                pltpu.make_async_copy(xn_ref.at[pl.ds(row, ROW_TILE)], xs_ref.at[pl.ds(dst, ROW_TILE)],
                                      sem).start(priority=u % 2)
        return carry

    lax.fori_loop(0, tt // ROW_UNROLL, issue, 0)
    for kk in range(TOP_K):
        pltpu.make_async_copy(xn_ref, xs_ref.at[pl.ds(0, tt * ROW_TILE)], sem).wait()


def _dispatch(xn, dest_blocks, counts, pad_end, n_slots, n_groups):
    nb, _, per = dest_blocks.shape
    tt = per // TOP_K
    E = counts.shape[0] // n_groups
    slots = jax.ShapeDtypeStruct((n_slots * ROW_TILE, LANES), F32)
    return pl.pallas_call(
        functools.partial(_dispatch_kernel, n_experts=E, n_groups=n_groups),
        out_shape=(slots,) * n_groups,
        grid=(nb // n_groups,),
        in_specs=[
            pl.BlockSpec((1, 1, per), lambda i: (i, 0, 0), memory_space=pltpu.SMEM),
            pl.BlockSpec(memory_space=pltpu.SMEM),
            pl.BlockSpec(memory_space=pltpu.SMEM),
            pl.BlockSpec((tt * ROW_TILE, LANES), lambda i: (i, 0)),
        ],
        out_specs=(pl.BlockSpec(memory_space=pl.ANY),) * n_groups,
        scratch_shapes=[pltpu.VMEM((MOE_ROWS * ROW_TILE, LANES), F32), pltpu.SemaphoreType.DMA,
                        pltpu.SemaphoreType.DMA],
        compiler_params=pltpu.CompilerParams(
            dimension_semantics=("arbitrary",), vmem_limit_bytes=VMEM_LIMIT, has_side_effects=True),
        name="dispatch",
    )(dest_blocks, counts, pad_end, xn)


def _expert_ffn(xs_ref, bg_ref, bu_ref, bd_ref, y_ref, wg_scr, wu_scr, wd_scr, h_scr, between):
    rows = xs_ref.shape[0] // ROW_TILE
    F, D = wg_scr.shape[1], wd_scr.shape[1]
    n = F // FFN_PIECE + D // FFN_PIECE
    xb = _load_row_tiles(xs_ref, rows).astype(BF16)
    for p in range(F // FFN_PIECE):
        cs = slice(p * FFN_PIECE, (p + 1) * FFN_PIECE)
        g = jnp.minimum(_dot(xb, wg_scr[:, cs]) + bg_ref[0, :, cs], SWIGLU_LIMIT)
        u = jnp.clip(_dot(xb, wu_scr[:, cs]) + bu_ref[0, :, cs], -SWIGLU_LIMIT, SWIGLU_LIMIT)
        h_scr[:, cs] = ((u + 1.0) * (g * jax.nn.sigmoid(SWIGLU_ALPHA * g))).astype(BF16)
        between(p, n)
    for p in range(D // FFN_PIECE):
        cs = slice(p * FFN_PIECE, (p + 1) * FFN_PIECE)
        y = _dot(h_scr[...], wd_scr[:, cs]) + bd_ref[0, :, cs]
        for jj in range(FFN_PIECE // LANES):
            lane_tile = p * (FFN_PIECE // LANES) + jj
            y_ref[pl.ds(lane_tile, rows, stride=ROW_TILE), :] = y[:, jj * LANES:(jj + 1) * LANES]
        between(F // FFN_PIECE + p, n)


def _expert_kernel(*refs, mode, tps, nmin):
    be_ref, nu_ref, xs_ref, wg_ref, bg_ref, wu_ref, bu_ref, wd_ref, bd_ref = refs[:9]
    refs = refs[9:]
    j = pl.program_id(0)
    nu = nu_ref[0]
    active = j < nu

    if mode == "dispatch":
        dn_ref, xnn_ref, _, y_ref, xsn_ref, wg_scr, wu_scr, wd_scr, h_scr, sem = refs
    elif mode == "combine":
        (dcur_ref, dnxt_ref, x2_ref, gate_ref, gfin_ref, ysp_ref, y_ref, o_ref,
         wg_scr, wu_scr, wd_scr, h_scr, buf_a, buf_b, sem) = refs
    else:
        y_ref, wg_scr, wu_scr, wd_scr, h_scr = refs
    ffn = functools.partial(_expert_ffn, xs_ref, bg_ref, bu_ref, bd_ref, y_ref, wg_scr, wu_scr, wd_scr, h_scr)

    @pl.when(active & ((j == 0) | (be_ref[j] != be_ref[jnp.maximum(j - 1, 0)])))
    def _():
        wg_scr[...] = wg_ref[0].astype(BF16)
        wu_scr[...] = wu_ref[0].astype(BF16)
        wd_scr[...] = wd_ref[0].astype(BF16)

    if mode is None:
        @pl.when(active)
        def _():
            ffn(lambda i, n: None)

    elif mode == "dispatch":
        @pl.when(active)
        def _():
            def between(i, n):
                for t in range(i * tps // n, (i + 1) * tps // n):
                    for kk in range(TOP_K):
                        dst = pl.multiple_of(dn_ref[0, 0, kk * tps + t] * ROW_TILE, ROW_TILE)
                        pltpu.make_async_copy(xnn_ref.at[pl.ds(t * ROW_TILE, ROW_TILE)],
                                              xsn_ref.at[pl.ds(dst, ROW_TILE)], sem).start(priority=t % 2)

            ffn(between)
            for kk in range(TOP_K):
                pltpu.make_async_copy(xnn_ref, xsn_ref.at[pl.ds(0, tps * ROW_TILE)], sem).wait()

    else:
        def wait_buf(buf, s):
            for kk in range(TOP_K):
                pltpu.make_async_copy(ysp_ref.at[pl.ds(0, tps * ROW_TILE)], buf.at[kk], sem.at[s]).wait()

        def gather(d_ref, buf, s, t):
            static = isinstance(t, int)
            row = t * ROW_TILE if static else pl.multiple_of(t * ROW_TILE, ROW_TILE)
            for kk in range(TOP_K):
                src = pl.multiple_of(d_ref[0, 0, kk * tps + t] * ROW_TILE, ROW_TILE)
                pltpu.make_async_copy(ysp_ref.at[pl.ds(src, ROW_TILE)], buf.at[kk, pl.ds(row, ROW_TILE)],
                                      sem.at[s]).start(priority=t % 2 if static else kk % 2)

        @pl.when(j == 0)
        def _():
            def first(t, carry):
                gather(dcur_ref, buf_a, 0, t)
                return carry

            lax.fori_loop(0, tps, first, 0)

        def step(cur, cur_s, nxt, nxt_s):
            wait_buf(cur, cur_s)

            def between(i, n):
                lo, hi = i * tps // n, (i + 1) * tps // n
                for t in range(lo, hi):
                    gather(dnxt_ref, nxt, nxt_s, t)
                rows = slice(lo, hi)
                acc = x2_ref[rows, :]
                for kk in range(TOP_K):
                    acc = acc + gate_ref[rows, kk:kk + 1] * _load_row_tiles(cur.at[kk], hi - lo, lo)
                o_ref[rows, :] = _rms(acc, gfin_ref[...])

            ffn(between)

            @pl.when(j == nu - 1)
            def _():
                wait_buf(nxt, nxt_s)

        @pl.when(active & (j % 2 == 0))
        def _():
            step(buf_a, 0, buf_b, 1)

        @pl.when(active & (j % 2 == 1))
        def _():
            step(buf_b, 1, buf_a, 0)


def _experts(xs, block_e, n_used, weights, *, mode=None, extra=(), tile0=0, nmin=0):
    w_gate, b_gate, w_up, b_up, w_down, b_down = weights
    E, D, F = w_gate.shape
    assert D == ROW_TILE * LANES and F % FFN_PIECE == 0 and D % FFN_PIECE == 0
    n_slots = xs.shape[0] // ROW_TILE
    R = MOE_ROWS
    n_blocks = n_slots // R
    tps = R // TOP_K
    slots = jax.ShapeDtypeStruct((n_slots * ROW_TILE, LANES), F32)
    blk = lambda j, be, nu: (jnp.minimum(j, nu[0] - 1), 0)
    wsel = lambda j, be, nu: (be[jnp.minimum(j, nu[0] - 1)], 0, 0)
    in_specs = [
        pl.BlockSpec((R * ROW_TILE, LANES), blk),
        pl.BlockSpec((1, D, F), wsel), pl.BlockSpec((1, 1, F), wsel),
        pl.BlockSpec((1, D, F), wsel), pl.BlockSpec((1, 1, F), wsel),
        pl.BlockSpec((1, F, D), wsel), pl.BlockSpec((1, 1, D), wsel),
    ]
    args = [xs, w_gate, b_gate.reshape(E, 1, F), w_up, b_up.reshape(E, 1, F), w_down, b_down.reshape(E, 1, D)]
    out_shape, out_specs = [slots], [pl.BlockSpec((R * ROW_TILE, LANES), blk)]
    scratch = [pltpu.VMEM((D, F), BF16), pltpu.VMEM((D, F), BF16), pltpu.VMEM((F, D), BF16), pltpu.VMEM((R, F), BF16)]
    aliases = {}
    tile = lambda j, be, nu: (tile0 + jnp.minimum(j, nmin - 1), 0)
    dspec = lambda off: pl.BlockSpec((1, 1, TOP_K * tps), lambda j, be, nu: (tile0 + jnp.minimum(j + off, nmin - 1), 0, 0),
                                     memory_space=pltpu.SMEM)
    if mode == "dispatch":
        xn, dest_blocks, xs_other = extra
        assert dest_blocks.shape[2] == TOP_K * tps
        in_specs += [dspec(0), pl.BlockSpec((tps * ROW_TILE, LANES), tile), pl.BlockSpec(memory_space=pl.ANY)]
        args += [dest_blocks, xn, xs_other]
        out_shape.append(slots)
        out_specs.append(pl.BlockSpec(memory_space=pl.ANY))
        aliases = {2 + len(args) - 1: 1}
        scratch.append(pltpu.SemaphoreType.DMA)
    elif mode == "combine":
        x2, gate_t, dest_blocks, ys_other, g_final = extra
        assert dest_blocks.shape[2] == TOP_K * tps
        in_specs += [dspec(0), dspec(1), pl.BlockSpec((tps, D), tile), pl.BlockSpec((tps, TOP_K), tile),
                     pl.BlockSpec((1, D), lambda j, be, nu: (0, 0)), pl.BlockSpec(memory_space=pl.ANY)]
        args += [dest_blocks, dest_blocks, x2, gate_t, g_final.reshape(1, D), ys_other]
        out_shape.append(jax.ShapeDtypeStruct(x2.shape, F32))
        out_specs.append(pl.BlockSpec((tps, D), tile))
        buf = pltpu.VMEM((TOP_K, tps * ROW_TILE, LANES), F32)
        scratch += [buf, buf, pltpu.SemaphoreType.DMA((2,))]
    grid_spec = pltpu.PrefetchScalarGridSpec(
        num_scalar_prefetch=2, grid=(n_blocks,), in_specs=in_specs, out_specs=tuple(out_specs),
        scratch_shapes=scratch)
    return pl.pallas_call(
        functools.partial(_expert_kernel, mode=mode, tps=tps, nmin=nmin),
        out_shape=tuple(out_shape),
        grid_spec=grid_spec,
        input_output_aliases=aliases,
        compiler_params=pltpu.CompilerParams(dimension_semantics=("arbitrary",), vmem_limit_bytes=VMEM_LIMIT,
                                             has_side_effects=mode == "dispatch"),
        name="experts" if mode is None else "experts_" + mode,
    )(block_e, n_used, *args)


def _combine_kernel(dcur_ref, dnext_ref, x_ref, gate_ref, gfin_ref, ys_ref, _, o_ref, buf_a, buf_b, sem):
    tc = x_ref.shape[0]
    i = pl.program_id(0)
    last = pl.num_programs(0) - 1

    def wait_buf(buf, s):
        for kk in range(TOP_K):
            pltpu.make_async_copy(ys_ref.at[pl.ds(0, tc * ROW_TILE)], buf.at[kk], sem.at[s]).wait()

    def issue_rows(d_ref, buf, s, rows):
        for t in rows:
            for kk in range(TOP_K):
                src = pl.multiple_of(d_ref[0, 0, kk * tc + t] * ROW_TILE, ROW_TILE)
                pltpu.make_async_copy(ys_ref.at[pl.ds(src, ROW_TILE)], buf.at[kk, pl.ds(t * ROW_TILE, ROW_TILE)],
                                      sem.at[s]).start(priority=t % 2)

    @pl.when(i == 0)
    def _():
        issue_rows(dcur_ref, buf_a, 0, range(tc))

    def step(cur, cur_s, nxt, nxt_s):
        wait_buf(cur, cur_s)
        for r0 in range(0, tc, COMBINE_ROWS):
            issue_rows(dnext_ref, nxt, nxt_s, range(r0, r0 + COMBINE_ROWS))
            rows = slice(r0, r0 + COMBINE_ROWS)
            acc = x_ref[rows, :]
            for kk in range(TOP_K):
                acc = acc + gate_ref[rows, kk:kk + 1] * _load_row_tiles(cur.at[kk], COMBINE_ROWS, r0)
            o_ref[rows, :] = _rms(acc, gfin_ref[...])

        @pl.when(i == last)
        def _():
            wait_buf(nxt, nxt_s)

    @pl.when(i % 2 == 0)
    def _():
        step(buf_a, 0, buf_b, 1)

    @pl.when(i % 2 == 1)
    def _():
        step(buf_b, 1, buf_a, 0)


def _combine(x2, gate_t, dest_blocks, ys, g_final, out_partial, tile0, n_tiles):
    T, D = x2.shape
    per = dest_blocks.shape[2]
    tc = per // TOP_K
    return pl.pallas_call(
        _combine_kernel,
        out_shape=jax.ShapeDtypeStruct((T, D), F32),
        grid=(n_tiles,),
        in_specs=[
            pl.BlockSpec((1, 1, per), lambda i: (tile0 + i, 0, 0), memory_space=pltpu.SMEM),
            pl.BlockSpec((1, 1, per), lambda i: (tile0 + jnp.minimum(i + 1, n_tiles - 1), 0, 0),
                         memory_space=pltpu.SMEM),
            pl.BlockSpec((tc, D), lambda i: (tile0 + i, 0)),
            pl.BlockSpec((tc, TOP_K), lambda i: (tile0 + i, 0)),
            pl.BlockSpec((1, D), lambda i: (0, 0)),
            pl.BlockSpec(memory_space=pl.ANY),
            pl.BlockSpec(memory_space=pl.ANY),
        ],
        out_specs=pl.BlockSpec((tc, D), lambda i: (tile0 + i, 0)),
        scratch_shapes=[pltpu.VMEM((TOP_K, tc * ROW_TILE, LANES), F32), pltpu.VMEM((TOP_K, tc * ROW_TILE, LANES), F32),
                        pltpu.SemaphoreType.DMA((2,))],
        input_output_aliases={6: 0},
        compiler_params=pltpu.CompilerParams(dimension_semantics=("arbitrary",), vmem_limit_bytes=VMEM_LIMIT),
        name="combine",
    )(dest_blocks, dest_blocks, x2, gate_t, g_final.reshape(1, D), ys, out_partial)


def _block_dest(dest, tokens):
    T = dest.shape[1]
    return dest.reshape(TOP_K, T // tokens, tokens).transpose(1, 0, 2).reshape(T // tokens, 1, TOP_K * tokens)


def _moe(x2, xn, idx, gate, rank, cnt, weights, g_final):
    T = x2.shape[0]
    E = weights[0].shape[0]
    R = MOE_ROWS
    G = MOE_GROUPS
    Tg = T // G
    tps = R // TOP_K
    assert T % G == 0 and Tg % tps == 0 and Tg % DISPATCH_TOKENS == 0 and Tg % COMBINE_TOKENS == 0
    n_blocks = -(-Tg * TOP_K // R) + E
    n_slots = n_blocks * R
    counts = cnt[:, 0].astype(jnp.int32).reshape(G, E)
    padded = (counts + R - 1) // R * R
    pad_end = jnp.cumsum(padded, axis=1)
    pad_start = pad_end - padded
    parts = []
    for g in range(G):
        cols = slice(g * Tg, (g + 1) * Tg)
        d = rank[:, cols]
        for e in range(E):
            d = d + jnp.where(idx[:, cols] == e, pad_start[g, e], 0)
        parts.append(d)
    dest = jnp.concatenate(parts, axis=1)
    n_used = (pad_end[:, -1:] // R).astype(jnp.int32)
    block_e = jnp.minimum(jnp.sum(jnp.arange(n_blocks)[None, :, None] * R >= pad_end[:, None, :], axis=2), E - 1)
    block_e = block_e.astype(jnp.int32)

    gate_t = gate.T
    d_fused = _block_dest(dest, tps)
    xs0, xs1 = _dispatch(xn, _block_dest(dest, DISPATCH_TOKENS), counts.reshape(-1),
                         pad_end.astype(jnp.int32).reshape(-1), n_slots, G)
    ys0, xs1 = _experts(xs0, block_e[0], n_used[0], weights, mode="dispatch", extra=(xn, d_fused, xs1),
                        tile0=Tg // tps, nmin=Tg // tps)
    ys1, out = _experts(xs1, block_e[1], n_used[1], weights, mode="combine",
                        extra=(x2, gate_t, d_fused, ys0, g_final), tile0=0, nmin=Tg // tps)
    return _combine(x2, gate_t, _block_dest(dest, COMBINE_TOKENS), ys1, g_final, out,
                    tile0=Tg // COMBINE_TOKENS, n_tiles=Tg // COMBINE_TOKENS)


def kernel(x, mem, g_mix, w_in, b_gates, g_mlstm_head, w_spatial, b_spatial, g_gmlp_v, g_gmlp_out, w_out,
           g_xattn, g_mem, w_q, w_kv, w_xo, g_moe, w_router, b_router, w_gate, b_gate, w_up, b_up,
           w_down, b_down, g_final):
    B, S, D = x.shape
    assert g_mix.shape[0] == 1, "the combine kernel fuses the closing norm, so exactly one layer is supported"
    x1 = _mixer(x, g_mix[0], w_in[0], b_gates[0], g_mlstm_head[0], w_spatial[0], b_spatial[0],
                g_gmlp_v[0], g_gmlp_out[0], w_out[0])
    k, v = _kv_proj(mem, g_mem[0], w_kv[0])
    x2, xn, idx, gate, rank, cnt = _xattn_router(x1, k, v, g_xattn[0], w_q[0], w_xo[0], g_moe[0],
                                                 w_router[0], b_router[0], MOE_GROUPS)
    out = _moe(x2.reshape(B * S, D), xn, idx, gate, rank, cnt,
               (w_gate[0], b_gate[0], w_up[0], b_up[0], w_down[0], b_down[0]), g_final)
    return out.reshape(B, S, D)
```

```python
import functools

import jax
import jax.numpy as jnp
from jax import lax
from jax.experimental import pallas as pl
from jax.experimental.pallas import tpu as pltpu

F32 = jnp.float32
BF16 = jnp.bfloat16

MLSTM_HEADS = 4
MLSTM_DK = 64
MLSTM_DV = 128
CHUNK = 128
GMLP_GROUPS = 4
GMLP_DG = 128
XATTN_HEADS = 4
TOP_K = 4
GATE_SOFTCAP = 15.0
SWIGLU_LIMIT = 7.0
SWIGLU_ALPHA = 1.702
NORM_EPS = 1e-6

LANES = 128
ROW_TILE = 8

MIX_TOKENS = 512
XATTN_TOKENS = 512
MOE_ROWS = 512
DISPATCH_TOKENS = 2048
COMBINE_TOKENS = 256
COMBINE_ROWS = 32
ROW_UNROLL = 16
VMEM_LIMIT = 52 * 1024 * 1024


def _rms(x, g):
    return x * lax.rsqrt(jnp.mean(x * x, axis=-1, keepdims=True) + NORM_EPS) * g


def _gelu_tanh(x):
    return 0.5 * x * (1.0 + jnp.tanh(0.7978845608028654 * (x + 0.044715 * x * x * x)))


def _dot(a, b):
    return jnp.dot(a, b, preferred_element_type=F32)


def _dot_nt(a, b):
    return lax.dot_general(a, b, (((1,), (1,)), ((), ())), preferred_element_type=F32)


def _load_row_tiles(ref, rows, first=0):
    return jnp.concatenate(
        [ref[pl.ds(first * ROW_TILE + j, rows, stride=ROW_TILE), :] for j in range(ROW_TILE)], axis=1)


def _store_row_tiles(ref, val):
    for j in range(ROW_TILE):
        ref[pl.ds(j, val.shape[0], stride=ROW_TILE), :] = val[:, j * LANES:(j + 1) * LANES]


def _split3(x):
    hi = x.astype(BF16)
    r1 = x - hi.astype(F32)
    mid = r1.astype(BF16)
    lo = (r1 - mid.astype(F32)).astype(BF16)
    return hi, mid, lo


def _mixer_kernel(x_ref, gmix_ref, win_ref, bg_ref, ghead_ref, wsp_ref, bsp_ref, gzv_ref, gzo_ref, wout_ref,
                  o_ref, proj_scr, y_scr, ct_scr, m_scr, *, n_chunks, cols):
    qk0, v0, og0, gu0, gv0, gt0 = cols
    H, G, L = MLSTM_HEADS, GMLP_GROUPS, CHUNK

    @pl.when(pl.program_id(1) == 0)
    def _():
        ct_scr[...] = jnp.zeros_like(ct_scr)
        m_scr[...] = jnp.zeros_like(m_scr)

    row = lax.broadcasted_iota(jnp.int32, (L, L), 0)
    col = lax.broadcasted_iota(jnp.int32, (L, L), 1)
    causal = row >= col
    tril = jnp.where(causal, 1.0, 0.0).astype(BF16)
    triu = jnp.where(row <= col, 1.0, 0.0).astype(BF16)
    lane = lax.broadcasted_iota(jnp.int32, (L, LANES), 1)
    ones_col = jnp.where(lane == 0, 1.0, 0.0).astype(F32)

    x = x_ref[0]
    proj_scr[...] = _dot(_rms(x, gmix_ref[...]).astype(BF16), win_ref[...])

    for c in range(n_chunks):
        rows = slice(c * L, (c + 1) * L)
        pre = proj_scr[rows, gt0:gt0 + LANES] + bg_ref[...]
        capped = GATE_SOFTCAP * jnp.tanh(pre * (1.0 / GATE_SOFTCAP))
        log_sig = jnp.minimum(capped, 0.0) - jnp.log1p(jnp.exp(-jnp.abs(capped)))
        lg = jnp.where(lane < H, capped, log_sig)
        lg_t = lg.T[0:8, :]
        bcol = sum(_dot(tril, p) for p in _split3(lg))
        brow = sum(_dot(p, triu) for p in _split3(lg_t))

        for h in range(H):
            b_c = bcol[:, H + h:H + h + 1]
            i_c = lg[:, h:h + 1]
            b_r = brow[H + h:H + h + 1, :]
            i_r = lg_t[h:h + 1, :]
            m_prev = m_scr[h, 0:1, 0:1]
            qk = proj_scr[rows, qk0 + h * LANES:qk0 + (h + 1) * LANES]
            q = jnp.where(lane < MLSTM_DK, qk * (MLSTM_DK ** -0.5), 0.0)
            k = jnp.where(lane < MLSTM_DK, pltpu.roll(qk, MLSTM_DK, axis=1), 0.0)
            v = proj_scr[rows, v0 + h * MLSTM_DV:v0 + (h + 1) * MLSTM_DV]
            qb = q.astype(BF16)
            kb = k.astype(BF16)
            vaug = jnp.concatenate([v, ones_col], axis=1)

            d = jnp.where(causal, b_c - b_r + i_r, -jnp.inf)
            inter = b_c + m_prev
            m_t = jnp.maximum(inter, jnp.max(d, axis=1, keepdims=True))
            w_intra = jnp.exp(d - m_t)
            w_inter = jnp.exp(inter - m_t)
            s = _dot_nt(qb, kb) * w_intra
            ct = ct_scr[h]
            na = _dot(s.astype(BF16), vaug.astype(BF16)) + w_inter * _dot(qb, ct.astype(BF16))
            num = na[:, :MLSTM_DV]
            den = na[:, MLSTM_DV:MLSTM_DV + 1]
            hh = num / jnp.maximum(jnp.abs(den), jnp.exp(-m_t))

            b_last = b_c[L - 1:L, :]
            g_c = b_last - b_c + i_c
            m_new = jnp.maximum(b_last + m_prev, jnp.max(g_c, axis=0, keepdims=True))
            wk = jnp.exp(g_c - m_new)
            decay = jnp.exp(b_last + m_prev - m_new)
            ct_scr[h] = decay * ct + _dot(k.T.astype(BF16), (wk * vaug).astype(BF16))
            m_scr[h] = jnp.broadcast_to(m_new, m_scr.shape[1:])

            hn = _rms(hh, ghead_ref[:, h * MLSTM_DV:(h + 1) * MLSTM_DV])
            og = proj_scr[rows, og0 + h * MLSTM_DV:og0 + (h + 1) * MLSTM_DV]
            y_scr[rows, h * MLSTM_DV:(h + 1) * MLSTM_DV] = (jax.nn.sigmoid(og) * hn).astype(BF16)

        for g in range(G):
            sl = slice(g * GMLP_DG, (g + 1) * GMLP_DG)
            u = _gelu_tanh(proj_scr[rows, gu0 + g * GMLP_DG:gu0 + (g + 1) * GMLP_DG])
            z = _rms(_gelu_tanh(proj_scr[rows, gv0 + g * GMLP_DG:gv0 + (g + 1) * GMLP_DG]), gzv_ref[:, sl])
            wsp = jnp.where(causal, wsp_ref[g], 0.0)
            mixed = _dot(wsp.astype(BF16), z.astype(BF16)) + bsp_ref[g]
            yg = _rms(u * mixed, gzo_ref[:, sl])
            y_scr[rows, H * MLSTM_DV + g * GMLP_DG:H * MLSTM_DV + (g + 1) * GMLP_DG] = yg.astype(BF16)
    o_ref[0] = x + _dot(y_scr[...], wout_ref[...])


def _mixer(x, g_mix, w_in, b_gates, g_head, w_spatial, b_spatial, g_zv, g_zo, w_out):
    B, S, D = x.shape
    H, G, L = MLSTM_HEADS, GMLP_GROUPS, CHUNK
    qw, vw, gw = H * MLSTM_DK, H * MLSTM_DV, G * GMLP_DG
    ts = min(MIX_TOKENS, S)
    assert S % ts == 0 and ts % L == 0

    assert 2 * MLSTM_DK == LANES
    c = 0
    w_q = w_in[:, c:c + qw].reshape(D, H, MLSTM_DK); c += qw
    w_k = w_in[:, c:c + qw].reshape(D, H, MLSTM_DK); c += qw
    w_qk = jnp.concatenate([w_q, w_k], axis=2).reshape(D, H * LANES)
    w_v = w_in[:, c:c + vw]; c += vw
    w_o = w_in[:, c:c + vw]; c += vw
    w_g = jnp.pad(w_in[:, c:c + 2 * H], ((0, 0), (0, LANES - 2 * H))); c += 2 * H
    w_gu = w_in[:, c:c + gw]; c += gw
    w_gv = w_in[:, c:c + gw]; c += gw
    w_pack = jnp.concatenate([w_qk, w_v, w_o, w_gu, w_gv, w_g], axis=1).astype(BF16)
    hp = H * LANES
    cols = (0, hp, hp + vw, hp + 2 * vw, hp + 2 * vw + gw, hp + 2 * vw + 2 * gw)
    n_cols = w_pack.shape[1]
    bg = jnp.pad(b_gates, (0, LANES - 2 * H)).reshape(1, LANES)
    bsp = jnp.broadcast_to(b_spatial[:, :, None], (G, L, GMLP_DG))

    const = lambda *shape: pl.BlockSpec(shape, lambda b, j: (0,) * len(shape))
    return pl.pallas_call(
        functools.partial(_mixer_kernel, n_chunks=ts // L, cols=cols),
        out_shape=jax.ShapeDtypeStruct((B, S, D), F32),
        grid=(B, S // ts),
        in_specs=[
            pl.BlockSpec((1, ts, D), lambda b, j: (b, j, 0)),
            const(1, D), const(D, n_cols), const(1, LANES), const(1, vw),
            const(G, L, L), const(G, L, GMLP_DG), const(1, gw), const(1, gw), const(vw + gw, D),
        ],
        out_specs=pl.BlockSpec((1, ts, D), lambda b, j: (b, j, 0)),
        scratch_shapes=[
            pltpu.VMEM((ts, n_cols), F32),
            pltpu.VMEM((ts, vw + gw), BF16),
            pltpu.VMEM((H, LANES, 2 * MLSTM_DV), F32),
            pltpu.VMEM((H, 8, LANES), F32),
        ],
        compiler_params=pltpu.CompilerParams(
            dimension_semantics=("arbitrary", "arbitrary"), vmem_limit_bytes=VMEM_LIMIT),
        name="mixer",
    )(x, g_mix.reshape(1, D), w_pack, bg, g_head.reshape(1, vw), w_spatial, bsp,
      g_zv.reshape(1, gw), g_zo.reshape(1, gw), w_out.astype(BF16))


def _kv_kernel(mem_ref, g_ref, w_ref, k_ref, v_ref):
    D = mem_ref.shape[-1]
    kv = _dot(_rms(mem_ref[0], g_ref[...]).astype(BF16), w_ref[...])
    k_ref[0] = kv[:, :D].astype(BF16)
    v_ref[0] = kv[:, D:].astype(BF16)


def _kv_proj(mem, g_mem, w_kv):
    B, M, D = mem.shape
    return pl.pallas_call(
        _kv_kernel,
        out_shape=(jax.ShapeDtypeStruct((B, M, D), BF16), jax.ShapeDtypeStruct((B, M, D), BF16)),
        grid=(B,),
        in_specs=[
            pl.BlockSpec((1, M, D), lambda b: (b, 0, 0)),
            pl.BlockSpec((1, D), lambda b: (0, 0)),
            pl.BlockSpec((D, 2 * D), lambda b: (0, 0)),
        ],
        out_specs=(pl.BlockSpec((1, M, D), lambda b: (b, 0, 0)), pl.BlockSpec((1, M, D), lambda b: (b, 0, 0))),
        compiler_params=pltpu.CompilerParams(dimension_semantics=("arbitrary",), vmem_limit_bytes=VMEM_LIMIT),
        name="kv_proj",
    )(mem, g_mem.reshape(1, D), w_kv.astype(BF16))


def _xattn_kernel(x_ref, gx_ref, wq_ref, k_ref, v_ref, wxo_ref, gmoe_ref, wr_ref, br_ref,
                  x2_ref, xn_ref, idx_ref, gate_ref, rank_ref, cnt_ref, o_scr, cnt_scr):
    ts, D = x_ref.shape[1], x_ref.shape[2]
    hd = D // XATTN_HEADS
    E = wr_ref.shape[0]

    @pl.when((pl.program_id(0) == 0) & (pl.program_id(1) == 0))
    def _():
        cnt_scr[...] = jnp.zeros_like(cnt_scr)

    x = x_ref[0]
    q = _dot(_rms(x, gx_ref[...]).astype(BF16), wq_ref[...]) * (hd ** -0.5)
    for h in range(XATTN_HEADS):
        sl = slice(h * hd, (h + 1) * hd)
        s = _dot_nt(q[:, sl].astype(BF16), k_ref[0, :, sl])
        p = jnp.exp(s - jnp.max(s, axis=1, keepdims=True))
        p = p / jnp.sum(p, axis=1, keepdims=True)
        o_scr[:, sl] = _dot(p.astype(BF16), v_ref[0, :, sl]).astype(BF16)
    x2 = x + _dot(o_scr[...], wxo_ref[...])
    x2_ref[0] = x2

    xn = _rms(x2, gmoe_ref[...])
    xh, xm, _ = _split3(xn)
    wh, wm, _ = _split3(wr_ref[...])
    logits = _dot_nt(wh, xh) + (_dot_nt(wh, xm) + _dot_nt(wm, xh)) + br_ref[...]

    e_iota = lax.broadcasted_iota(jnp.int32, (E, ts), 0)
    work = logits
    tops, idxs, hots = [], [], []
    for _ in range(TOP_K):
        m = jnp.max(work, axis=0, keepdims=True)
        i = jnp.min(jnp.where(work == m, e_iota, E), axis=0, keepdims=True)
        hot = e_iota == i
        work = jnp.where(hot, -jnp.inf, work)
        tops.append(m); idxs.append(i); hots.append(hot)
    ex = [jnp.exp(t - tops[0]) for t in tops]
    tot = ex[0] + ex[1] + ex[2] + ex[3]
    gate_ref[...] = jnp.concatenate([e / tot for e in ex], axis=0)
    idx_ref[...] = jnp.concatenate(idxs, axis=0)

    hot_all = jnp.where(hots[0] | hots[1] | hots[2] | hots[3], 1.0, 0.0)
    r = lax.broadcasted_iota(jnp.int32, (ts, ts), 0)
    c = lax.broadcasted_iota(jnp.int32, (ts, ts), 1)
    before = jnp.where(r < c, 1.0, 0.0).astype(BF16)
    base = cnt_scr[:, 0:1] + _dot(hot_all.astype(BF16), before)
    ranks = [jnp.sum(jnp.where(hot, base, 0.0), axis=0, keepdims=True) for hot in hots]
    rank_ref[...] = jnp.concatenate(ranks, axis=0).astype(jnp.int32)
    cnt_new = cnt_scr[...] + jnp.sum(hot_all, axis=1, keepdims=True)
    cnt_scr[...] = cnt_new
    cnt_ref[...] = cnt_new
    _store_row_tiles(xn_ref, xn)


def _xattn_router(x, k, v, g_xattn, w_q, w_xo, g_moe, w_router, b_router):
    B, S, D = x.shape
    M = k.shape[1]
    E = w_router.shape[1]
    ts = min(XATTN_TOKENS, S)
    assert S % ts == 0
    nj = S // ts
    T = B * S
    const = lambda *shape: pl.BlockSpec(shape, lambda b, j: (0,) * len(shape))
    tok = lambda rows: pl.BlockSpec((rows, ts), lambda b, j: (0, b * nj + j))
    return pl.pallas_call(
        _xattn_kernel,
        out_shape=(
            jax.ShapeDtypeStruct((B, S, D), F32),
            jax.ShapeDtypeStruct((T * ROW_TILE, LANES), F32),
            jax.ShapeDtypeStruct((TOP_K, T), jnp.int32),
            jax.ShapeDtypeStruct((TOP_K, T), F32),
            jax.ShapeDtypeStruct((TOP_K, T), jnp.int32),
            jax.ShapeDtypeStruct((E, LANES), F32),
        ),
        grid=(B, nj),
        in_specs=[
            pl.BlockSpec((1, ts, D), lambda b, j: (b, j, 0)),
            const(1, D), const(D, D),
            pl.BlockSpec((1, M, D), lambda b, j: (b, 0, 0)),
            pl.BlockSpec((1, M, D), lambda b, j: (b, 0, 0)),
            const(D, D), const(1, D), const(E, D), const(E, 1),
        ],
        out_specs=(
            pl.BlockSpec((1, ts, D), lambda b, j: (b, j, 0)),
            pl.BlockSpec((ts * ROW_TILE, LANES), lambda b, j: (b * nj + j, 0)),
            tok(TOP_K), tok(TOP_K), tok(TOP_K),
            const(E, LANES),
        ),
        scratch_shapes=[pltpu.VMEM((ts, D), BF16), pltpu.VMEM((E, LANES), F32)],
        compiler_params=pltpu.CompilerParams(
            dimension_semantics=("arbitrary", "arbitrary"), vmem_limit_bytes=VMEM_LIMIT),
        name="xattn_router",
    )(x, g_xattn.reshape(1, D), w_q.astype(BF16), k, v, w_xo.astype(BF16), g_moe.reshape(1, D),
      w_router.T, b_router.reshape(E, 1))


def _dispatch_kernel(dest_ref, cnt_ref, pend_ref, xn_ref, xs_ref, zero_scr, sem, zsem, *, n_experts):
    tt = xn_ref.shape[0] // ROW_TILE
    R = zero_scr.shape[0] // ROW_TILE

    @pl.when(pl.program_id(0) == 0)
    def _():
        zero_scr[...] = jnp.zeros_like(zero_scr)

        def fill(e, carry):
            @pl.when(cnt_ref[e] > 0)
            def _():
                start = pl.multiple_of((pend_ref[e] - R) * ROW_TILE, R * ROW_TILE)
                cp = pltpu.make_async_copy(zero_scr, xs_ref.at[pl.ds(start, R * ROW_TILE)], zsem)
                cp.start()
                cp.wait()
            return carry

        lax.fori_loop(0, n_experts, fill, 0)

    def issue(grp, carry):
        t0 = pl.multiple_of(grp * ROW_UNROLL, ROW_UNROLL)
        for kk in range(TOP_K):
            for u in range(ROW_UNROLL):
                dst = pl.multiple_of(dest_ref[0, 0, kk * tt + t0 + u] * ROW_TILE, ROW_TILE)
                row = pl.multiple_of((t0 + u) * ROW_TILE, ROW_TILE)
                pltpu.make_async_copy(xn_ref.at[pl.ds(row, ROW_TILE)], xs_ref.at[pl.ds(dst, ROW_TILE)],
                                      sem).start(priority=u % 2)
        return carry

    lax.fori_loop(0, tt // ROW_UNROLL, issue, 0)
    for kk in range(TOP_K):
        pltpu.make_async_copy(xn_ref, xs_ref.at[pl.ds(0, tt * ROW_TILE)], sem).wait()


def _dispatch(xn, dest_blocks, counts, pad_end, n_slots):
    nb, _, per = dest_blocks.shape
    tt = per // TOP_K
    E = counts.shape[0]
    return pl.pallas_call(
        functools.partial(_dispatch_kernel, n_experts=E),
        out_shape=jax.ShapeDtypeStruct((n_slots * ROW_TILE, LANES), F32),
        grid=(nb,),
        in_specs=[
            pl.BlockSpec((1, 1, per), lambda i: (i, 0, 0), memory_space=pltpu.SMEM),
            pl.BlockSpec(memory_space=pltpu.SMEM),
            pl.BlockSpec(memory_space=pltpu.SMEM),
            pl.BlockSpec((tt * ROW_TILE, LANES), lambda i: (i, 0)),
        ],
        out_specs=pl.BlockSpec(memory_space=pl.ANY),
        scratch_shapes=[pltpu.VMEM((MOE_ROWS * ROW_TILE, LANES), F32), pltpu.SemaphoreType.DMA,
                        pltpu.SemaphoreType.DMA],
        compiler_params=pltpu.CompilerParams(
            dimension_semantics=("arbitrary",), vmem_limit_bytes=VMEM_LIMIT, has_side_effects=True),
        name="dispatch",
    )(dest_blocks, counts, pad_end, xn)


def _expert_kernel(be_ref, nu_ref, reg_ref, nxt_ref, xs_ref, wg_ref, bg_ref, wu_ref, bu_ref, wd_ref, bd_ref, y_ref,
                   wf_scr, wg_scr, wu_scr, wd_scr, sem):
    j = pl.program_id(0)
    active = j < nu_ref[0]
    w_hbm = (wg_ref, wu_ref, wd_ref)

    def fetch(e, slot):
        return [pltpu.make_async_copy(w_hbm[i].at[e], wf_scr.at[slot, i], sem.at[slot, i]) for i in range(3)]

    @pl.when(active & ((j == 0) | (be_ref[j] != be_ref[jnp.maximum(j - 1, 0)])))
    def _():
        slot = reg_ref[j] % 2

        @pl.when(j == 0)
        def _():
            for cp in fetch(be_ref[0], 0):
                cp.start()

        for cp in fetch(be_ref[j], slot):
            cp.wait()
        wg_scr[...] = wf_scr[slot, 0].astype(BF16)
        wu_scr[...] = wf_scr[slot, 1].astype(BF16)
        wd_scr[...] = wf_scr[slot, 2].astype(BF16)

        @pl.when(nxt_ref[j] >= 0)
        def _():
            for cp in fetch(nxt_ref[j], 1 - slot):
                cp.start()

    @pl.when(active)
    def _():
        xb = _load_row_tiles(xs_ref, xs_ref.shape[0] // ROW_TILE).astype(BF16)
        g = jnp.minimum(_dot(xb, wg_scr[...]) + bg_ref[0], SWIGLU_LIMIT)
        u = jnp.clip(_dot(xb, wu_scr[...]) + bu_ref[0], -SWIGLU_LIMIT, SWIGLU_LIMIT)
        hdn = (u + 1.0) * (g * jax.nn.sigmoid(SWIGLU_ALPHA * g))
        _store_row_tiles(y_ref, _dot(hdn.astype(BF16), wd_scr[...]) + bd_ref[0])


def _experts(xs, block_e, n_used, region, next_e, w_gate, b_gate, w_up, b_up, w_down, b_down):
    E, D, F = w_gate.shape
    assert D == ROW_TILE * LANES and F == D, "the two weight staging slots hold (D, F) and (F, D) alike"
    n_slots = xs.shape[0] // ROW_TILE
    R = MOE_ROWS
    n_blocks = n_slots // R
    blk = lambda j, be, nu, rg, nx: (jnp.minimum(j, nu[0] - 1), 0)
    wsel = lambda j, be, nu, rg, nx: (be[jnp.minimum(j, nu[0] - 1)], 0, 0)
    hbm = pl.BlockSpec(memory_space=pl.ANY)
    grid_spec = pltpu.PrefetchScalarGridSpec(
        num_scalar_prefetch=4,
        grid=(n_blocks,),
        in_specs=[
            pl.BlockSpec((R * ROW_TILE, LANES), blk),
            hbm, pl.BlockSpec((1, 1, F), wsel),
            hbm, pl.BlockSpec((1, 1, F), wsel),
            hbm, pl.BlockSpec((1, 1, D), wsel),
        ],
        out_specs=pl.BlockSpec((R * ROW_TILE, LANES), blk),
        scratch_shapes=[pltpu.VMEM((2, 3, D, F), F32), pltpu.VMEM((D, F), BF16), pltpu.VMEM((D, F), BF16),
                        pltpu.VMEM((F, D), BF16), pltpu.SemaphoreType.DMA((2, 3))],
    )
    return pl.pallas_call(
        _expert_kernel,
        out_shape=jax.ShapeDtypeStruct((n_slots * ROW_TILE, LANES), F32),
        grid_spec=grid_spec,
        compiler_params=pltpu.CompilerParams(dimension_semantics=("arbitrary",), vmem_limit_bytes=VMEM_LIMIT),
        name="experts",
    )(block_e, n_used, region, next_e, xs, w_gate, b_gate.reshape(E, 1, F), w_up, b_up.reshape(E, 1, F),
      w_down, b_down.reshape(E, 1, D))


def _combine_kernel(dcur_ref, dnext_ref, x_ref, gate_ref, gfin_ref, ys_ref, o_ref, buf, sem):
    tc = x_ref.shape[0]
    i = pl.program_id(0)
    slot = i % 2

    def issue_all(d_ref, into):
        def issue(grp, carry):
            t0 = pl.multiple_of(grp * ROW_UNROLL, ROW_UNROLL)
            for kk in range(TOP_K):
                for u in range(ROW_UNROLL):
                    src = pl.multiple_of(d_ref[0, 0, kk * tc + t0 + u] * ROW_TILE, ROW_TILE)
                    row = pl.multiple_of((t0 + u) * ROW_TILE, ROW_TILE)
                    pltpu.make_async_copy(ys_ref.at[pl.ds(src, ROW_TILE)], buf.at[into, kk, pl.ds(row, ROW_TILE)],
                                          sem.at[into]).start(priority=u % 2)
            return carry

        lax.fori_loop(0, tc // ROW_UNROLL, issue, 0)

    def wait_slot(s):
        for kk in range(TOP_K):
            pltpu.make_async_copy(ys_ref.at[pl.ds(0, tc * ROW_TILE)], buf.at[s, kk], sem.at[s]).wait()

    @pl.when(i == 0)
    def _():
        issue_all(dcur_ref, 0)

    wait_slot(slot)
    other = 1 - slot
    for t in range(tc):
        for kk in range(TOP_K):
            src = pl.multiple_of(dnext_ref[0, 0, kk * tc + t] * ROW_TILE, ROW_TILE)
            pltpu.make_async_copy(ys_ref.at[pl.ds(src, ROW_TILE)], buf.at[other, kk, pl.ds(t * ROW_TILE, ROW_TILE)],
                                  sem.at[other]).start(priority=t % 2)
    for r0 in range(0, tc, COMBINE_ROWS):
        rows = slice(r0, r0 + COMBINE_ROWS)
        acc = x_ref[rows, :]
        for kk in range(TOP_K):
            acc = acc + gate_ref[rows, kk:kk + 1] * _load_row_tiles(buf.at[slot, kk], COMBINE_ROWS, r0)
        o_ref[rows, :] = _rms(acc, gfin_ref[...])

    @pl.when(i == pl.num_programs(0) - 1)
    def _():
        wait_slot(other)


def _combine(x2, gate_t, dest_blocks, ys, g_final):
    T, D = x2.shape
    nb, _, per = dest_blocks.shape
    tc = per // TOP_K
    return pl.pallas_call(
        _combine_kernel,
        out_shape=jax.ShapeDtypeStruct((T, D), F32),
        grid=(nb,),
        in_specs=[
            pl.BlockSpec((1, 1, per), lambda i: (i, 0, 0), memory_space=pltpu.SMEM),
            pl.BlockSpec((1, 1, per), lambda i: (jnp.minimum(i + 1, nb - 1), 0, 0), memory_space=pltpu.SMEM),
            pl.BlockSpec((tc, D), lambda i: (i, 0)),
            pl.BlockSpec((tc, TOP_K), lambda i: (i, 0)),
            pl.BlockSpec((1, D), lambda i: (0, 0)),
            pl.BlockSpec(memory_space=pl.ANY),
        ],
        out_specs=pl.BlockSpec((tc, D), lambda i: (i, 0)),
        scratch_shapes=[pltpu.VMEM((2, TOP_K, tc * ROW_TILE, LANES), F32), pltpu.SemaphoreType.DMA((2,))],
        compiler_params=pltpu.CompilerParams(dimension_semantics=("arbitrary",), vmem_limit_bytes=VMEM_LIMIT),
        name="combine",
    )(dest_blocks, dest_blocks, x2, gate_t, g_final.reshape(1, D), ys)


def _block_dest(dest, tokens):
    T = dest.shape[1]
    return dest.reshape(TOP_K, T // tokens, tokens).transpose(1, 0, 2).reshape(T // tokens, 1, TOP_K * tokens)


def _moe(x2, xn, idx, gate, rank, cnt, w_gate, b_gate, w_up, b_up, w_down, b_down, g_final):
    T = x2.shape[0]
    E = w_gate.shape[0]
    R = MOE_ROWS
    n_blocks = -(-T * TOP_K // R) + E
    n_slots = n_blocks * R
    counts = cnt[:, 0].astype(jnp.int32)
    padded = (counts + R - 1) // R * R
    pad_end = jnp.cumsum(padded)
    pad_start = pad_end - padded
    dest = rank
    for e in range(E):
        dest = dest + jnp.where(idx == e, pad_start[e], 0)
    n_used = (pad_end[-1:] // R).astype(jnp.int32)
    block_e = jnp.minimum(jnp.sum(jnp.arange(n_blocks)[:, None] * R >= pad_end[None, :], axis=1), E - 1)
    block_e = block_e.astype(jnp.int32)
    blocks = jnp.arange(n_blocks)
    first = (blocks < n_used[0]) & ((blocks == 0) | (block_e != jnp.roll(block_e, 1)))
    region = (jnp.cumsum(first) - 1).astype(jnp.int32)
    later = jnp.where((counts[None, :] > 0) & (jnp.arange(E)[None, :] > block_e[:, None]), jnp.arange(E)[None, :], E)
    next_e = jnp.min(later, axis=1)
    next_e = jnp.where(next_e == E, -1, next_e).astype(jnp.int32)

    xs = _dispatch(xn, _block_dest(dest, min(DISPATCH_TOKENS, T)), counts, pad_end.astype(jnp.int32), n_slots)
    ys = _experts(xs, block_e, n_used, region, next_e, w_gate, b_gate, w_up, b_up, w_down, b_down)
    return _combine(x2, gate.T, _block_dest(dest, min(COMBINE_TOKENS, T)), ys, g_final)


def kernel(x, mem, g_mix, w_in, b_gates, g_mlstm_head, w_spatial, b_spatial, g_gmlp_v, g_gmlp_out, w_out,
           g_xattn, g_mem, w_q, w_kv, w_xo, g_moe, w_router, b_router, w_gate, b_gate, w_up, b_up,
           w_down, b_down, g_final):
    B, S, D = x.shape
    assert g_mix.shape[0] == 1, "the combine kernel fuses the closing norm, so exactly one layer is supported"
    x1 = _mixer(x, g_mix[0], w_in[0], b_gates[0], g_mlstm_head[0], w_spatial[0], b_spatial[0],
                g_gmlp_v[0], g_gmlp_out[0], w_out[0])
    k, v = _kv_proj(mem, g_mem[0], w_kv[0])
    x2, xn, idx, gate, rank, cnt = _xattn_router(x1, k, v, g_xattn[0], w_q[0], w_xo[0], g_moe[0],
                                                 w_router[0], b_router[0])
    out = _moe(x2.reshape(B * S, D), xn, idx, gate, rank, cnt, w_gate[0], b_gate[0], w_up[0], b_up[0],
               w_down[0], b_down[0], g_final)
    return out.reshape(B, S, D)
```

```python
import functools

import jax
import jax.numpy as jnp
from jax import lax
from jax.experimental import pallas as pl
from jax.experimental.pallas import tpu as pltpu

F32 = jnp.float32
BF16 = jnp.bfloat16

MLSTM_HEADS = 4
MLSTM_DK = 64
MLSTM_DV = 128
CHUNK = 128
GMLP_GROUPS = 4
GMLP_DG = 128
XATTN_HEADS = 4
TOP_K = 4
GATE_SOFTCAP = 15.0
SWIGLU_LIMIT = 7.0
SWIGLU_ALPHA = 1.702
NORM_EPS = 1e-6

LANES = 128
ROW_TILE = 8

MIX_TOKENS = 512
XATTN_TOKENS = 512
MOE_ROWS = 512
DISPATCH_TOKENS = 2048
COMBINE_TOKENS = 256
COMBINE_ROWS = 32
ROW_UNROLL = 16
VMEM_LIMIT = 52 * 1024 * 1024


def _rms(x, g):
    return x * lax.rsqrt(jnp.mean(x * x, axis=-1, keepdims=True) + NORM_EPS) * g


def _gelu_tanh(x):
    return 0.5 * x * (1.0 + jnp.tanh(0.7978845608028654 * (x + 0.044715 * x * x * x)))


def _dot(a, b):
    return jnp.dot(a, b, preferred_element_type=F32)


def _dot_nt(a, b):
    return lax.dot_general(a, b, (((1,), (1,)), ((), ())), preferred_element_type=F32)


def _load_row_tiles(ref, rows, first=0):
    return jnp.concatenate(
        [ref[pl.ds(first * ROW_TILE + j, rows, stride=ROW_TILE), :] for j in range(ROW_TILE)], axis=1)


def _store_row_tiles(ref, val):
    for j in range(ROW_TILE):
        ref[pl.ds(j, val.shape[0], stride=ROW_TILE), :] = val[:, j * LANES:(j + 1) * LANES]


def _split3(x):
    hi = x.astype(BF16)
    r1 = x - hi.astype(F32)
    mid = r1.astype(BF16)
    lo = (r1 - mid.astype(F32)).astype(BF16)
    return hi, mid, lo


def _mixer_kernel(x_ref, gmix_ref, win_ref, bg_ref, ghead_ref, wsp_ref, bsp_ref, gzv_ref, gzo_ref, wout_ref,
                  o_ref, proj_scr, y_scr, ct_scr, m_scr, *, n_chunks, cols):
    qk0, v0, og0, gu0, gv0, gt0 = cols
    H, G, L = MLSTM_HEADS, GMLP_GROUPS, CHUNK

    @pl.when(pl.program_id(1) == 0)
    def _():
        ct_scr[...] = jnp.zeros_like(ct_scr)
        m_scr[...] = jnp.zeros_like(m_scr)

    row = lax.broadcasted_iota(jnp.int32, (L, L), 0)
    col = lax.broadcasted_iota(jnp.int32, (L, L), 1)
    causal = row >= col
    tril = jnp.where(causal, 1.0, 0.0).astype(BF16)
    triu = jnp.where(row <= col, 1.0, 0.0).astype(BF16)
    lane = lax.broadcasted_iota(jnp.int32, (L, LANES), 1)
    ones_col = jnp.where(lane == 0, 1.0, 0.0).astype(F32)

    x = x_ref[0]
    proj_scr[...] = _dot(_rms(x, gmix_ref[...]).astype(BF16), win_ref[...])

    for c in range(n_chunks):
        rows = slice(c * L, (c + 1) * L)
        pre = proj_scr[rows, gt0:gt0 + LANES] + bg_ref[...]
        capped = GATE_SOFTCAP * jnp.tanh(pre * (1.0 / GATE_SOFTCAP))
        log_sig = jnp.minimum(capped, 0.0) - jnp.log1p(jnp.exp(-jnp.abs(capped)))
        lg = jnp.where(lane < H, capped, log_sig)
        lg_t = lg.T[0:8, :]
        bcol = sum(_dot(tril, p) for p in _split3(lg))
        brow = sum(_dot(p, triu) for p in _split3(lg_t))

        for h in range(H):
            b_c = bcol[:, H + h:H + h + 1]
            i_c = lg[:, h:h + 1]
            b_r = brow[H + h:H + h + 1, :]
            i_r = lg_t[h:h + 1, :]
            m_prev = m_scr[h, 0:1, 0:1]
            qk = proj_scr[rows, qk0 + h * LANES:qk0 + (h + 1) * LANES]
            q = jnp.where(lane < MLSTM_DK, qk * (MLSTM_DK ** -0.5), 0.0)
            k = jnp.where(lane < MLSTM_DK, pltpu.roll(qk, MLSTM_DK, axis=1), 0.0)
            v = proj_scr[rows, v0 + h * MLSTM_DV:v0 + (h + 1) * MLSTM_DV]
            qb = q.astype(BF16)
            kb = k.astype(BF16)
            vaug = jnp.concatenate([v, ones_col], axis=1)

            d = jnp.where(causal, b_c - b_r + i_r, -jnp.inf)
            inter = b_c + m_prev
            m_t = jnp.maximum(inter, jnp.max(d, axis=1, keepdims=True))
            w_intra = jnp.exp(d - m_t)
            w_inter = jnp.exp(inter - m_t)
            s = _dot_nt(qb, kb) * w_intra
            ct = ct_scr[h]
            na = _dot(s.astype(BF16), vaug.astype(BF16)) + w_inter * _dot(qb, ct.astype(BF16))
            num = na[:, :MLSTM_DV]
            den = na[:, MLSTM_DV:MLSTM_DV + 1]
            hh = num / jnp.maximum(jnp.abs(den), jnp.exp(-m_t))

            b_last = b_c[L - 1:L, :]
            g_c = b_last - b_c + i_c
            m_new = jnp.maximum(b_last + m_prev, jnp.max(g_c, axis=0, keepdims=True))
            wk = jnp.exp(g_c - m_new)
            decay = jnp.exp(b_last + m_prev - m_new)
            ct_scr[h] = decay * ct + _dot(k.T.astype(BF16), (wk * vaug).astype(BF16))
            m_scr[h] = jnp.broadcast_to(m_new, m_scr.shape[1:])

            hn = _rms(hh, ghead_ref[:, h * MLSTM_DV:(h + 1) * MLSTM_DV])
            og = proj_scr[rows, og0 + h * MLSTM_DV:og0 + (h + 1) * MLSTM_DV]
            y_scr[rows, h * MLSTM_DV:(h + 1) * MLSTM_DV] = (jax.nn.sigmoid(og) * hn).astype(BF16)

        for g in range(G):
            sl = slice(g * GMLP_DG, (g + 1) * GMLP_DG)
            u = _gelu_tanh(proj_scr[rows, gu0 + g * GMLP_DG:gu0 + (g + 1) * GMLP_DG])
            z = _rms(_gelu_tanh(proj_scr[rows, gv0 + g * GMLP_DG:gv0 + (g + 1) * GMLP_DG]), gzv_ref[:, sl])
            wsp = jnp.where(causal, wsp_ref[g], 0.0)
            mixed = _dot(wsp.astype(BF16), z.astype(BF16)) + bsp_ref[g]
            yg = _rms(u * mixed, gzo_ref[:, sl])
            y_scr[rows, H * MLSTM_DV + g * GMLP_DG:H * MLSTM_DV + (g + 1) * GMLP_DG] = yg.astype(BF16)
    o_ref[0] = x + _dot(y_scr[...], wout_ref[...])


def _mixer(x, g_mix, w_in, b_gates, g_head, w_spatial, b_spatial, g_zv, g_zo, w_out):
    B, S, D = x.shape
    H, G, L = MLSTM_HEADS, GMLP_GROUPS, CHUNK
    qw, vw, gw = H * MLSTM_DK, H * MLSTM_DV, G * GMLP_DG
    ts = min(MIX_TOKENS, S)
    assert S % ts == 0 and ts % L == 0

    assert 2 * MLSTM_DK == LANES
    c = 0
    w_q = w_in[:, c:c + qw].reshape(D, H, MLSTM_DK); c += qw
    w_k = w_in[:, c:c + qw].reshape(D, H, MLSTM_DK); c += qw
    w_qk = jnp.concatenate([w_q, w_k], axis=2).reshape(D, H * LANES)
    w_v = w_in[:, c:c + vw]; c += vw
    w_o = w_in[:, c:c + vw]; c += vw
    w_g = jnp.pad(w_in[:, c:c + 2 * H], ((0, 0), (0, LANES - 2 * H))); c += 2 * H
    w_gu = w_in[:, c:c + gw]; c += gw
    w_gv = w_in[:, c:c + gw]; c += gw
    w_pack = jnp.concatenate([w_qk, w_v, w_o, w_gu, w_gv, w_g], axis=1).astype(BF16)
    hp = H * LANES
    cols = (0, hp, hp + vw, hp + 2 * vw, hp + 2 * vw + gw, hp + 2 * vw + 2 * gw)
    n_cols = w_pack.shape[1]
    bg = jnp.pad(b_gates, (0, LANES - 2 * H)).reshape(1, LANES)
    bsp = jnp.broadcast_to(b_spatial[:, :, None], (G, L, GMLP_DG))

    const = lambda *shape: pl.BlockSpec(shape, lambda b, j: (0,) * len(shape))
    return pl.pallas_call(
        functools.partial(_mixer_kernel, n_chunks=ts // L, cols=cols),
        out_shape=jax.ShapeDtypeStruct((B, S, D), F32),
        grid=(B, S // ts),
        in_specs=[
            pl.BlockSpec((1, ts, D), lambda b, j: (b, j, 0)),
            const(1, D), const(D, n_cols), const(1, LANES), const(1, vw),
            const(G, L, L), const(G, L, GMLP_DG), const(1, gw), const(1, gw), const(vw + gw, D),
        ],
        out_specs=pl.BlockSpec((1, ts, D), lambda b, j: (b, j, 0)),
        scratch_shapes=[
            pltpu.VMEM((ts, n_cols), F32),
            pltpu.VMEM((ts, vw + gw), BF16),
            pltpu.VMEM((H, LANES, 2 * MLSTM_DV), F32),
            pltpu.VMEM((H, 8, LANES), F32),
        ],
        compiler_params=pltpu.CompilerParams(
            dimension_semantics=("arbitrary", "arbitrary"), vmem_limit_bytes=VMEM_LIMIT),
        name="mixer",
    )(x, g_mix.reshape(1, D), w_pack, bg, g_head.reshape(1, vw), w_spatial, bsp,
      g_zv.reshape(1, gw), g_zo.reshape(1, gw), w_out.astype(BF16))


def _kv_kernel(mem_ref, g_ref, w_ref, k_ref, v_ref):
    D = mem_ref.shape[-1]
    kv = _dot(_rms(mem_ref[0], g_ref[...]).astype(BF16), w_ref[...])
    k_ref[0] = kv[:, :D].astype(BF16)
    v_ref[0] = kv[:, D:].astype(BF16)


def _kv_proj(mem, g_mem, w_kv):
    B, M, D = mem.shape
    return pl.pallas_call(
        _kv_kernel,
        out_shape=(jax.ShapeDtypeStruct((B, M, D), BF16), jax.ShapeDtypeStruct((B, M, D), BF16)),
        grid=(B,),
        in_specs=[
            pl.BlockSpec((1, M, D), lambda b: (b, 0, 0)),
            pl.BlockSpec((1, D), lambda b: (0, 0)),
            pl.BlockSpec((D, 2 * D), lambda b: (0, 0)),
        ],
        out_specs=(pl.BlockSpec((1, M, D), lambda b: (b, 0, 0)), pl.BlockSpec((1, M, D), lambda b: (b, 0, 0))),
        compiler_params=pltpu.CompilerParams(dimension_semantics=("arbitrary",), vmem_limit_bytes=VMEM_LIMIT),
        name="kv_proj",
    )(mem, g_mem.reshape(1, D), w_kv.astype(BF16))


def _xattn_kernel(x_ref, gx_ref, wq_ref, k_ref, v_ref, wxo_ref, gmoe_ref, wr_ref, br_ref,
                  x2_ref, xn_ref, idx_ref, gate_ref, rank_ref, cnt_ref, o_scr, cnt_scr):
    ts, D = x_ref.shape[1], x_ref.shape[2]
    hd = D // XATTN_HEADS
    E = wr_ref.shape[0]

    @pl.when((pl.program_id(0) == 0) & (pl.program_id(1) == 0))
    def _():
        cnt_scr[...] = jnp.zeros_like(cnt_scr)

    x = x_ref[0]
    q = _dot(_rms(x, gx_ref[...]).astype(BF16), wq_ref[...]) * (hd ** -0.5)
    for h in range(XATTN_HEADS):
        sl = slice(h * hd, (h + 1) * hd)
        s = _dot_nt(q[:, sl].astype(BF16), k_ref[0, :, sl])
        p = jnp.exp(s - jnp.max(s, axis=1, keepdims=True))
        p = p / jnp.sum(p, axis=1, keepdims=True)
        o_scr[:, sl] = _dot(p.astype(BF16), v_ref[0, :, sl]).astype(BF16)
    x2 = x + _dot(o_scr[...], wxo_ref[...])
    x2_ref[0] = x2

    xn = _rms(x2, gmoe_ref[...])
    xh, xm, _ = _split3(xn)
    wh, wm, _ = _split3(wr_ref[...])
    logits = _dot_nt(wh, xh) + (_dot_nt(wh, xm) + _dot_nt(wm, xh)) + br_ref[...]

    e_iota = lax.broadcasted_iota(jnp.int32, (E, ts), 0)
    work = logits
    tops, idxs, hots = [], [], []
    for _ in range(TOP_K):
        m = jnp.max(work, axis=0, keepdims=True)
        i = jnp.min(jnp.where(work == m, e_iota, E), axis=0, keepdims=True)
        hot = e_iota == i
        work = jnp.where(hot, -jnp.inf, work)
        tops.append(m); idxs.append(i); hots.append(hot)
    ex = [jnp.exp(t - tops[0]) for t in tops]
    tot = ex[0] + ex[1] + ex[2] + ex[3]
    gate_ref[...] = jnp.concatenate([e / tot for e in ex], axis=0)
    idx_ref[...] = jnp.concatenate(idxs, axis=0)

    hot_all = jnp.where(hots[0] | hots[1] | hots[2] | hots[3], 1.0, 0.0)
    r = lax.broadcasted_iota(jnp.int32, (ts, ts), 0)
    c = lax.broadcasted_iota(jnp.int32, (ts, ts), 1)
    before = jnp.where(r < c, 1.0, 0.0).astype(BF16)
    base = cnt_scr[:, 0:1] + _dot(hot_all.astype(BF16), before)
    ranks = [jnp.sum(jnp.where(hot, base, 0.0), axis=0, keepdims=True) for hot in hots]
    rank_ref[...] = jnp.concatenate(ranks, axis=0).astype(jnp.int32)
    cnt_new = cnt_scr[...] + jnp.sum(hot_all, axis=1, keepdims=True)
    cnt_scr[...] = cnt_new
    cnt_ref[...] = cnt_new
    _store_row_tiles(xn_ref, xn)


def _xattn_router(x, k, v, g_xattn, w_q, w_xo, g_moe, w_router, b_router):
    B, S, D = x.shape
    M = k.shape[1]
    E = w_router.shape[1]
    ts = min(XATTN_TOKENS, S)
    assert S % ts == 0
    nj = S // ts
    T = B * S
    const = lambda *shape: pl.BlockSpec(shape, lambda b, j: (0,) * len(shape))
    tok = lambda rows: pl.BlockSpec((rows, ts), lambda b, j: (0, b * nj + j))
    return pl.pallas_call(
        _xattn_kernel,
        out_shape=(
            jax.ShapeDtypeStruct((B, S, D), F32),
            jax.ShapeDtypeStruct((T * ROW_TILE, LANES), F32),
            jax.ShapeDtypeStruct((TOP_K, T), jnp.int32),
            jax.ShapeDtypeStruct((TOP_K, T), F32),
            jax.ShapeDtypeStruct((TOP_K, T), jnp.int32),
            jax.ShapeDtypeStruct((E, LANES), F32),
        ),
        grid=(B, nj),
        in_specs=[
            pl.BlockSpec((1, ts, D), lambda b, j: (b, j, 0)),
            const(1, D), const(D, D),
            pl.BlockSpec((1, M, D), lambda b, j: (b, 0, 0)),
            pl.BlockSpec((1, M, D), lambda b, j: (b, 0, 0)),
            const(D, D), const(1, D), const(E, D), const(E, 1),
        ],
        out_specs=(
            pl.BlockSpec((1, ts, D), lambda b, j: (b, j, 0)),
            pl.BlockSpec((ts * ROW_TILE, LANES), lambda b, j: (b * nj + j, 0)),
            tok(TOP_K), tok(TOP_K), tok(TOP_K),
            const(E, LANES),
        ),
        scratch_shapes=[pltpu.VMEM((ts, D), BF16), pltpu.VMEM((E, LANES), F32)],
        compiler_params=pltpu.CompilerParams(
            dimension_semantics=("arbitrary", "arbitrary"), vmem_limit_bytes=VMEM_LIMIT),
        name="xattn_router",
    )(x, g_xattn.reshape(1, D), w_q.astype(BF16), k, v, w_xo.astype(BF16), g_moe.reshape(1, D),
      w_router.T, b_router.reshape(E, 1))


def _dispatch_kernel(dest_ref, cnt_ref, pend_ref, xn_ref, xs_ref, zero_scr, sem, zsem, *, n_experts):
    tt = xn_ref.shape[0] // ROW_TILE
    R = zero_scr.shape[0] // ROW_TILE

    @pl.when(pl.program_id(0) == 0)
    def _():
        zero_scr[...] = jnp.zeros_like(zero_scr)

        def fill_copy(e):
            start = pl.multiple_of((pend_ref[e] - R) * ROW_TILE, R * ROW_TILE)
            return pltpu.make_async_copy(zero_scr, xs_ref.at[pl.ds(start, R * ROW_TILE)], zsem)

        def start_fill(e, carry):
            @pl.when(cnt_ref[e] > 0)
            def _():
                fill_copy(e).start()
            return carry

        def wait_fill(e, carry):
            @pl.when(cnt_ref[e] > 0)
            def _():
                fill_copy(e).wait()
            return carry

        lax.fori_loop(0, n_experts, start_fill, 0)
        lax.fori_loop(0, n_experts, wait_fill, 0)

    def issue(grp, carry):
        t0 = pl.multiple_of(grp * ROW_UNROLL, ROW_UNROLL)
        for kk in range(TOP_K):
            for u in range(ROW_UNROLL):
                dst = pl.multiple_of(dest_ref[0, 0, kk * tt + t0 + u] * ROW_TILE, ROW_TILE)
                row = pl.multiple_of((t0 + u) * ROW_TILE, ROW_TILE)
                pltpu.make_async_copy(xn_ref.at[pl.ds(row, ROW_TILE)], xs_ref.at[pl.ds(dst, ROW_TILE)],
                                      sem).start(priority=u % 2)
        return carry

    lax.fori_loop(0, tt // ROW_UNROLL, issue, 0)
    for kk in range(TOP_K):
        pltpu.make_async_copy(xn_ref, xs_ref.at[pl.ds(0, tt * ROW_TILE)], sem).wait()


def _dispatch(xn, dest, counts, pad_end, n_slots):
    T = dest.shape[1]
    tt = min(DISPATCH_TOKENS, T)
    assert T % tt == 0
    E = counts.shape[0]
    dest = dest.reshape(TOP_K, T // tt, tt).transpose(1, 0, 2).reshape(T // tt, 1, TOP_K * tt)
    return pl.pallas_call(
        functools.partial(_dispatch_kernel, n_experts=E),
        out_shape=jax.ShapeDtypeStruct((n_slots * ROW_TILE, LANES), F32),
        grid=(T // tt,),
        in_specs=[
            pl.BlockSpec((1, 1, TOP_K * tt), lambda i: (i, 0, 0), memory_space=pltpu.SMEM),
            pl.BlockSpec(memory_space=pltpu.SMEM),
            pl.BlockSpec(memory_space=pltpu.SMEM),
            pl.BlockSpec((tt * ROW_TILE, LANES), lambda i: (i, 0)),
        ],
        out_specs=pl.BlockSpec(memory_space=pl.ANY),
        scratch_shapes=[pltpu.VMEM((MOE_ROWS * ROW_TILE, LANES), F32), pltpu.SemaphoreType.DMA,
                        pltpu.SemaphoreType.DMA],
        compiler_params=pltpu.CompilerParams(
            dimension_semantics=("arbitrary",), vmem_limit_bytes=VMEM_LIMIT, has_side_effects=True),
        name="dispatch",
    )(dest, counts, pad_end, xn)


def _expert_kernel(be_ref, nu_ref, reg_ref, nxt_ref, xs_ref, wg_ref, bg_ref, wu_ref, bu_ref, wd_ref, bd_ref, y_ref,
                   wf_scr, wg_scr, wu_scr, wd_scr, sem):
    j = pl.program_id(0)
    active = j < nu_ref[0]
    w_hbm = (wg_ref, wu_ref, wd_ref)

    def fetch(e, slot):
        return [pltpu.make_async_copy(w_hbm[i].at[e], wf_scr.at[slot, i], sem.at[slot, i]) for i in range(3)]

    @pl.when(active & ((j == 0) | (be_ref[j] != be_ref[jnp.maximum(j - 1, 0)])))
    def _():
        slot = reg_ref[j] % 2

        @pl.when(j == 0)
        def _():
            for cp in fetch(be_ref[0], 0):
                cp.start()

        for cp in fetch(be_ref[j], slot):
            cp.wait()
        wg_scr[...] = wf_scr[slot, 0].astype(BF16)
        wu_scr[...] = wf_scr[slot, 1].astype(BF16)
        wd_scr[...] = wf_scr[slot, 2].astype(BF16)

        @pl.when(nxt_ref[j] >= 0)
        def _():
            for cp in fetch(nxt_ref[j], 1 - slot):
                cp.start()

    @pl.when(active)
    def _():
        xb = _load_row_tiles(xs_ref, xs_ref.shape[0] // ROW_TILE).astype(BF16)
        g = jnp.minimum(_dot(xb, wg_scr[...]) + bg_ref[0], SWIGLU_LIMIT)
        u = jnp.clip(_dot(xb, wu_scr[...]) + bu_ref[0], -SWIGLU_LIMIT, SWIGLU_LIMIT)
        hdn = (u + 1.0) * (g * jax.nn.sigmoid(SWIGLU_ALPHA * g))
        _store_row_tiles(y_ref, _dot(hdn.astype(BF16), wd_scr[...]) + bd_ref[0])


def _experts(xs, block_e, n_used, region, next_e, w_gate, b_gate, w_up, b_up, w_down, b_down):
    E, D, F = w_gate.shape
    assert D == ROW_TILE * LANES and F == D, "the two weight staging slots hold (D, F) and (F, D) alike"
    n_slots = xs.shape[0] // ROW_TILE
    R = MOE_ROWS
    n_blocks = n_slots // R
    blk = lambda j, be, nu, rg, nx: (jnp.minimum(j, nu[0] - 1), 0)
    wsel = lambda j, be, nu, rg, nx: (be[jnp.minimum(j, nu[0] - 1)], 0, 0)
    hbm = pl.BlockSpec(memory_space=pl.ANY)
    grid_spec = pltpu.PrefetchScalarGridSpec(
        num_scalar_prefetch=4,
        grid=(n_blocks,),
        in_specs=[
            pl.BlockSpec((R * ROW_TILE, LANES), blk),
            hbm, pl.BlockSpec((1, 1, F), wsel),
            hbm, pl.BlockSpec((1, 1, F), wsel),
            hbm, pl.BlockSpec((1, 1, D), wsel),
        ],
        out_specs=pl.BlockSpec((R * ROW_TILE, LANES), blk),
        scratch_shapes=[pltpu.VMEM((2, 3, D, F), F32), pltpu.VMEM((D, F), BF16), pltpu.VMEM((D, F), BF16),
                        pltpu.VMEM((F, D), BF16), pltpu.SemaphoreType.DMA((2, 3))],
    )
    return pl.pallas_call(
        _expert_kernel,
        out_shape=jax.ShapeDtypeStruct((n_slots * ROW_TILE, LANES), F32),
        grid_spec=grid_spec,
        compiler_params=pltpu.CompilerParams(dimension_semantics=("arbitrary",), vmem_limit_bytes=VMEM_LIMIT),
        name="experts",
    )(block_e, n_used, region, next_e, xs, w_gate, b_gate.reshape(E, 1, F), w_up, b_up.reshape(E, 1, F),
      w_down, b_down.reshape(E, 1, D))


def _combine_kernel(dcur_ref, dnext_ref, x_ref, gate_ref, gfin_ref, ys_ref, o_ref, buf, sem):
    tc = x_ref.shape[0]
    i = pl.program_id(0)
    slot = i % 2

    def issue_all(d_ref, into):
        def issue(grp, carry):
            t0 = pl.multiple_of(grp * ROW_UNROLL, ROW_UNROLL)
            for kk in range(TOP_K):
                for u in range(ROW_UNROLL):
                    src = pl.multiple_of(d_ref[kk, t0 + u] * ROW_TILE, ROW_TILE)
                    row = pl.multiple_of((t0 + u) * ROW_TILE, ROW_TILE)
                    pltpu.make_async_copy(ys_ref.at[pl.ds(src, ROW_TILE)], buf.at[into, kk, pl.ds(row, ROW_TILE)],
                                          sem.at[into]).start(priority=u % 2)
            return carry

        lax.fori_loop(0, tc // ROW_UNROLL, issue, 0)

    def wait_slot(s):
        for kk in range(TOP_K):
            pltpu.make_async_copy(ys_ref.at[pl.ds(0, tc * ROW_TILE)], buf.at[s, kk], sem.at[s]).wait()

    @pl.when(i == 0)
    def _():
        issue_all(dcur_ref, 0)

    wait_slot(slot)
    other = 1 - slot
    for t in range(tc):
        for kk in range(TOP_K):
            src = pl.multiple_of(dnext_ref[kk, t] * ROW_TILE, ROW_TILE)
            pltpu.make_async_copy(ys_ref.at[pl.ds(src, ROW_TILE)], buf.at[other, kk, pl.ds(t * ROW_TILE, ROW_TILE)],
                                  sem.at[other]).start(priority=t % 2)
    for r0 in range(0, tc, COMBINE_ROWS):
        rows = slice(r0, r0 + COMBINE_ROWS)
        acc = x_ref[rows, :]
        for kk in range(TOP_K):
            acc = acc + gate_ref[rows, kk:kk + 1] * _load_row_tiles(buf.at[slot, kk], COMBINE_ROWS, r0)
        o_ref[rows, :] = _rms(acc, gfin_ref[...])

    @pl.when(i == pl.num_programs(0) - 1)
    def _():
        wait_slot(other)


def _combine(x2, gate_t, dest, ys, g_final):
    T, D = x2.shape
    tc = min(COMBINE_TOKENS, T)
    assert T % tc == 0
    nb = T // tc
    return pl.pallas_call(
        _combine_kernel,
        out_shape=jax.ShapeDtypeStruct((T, D), F32),
        grid=(nb,),
        in_specs=[
            pl.BlockSpec((TOP_K, tc), lambda i: (0, i), memory_space=pltpu.SMEM),
            pl.BlockSpec((TOP_K, tc), lambda i: (0, jnp.minimum(i + 1, nb - 1)), memory_space=pltpu.SMEM),
            pl.BlockSpec((tc, D), lambda i: (i, 0)),
            pl.BlockSpec((tc, TOP_K), lambda i: (i, 0)),
            pl.BlockSpec((1, D), lambda i: (0, 0)),
            pl.BlockSpec(memory_space=pl.ANY),
        ],
        out_specs=pl.BlockSpec((tc, D), lambda i: (i, 0)),
        scratch_shapes=[pltpu.VMEM((2, TOP_K, tc * ROW_TILE, LANES), F32), pltpu.SemaphoreType.DMA((2,))],
        compiler_params=pltpu.CompilerParams(dimension_semantics=("arbitrary",), vmem_limit_bytes=VMEM_LIMIT),
        name="combine",
    )(dest, dest, x2, gate_t, g_final.reshape(1, D), ys)


def _moe(x2, xn, idx, gate, rank, cnt, w_gate, b_gate, w_up, b_up, w_down, b_down, g_final):
    T = x2.shape[0]
    E = w_gate.shape[0]
    R = MOE_ROWS
    n_blocks = -(-T * TOP_K // R) + E
    n_slots = n_blocks * R
    counts = cnt[:, 0].astype(jnp.int32)
    padded = (counts + R - 1) // R * R
    pad_end = jnp.cumsum(padded)
    pad_start = pad_end - padded
    dest = rank
    for e in range(E):
        dest = dest + jnp.where(idx == e, pad_start[e], 0)
    n_used = (pad_end[-1:] // R).astype(jnp.int32)
    block_e = jnp.minimum(jnp.sum(jnp.arange(n_blocks)[:, None] * R >= pad_end[None, :], axis=1), E - 1)
    block_e = block_e.astype(jnp.int32)
    blocks = jnp.arange(n_blocks)
    first = (blocks < n_used[0]) & ((blocks == 0) | (block_e != jnp.roll(block_e, 1)))
    region = (jnp.cumsum(first) - 1).astype(jnp.int32)
    later = jnp.where((counts[None, :] > 0) & (jnp.arange(E)[None, :] > block_e[:, None]), jnp.arange(E)[None, :], E)
    next_e = jnp.min(later, axis=1)
    next_e = jnp.where(next_e == E, -1, next_e).astype(jnp.int32)

    xs = _dispatch(xn, dest, counts, pad_end.astype(jnp.int32), n_slots)
    ys = _experts(xs, block_e, n_used, region, next_e, w_gate, b_gate, w_up, b_up, w_down, b_down)
    return _combine(x2, gate.T, dest, ys, g_final)


def kernel(x, mem, g_mix, w_in, b_gates, g_mlstm_head, w_spatial, b_spatial, g_gmlp_v, g_gmlp_out, w_out,
           g_xattn, g_mem, w_q, w_kv, w_xo, g_moe, w_router, b_router, w_gate, b_gate, w_up, b_up,
           w_down, b_down, g_final):
    B, S, D = x.shape
    assert g_mix.shape[0] == 1, "the combine kernel fuses the closing norm, so exactly one layer is supported"
    x1 = _mixer(x, g_mix[0], w_in[0], b_gates[0], g_mlstm_head[0], w_spatial[0], b_spatial[0],
                g_gmlp_v[0], g_gmlp_out[0], w_out[0])
    k, v = _kv_proj(mem, g_mem[0], w_kv[0])
    x2, xn, idx, gate, rank, cnt = _xattn_router(x1, k, v, g_xattn[0], w_q[0], w_xo[0], g_moe[0],
                                                 w_router[0], b_router[0])
    out = _moe(x2.reshape(B * S, D), xn, idx, gate, rank, cnt, w_gate[0], b_gate[0], w_up[0], b_up[0],
               w_down[0], b_down[0], g_final)
    return out.reshape(B, S, D)
```

```python
import functools

import jax
import jax.numpy as jnp
from jax import lax
from jax.experimental import pallas as pl
from jax.experimental.pallas import tpu as pltpu

F32 = jnp.float32
BF16 = jnp.bfloat16

MLSTM_HEADS = 4
MLSTM_DK = 64
MLSTM_DV = 128
CHUNK = 128
GMLP_GROUPS = 4
GMLP_DG = 128
XATTN_HEADS = 4
TOP_K = 4
GATE_SOFTCAP = 15.0
SWIGLU_LIMIT = 7.0
SWIGLU_ALPHA = 1.702
NORM_EPS = 1e-6

LANES = 128
ROW_TILE = 8

MIX_TOKENS = 512
XATTN_TOKENS = 512
MOE_ROWS = 512
DISPATCH_TOKENS = 2048
COMBINE_TOKENS = 512
COMBINE_ROWS = 32
ROW_UNROLL = 16
VMEM_LIMIT = 52 * 1024 * 1024


def _rms(x, g):
    return x * lax.rsqrt(jnp.mean(x * x, axis=-1, keepdims=True) + NORM_EPS) * g


def _gelu_tanh(x):
    return 0.5 * x * (1.0 + jnp.tanh(0.7978845608028654 * (x + 0.044715 * x * x * x)))


def _dot(a, b):
    return jnp.dot(a, b, preferred_element_type=F32)


def _dot_nt(a, b):
    return lax.dot_general(a, b, (((1,), (1,)), ((), ())), preferred_element_type=F32)


def _load_row_tiles(ref, rows, first=0):
    return jnp.concatenate(
        [ref[pl.ds(first * ROW_TILE + j, rows, stride=ROW_TILE), :] for j in range(ROW_TILE)], axis=1)


def _store_row_tiles(ref, val):
    for j in range(ROW_TILE):
        ref[pl.ds(j, val.shape[0], stride=ROW_TILE), :] = val[:, j * LANES:(j + 1) * LANES]


def _split3(x):
    hi = x.astype(BF16)
    r1 = x - hi.astype(F32)
    mid = r1.astype(BF16)
    lo = (r1 - mid.astype(F32)).astype(BF16)
    return hi, mid, lo


def _mixer_kernel(x_ref, gmix_ref, win_ref, bg_ref, ghead_ref, wsp_ref, bsp_ref, gzv_ref, gzo_ref, wout_ref,
                  o_ref, proj_scr, y_scr, ct_scr, m_scr, *, n_chunks, cols):
    qk0, v0, og0, gu0, gv0, gt0 = cols
    H, G, L = MLSTM_HEADS, GMLP_GROUPS, CHUNK

    @pl.when(pl.program_id(1) == 0)
    def _():
        ct_scr[...] = jnp.zeros_like(ct_scr)
        m_scr[...] = jnp.zeros_like(m_scr)

    row = lax.broadcasted_iota(jnp.int32, (L, L), 0)
    col = lax.broadcasted_iota(jnp.int32, (L, L), 1)
    causal = row >= col
    tril = jnp.where(causal, 1.0, 0.0).astype(BF16)
    triu = jnp.where(row <= col, 1.0, 0.0).astype(BF16)
    lane = lax.broadcasted_iota(jnp.int32, (L, LANES), 1)
    ones_col = jnp.where(lane == 0, 1.0, 0.0).astype(F32)

    x = x_ref[0]
    proj_scr[...] = _dot(_rms(x, gmix_ref[...]).astype(BF16), win_ref[...])

    for c in range(n_chunks):
        rows = slice(c * L, (c + 1) * L)
        pre = proj_scr[rows, gt0:gt0 + LANES] + bg_ref[...]
        capped = GATE_SOFTCAP * jnp.tanh(pre * (1.0 / GATE_SOFTCAP))
        log_sig = jnp.minimum(capped, 0.0) - jnp.log1p(jnp.exp(-jnp.abs(capped)))
        lg = jnp.where(lane < H, capped, log_sig)
        lg_t = lg.T[0:ROW_TILE, :]
        bcol = sum(_dot(tril, p) for p in _split3(lg))
        brow = sum(_dot(p, triu) for p in _split3(lg_t))

        for h in range(H):
            b_c = bcol[:, H + h:H + h + 1]
            i_c = lg[:, h:h + 1]
            b_r = brow[H + h:H + h + 1, :]
            i_r = lg_t[h:h + 1, :]
            m_prev = m_scr[h, 0:1, 0:1]
            qk = proj_scr[rows, qk0 + h * LANES:qk0 + (h + 1) * LANES]
            q = jnp.where(lane < MLSTM_DK, qk * (MLSTM_DK ** -0.5), 0.0)
            k = jnp.where(lane < MLSTM_DK, pltpu.roll(qk, MLSTM_DK, axis=1), 0.0)
            v = proj_scr[rows, v0 + h * MLSTM_DV:v0 + (h + 1) * MLSTM_DV]
            qb = q.astype(BF16)
            kb = k.astype(BF16)
            vaug = jnp.concatenate([v, ones_col], axis=1)

            d = jnp.where(causal, b_c - b_r + i_r, -jnp.inf)
            inter = b_c + m_prev
            m_t = jnp.maximum(inter, jnp.max(d, axis=1, keepdims=True))
            w_intra = jnp.exp(d - m_t)
            w_inter = jnp.exp(inter - m_t)
            s = _dot_nt(qb, kb) * w_intra
            ct = ct_scr[h]
            na = _dot(s.astype(BF16), vaug.astype(BF16)) + w_inter * _dot(qb, ct.astype(BF16))
            num = na[:, :MLSTM_DV]
            den = na[:, MLSTM_DV:MLSTM_DV + 1]
            hh = num / jnp.maximum(jnp.abs(den), jnp.exp(-m_t))

            b_last = b_c[L - 1:L, :]
            g_c = b_last - b_c + i_c
            m_new = jnp.maximum(b_last + m_prev, jnp.max(g_c, axis=0, keepdims=True))
            wk = jnp.exp(g_c - m_new)
            decay = jnp.exp(b_last + m_prev - m_new)
            ct_scr[h] = decay * ct + _dot(k.T.astype(BF16), (wk * vaug).astype(BF16))
            m_scr[h] = jnp.broadcast_to(m_new, m_scr.shape[1:])

            hn = _rms(hh, ghead_ref[:, h * MLSTM_DV:(h + 1) * MLSTM_DV])
            og = proj_scr[rows, og0 + h * MLSTM_DV:og0 + (h + 1) * MLSTM_DV]
            y_scr[rows, h * MLSTM_DV:(h + 1) * MLSTM_DV] = (jax.nn.sigmoid(og) * hn).astype(BF16)

        for g in range(G):
            sl = slice(g * GMLP_DG, (g + 1) * GMLP_DG)
            u = _gelu_tanh(proj_scr[rows, gu0 + g * GMLP_DG:gu0 + (g + 1) * GMLP_DG])
            z = _rms(_gelu_tanh(proj_scr[rows, gv0 + g * GMLP_DG:gv0 + (g + 1) * GMLP_DG]), gzv_ref[:, sl])
            wsp = jnp.where(causal, wsp_ref[g], 0.0)
            mixed = _dot(wsp.astype(BF16), z.astype(BF16)) + bsp_ref[g]
            yg = _rms(u * mixed, gzo_ref[:, sl])
            y_scr[rows, H * MLSTM_DV + g * GMLP_DG:H * MLSTM_DV + (g + 1) * GMLP_DG] = yg.astype(BF16)
    o_ref[0] = x + _dot(y_scr[...], wout_ref[...])


def _mixer(x, g_mix, w_in, b_gates, g_head, w_spatial, b_spatial, g_zv, g_zo, w_out):
    B, S, D = x.shape
    H, G, L = MLSTM_HEADS, GMLP_GROUPS, CHUNK
    qw, vw, gw = H * MLSTM_DK, H * MLSTM_DV, G * GMLP_DG
    ts = min(MIX_TOKENS, S)
    assert S % ts == 0 and ts % L == 0

    assert 2 * MLSTM_DK == LANES
    c = 0
    w_q = w_in[:, c:c + qw].reshape(D, H, MLSTM_DK); c += qw
    w_k = w_in[:, c:c + qw].reshape(D, H, MLSTM_DK); c += qw
    w_qk = jnp.concatenate([w_q, w_k], axis=2).reshape(D, H * LANES)
    w_v = w_in[:, c:c + vw]; c += vw
    w_o = w_in[:, c:c + vw]; c += vw
    w_g = jnp.pad(w_in[:, c:c + 2 * H], ((0, 0), (0, LANES - 2 * H))); c += 2 * H
    w_gu = w_in[:, c:c + gw]; c += gw
    w_gv = w_in[:, c:c + gw]; c += gw
    w_pack = jnp.concatenate([w_qk, w_v, w_o, w_gu, w_gv, w_g], axis=1).astype(BF16)
    hp = H * LANES
    cols = (0, hp, hp + vw, hp + 2 * vw, hp + 2 * vw + gw, hp + 2 * vw + 2 * gw)
    n_cols = w_pack.shape[1]
    bg = jnp.pad(b_gates, (0, LANES - 2 * H)).reshape(1, LANES)
    bsp = jnp.broadcast_to(b_spatial[:, :, None], (G, L, GMLP_DG))

    const = lambda *shape: pl.BlockSpec(shape, lambda b, j: (0,) * len(shape))
    return pl.pallas_call(
        functools.partial(_mixer_kernel, n_chunks=ts // L, cols=cols),
        out_shape=jax.ShapeDtypeStruct((B, S, D), F32),
        grid=(B, S // ts),
        in_specs=[
            pl.BlockSpec((1, ts, D), lambda b, j: (b, j, 0)),
            const(1, D), const(D, n_cols), const(1, LANES), const(1, vw),
            const(G, L, L), const(G, L, GMLP_DG), const(1, gw), const(1, gw), const(vw + gw, D),
        ],
        out_specs=pl.BlockSpec((1, ts, D), lambda b, j: (b, j, 0)),
        scratch_shapes=[
            pltpu.VMEM((ts, n_cols), F32),
            pltpu.VMEM((ts, vw + gw), BF16),
            pltpu.VMEM((H, LANES, 2 * MLSTM_DV), F32),
            pltpu.VMEM((H, ROW_TILE, LANES), F32),
        ],
        compiler_params=pltpu.CompilerParams(
            dimension_semantics=("arbitrary", "arbitrary"), vmem_limit_bytes=VMEM_LIMIT),
        name="mixer",
    )(x, g_mix.reshape(1, D), w_pack, bg, g_head.reshape(1, vw), w_spatial, bsp,
      g_zv.reshape(1, gw), g_zo.reshape(1, gw), w_out.astype(BF16))


def _kv_kernel(mem_ref, g_ref, w_ref, k_ref, v_ref):
    D = mem_ref.shape[-1]
    kv = _dot(_rms(mem_ref[0], g_ref[...]).astype(BF16), w_ref[...])
    k_ref[0] = kv[:, :D].astype(BF16)
    v_ref[0] = kv[:, D:].astype(BF16)


def _kv_proj(mem, g_mem, w_kv):
    B, M, D = mem.shape
    return pl.pallas_call(
        _kv_kernel,
        out_shape=(jax.ShapeDtypeStruct((B, M, D), BF16), jax.ShapeDtypeStruct((B, M, D), BF16)),
        grid=(B,),
        in_specs=[
            pl.BlockSpec((1, M, D), lambda b: (b, 0, 0)),
            pl.BlockSpec((1, D), lambda b: (0, 0)),
            pl.BlockSpec((D, 2 * D), lambda b: (0, 0)),
        ],
        out_specs=(pl.BlockSpec((1, M, D), lambda b: (b, 0, 0)), pl.BlockSpec((1, M, D), lambda b: (b, 0, 0))),
        compiler_params=pltpu.CompilerParams(dimension_semantics=("arbitrary",), vmem_limit_bytes=VMEM_LIMIT),
        name="kv_proj",
    )(mem, g_mem.reshape(1, D), w_kv.astype(BF16))


def _xattn_kernel(x_ref, gx_ref, wq_ref, k_ref, v_ref, wxo_ref, gmoe_ref, wr_ref, br_ref,
                  x2_ref, xn_ref, idx_ref, gate_ref, rank_ref, cnt_ref, o_scr, cnt_scr):
    ts, D = x_ref.shape[1], x_ref.shape[2]
    hd = D // XATTN_HEADS
    E = wr_ref.shape[0]

    @pl.when((pl.program_id(0) == 0) & (pl.program_id(1) == 0))
    def _():
        cnt_scr[...] = jnp.zeros_like(cnt_scr)

    x = x_ref[0]
    q = _dot(_rms(x, gx_ref[...]).astype(BF16), wq_ref[...]) * (hd ** -0.5)
    for h in range(XATTN_HEADS):
        sl = slice(h * hd, (h + 1) * hd)
        s = _dot_nt(q[:, sl].astype(BF16), k_ref[0, :, sl])
        p = jnp.exp(s - jnp.max(s, axis=1, keepdims=True))
        p = p / jnp.sum(p, axis=1, keepdims=True)
        o_scr[:, sl] = _dot(p.astype(BF16), v_ref[0, :, sl]).astype(BF16)
    x2 = x + _dot(o_scr[...], wxo_ref[...])
    x2_ref[0] = x2

    xn = _rms(x2, gmoe_ref[...])
    xh, xm, _ = _split3(xn)
    wh, wm, _ = _split3(wr_ref[...])
    logits = _dot_nt(wh, xh) + (_dot_nt(wh, xm) + _dot_nt(wm, xh)) + br_ref[...]

    e_iota = lax.broadcasted_iota(jnp.int32, (E, ts), 0)
    work = logits
    tops, idxs, hots = [], [], []
    for _ in range(TOP_K):
        m = jnp.max(work, axis=0, keepdims=True)
        i = jnp.min(jnp.where(work == m, e_iota, E), axis=0, keepdims=True)
        hot = e_iota == i
        work = jnp.where(hot, -jnp.inf, work)
        tops.append(m); idxs.append(i); hots.append(hot)
    ex = [jnp.exp(t - tops[0]) for t in tops]
    tot = ex[0] + ex[1] + ex[2] + ex[3]
    gate_ref[...] = jnp.concatenate([e / tot for e in ex], axis=0)
    idx_ref[...] = jnp.concatenate(idxs, axis=0)

    hot_all = jnp.where(hots[0] | hots[1] | hots[2] | hots[3], 1.0, 0.0)
    r = lax.broadcasted_iota(jnp.int32, (ts, ts), 0)
    c = lax.broadcasted_iota(jnp.int32, (ts, ts), 1)
    before = jnp.where(r < c, 1.0, 0.0).astype(BF16)
    base = cnt_scr[:, 0:1] + _dot(hot_all.astype(BF16), before)
    ranks = [jnp.sum(jnp.where(hot, base, 0.0), axis=0, keepdims=True) for hot in hots]
    rank_ref[...] = jnp.concatenate(ranks, axis=0).astype(jnp.int32)
    cnt_new = cnt_scr[...] + jnp.sum(hot_all, axis=1, keepdims=True)
    cnt_scr[...] = cnt_new
    cnt_ref[...] = cnt_new
    _store_row_tiles(xn_ref, xn)


def _xattn_router(x, k, v, g_xattn, w_q, w_xo, g_moe, w_router, b_router):
    B, S, D = x.shape
    M = k.shape[1]
    E = w_router.shape[1]
    ts = min(XATTN_TOKENS, S)
    assert S % ts == 0
    nj = S // ts
    T = B * S
    const = lambda *shape: pl.BlockSpec(shape, lambda b, j: (0,) * len(shape))
    tok = lambda rows: pl.BlockSpec((rows, ts), lambda b, j: (0, b * nj + j))
    return pl.pallas_call(
        _xattn_kernel,
        out_shape=(
            jax.ShapeDtypeStruct((B, S, D), F32),
            jax.ShapeDtypeStruct((T * ROW_TILE, LANES), F32),
            jax.ShapeDtypeStruct((TOP_K, T), jnp.int32),
            jax.ShapeDtypeStruct((TOP_K, T), F32),
            jax.ShapeDtypeStruct((TOP_K, T), jnp.int32),
            jax.ShapeDtypeStruct((E, LANES), F32),
        ),
        grid=(B, nj),
        in_specs=[
            pl.BlockSpec((1, ts, D), lambda b, j: (b, j, 0)),
            const(1, D), const(D, D),
            pl.BlockSpec((1, M, D), lambda b, j: (b, 0, 0)),
            pl.BlockSpec((1, M, D), lambda b, j: (b, 0, 0)),
            const(D, D), const(1, D), const(E, D), const(E, 1),
        ],
        out_specs=(
            pl.BlockSpec((1, ts, D), lambda b, j: (b, j, 0)),
            pl.BlockSpec((ts * ROW_TILE, LANES), lambda b, j: (b * nj + j, 0)),
            tok(TOP_K), tok(TOP_K), tok(TOP_K),
            const(E, LANES),
        ),
        scratch_shapes=[pltpu.VMEM((ts, D), BF16), pltpu.VMEM((E, LANES), F32)],
        compiler_params=pltpu.CompilerParams(
            dimension_semantics=("arbitrary", "arbitrary"), vmem_limit_bytes=VMEM_LIMIT),
        name="xattn_router",
    )(x, g_xattn.reshape(1, D), w_q.astype(BF16), k, v, w_xo.astype(BF16), g_moe.reshape(1, D),
      w_router.T, b_router.reshape(E, 1))


def _dispatch_kernel(dest_ref, cnt_ref, pend_ref, xn_ref, xs_ref, zero_scr, sem, zsem, *, n_experts):
    tt = xn_ref.shape[0] // ROW_TILE
    R = zero_scr.shape[0] // ROW_TILE

    @pl.when(pl.program_id(0) == 0)
    def _():
        zero_scr[...] = jnp.zeros_like(zero_scr)

        def fill_copy(e):
            start = pl.multiple_of((pend_ref[e] - R) * ROW_TILE, R * ROW_TILE)
            return pltpu.make_async_copy(zero_scr, xs_ref.at[pl.ds(start, R * ROW_TILE)], zsem)

        def start_fill(e, carry):
            @pl.when(cnt_ref[e] > 0)
            def _():
                fill_copy(e).start()
            return carry

        def wait_fill(e, carry):
            @pl.when(cnt_ref[e] > 0)
            def _():
                fill_copy(e).wait()
            return carry

        lax.fori_loop(0, n_experts, start_fill, 0)
        lax.fori_loop(0, n_experts, wait_fill, 0)

    def issue(grp, carry):
        t0 = pl.multiple_of(grp * ROW_UNROLL, ROW_UNROLL)
        for kk in range(TOP_K):
            for u in range(ROW_UNROLL):
                dst = pl.multiple_of(dest_ref[0, 0, kk * tt + t0 + u] * ROW_TILE, ROW_TILE)
                row = pl.multiple_of((t0 + u) * ROW_TILE, ROW_TILE)
                pltpu.make_async_copy(xn_ref.at[pl.ds(row, ROW_TILE)], xs_ref.at[pl.ds(dst, ROW_TILE)],
                                      sem).start(priority=u % 2)
        return carry

    lax.fori_loop(0, tt // ROW_UNROLL, issue, 0)
    for kk in range(TOP_K):
        pltpu.make_async_copy(xn_ref, xs_ref.at[pl.ds(0, tt * ROW_TILE)], sem).wait()


def _dispatch(xn, dest, counts, pad_end, n_slots):
    T = dest.shape[1]
    tt = min(DISPATCH_TOKENS, T)
    assert T % tt == 0
    E = counts.shape[0]
    dest = dest.reshape(TOP_K, T // tt, tt).transpose(1, 0, 2).reshape(T // tt, 1, TOP_K * tt)
    return pl.pallas_call(
        functools.partial(_dispatch_kernel, n_experts=E),
        out_shape=jax.ShapeDtypeStruct((n_slots * ROW_TILE, LANES), F32),
        grid=(T // tt,),
        in_specs=[
            pl.BlockSpec((1, 1, TOP_K * tt), lambda i: (i, 0, 0), memory_space=pltpu.SMEM),
            pl.BlockSpec(memory_space=pltpu.SMEM),
            pl.BlockSpec(memory_space=pltpu.SMEM),
            pl.BlockSpec((tt * ROW_TILE, LANES), lambda i: (i, 0)),
        ],
        out_specs=pl.BlockSpec(memory_space=pl.ANY),
        scratch_shapes=[pltpu.VMEM((MOE_ROWS * ROW_TILE, LANES), F32), pltpu.SemaphoreType.DMA,
                        pltpu.SemaphoreType.DMA],
        compiler_params=pltpu.CompilerParams(
            dimension_semantics=("arbitrary",), vmem_limit_bytes=VMEM_LIMIT, has_side_effects=True),
        name="dispatch",
    )(dest, counts, pad_end, xn)


def _expert_kernel(be_ref, nu_ref, reg_ref, nxt_ref, xs_ref, wg_ref, bg_ref, wu_ref, bu_ref, wd_ref, bd_ref, y_ref,
                   wf_scr, wg_scr, wu_scr, wd_scr, sem):
    j = pl.program_id(0)
    active = j < nu_ref[0]
    w_hbm = (wg_ref, wu_ref, wd_ref)

    def fetch(e, slot):
        return [pltpu.make_async_copy(w_hbm[i].at[e], wf_scr.at[slot, i], sem.at[slot, i]) for i in range(3)]

    @pl.when(active & ((j == 0) | (be_ref[j] != be_ref[jnp.maximum(j - 1, 0)])))
    def _():
        slot = reg_ref[j] % 2

        @pl.when(j == 0)
        def _():
            for cp in fetch(be_ref[0], 0):
                cp.start()

        for cp in fetch(be_ref[j], slot):
            cp.wait()
        wg_scr[...] = wf_scr[slot, 0].astype(BF16)
        wu_scr[...] = wf_scr[slot, 1].astype(BF16)
        wd_scr[...] = wf_scr[slot, 2].astype(BF16)

        @pl.when(nxt_ref[j] >= 0)
        def _():
            for cp in fetch(nxt_ref[j], 1 - slot):
                cp.start()

    @pl.when(active)
    def _():
        xb = _load_row_tiles(xs_ref, xs_ref.shape[0] // ROW_TILE).astype(BF16)
        g = jnp.minimum(_dot(xb, wg_scr[...]) + bg_ref[0], SWIGLU_LIMIT)
        u = jnp.clip(_dot(xb, wu_scr[...]) + bu_ref[0], -SWIGLU_LIMIT, SWIGLU_LIMIT)
        hdn = (u + 1.0) * (g * jax.nn.sigmoid(SWIGLU_ALPHA * g))
        _store_row_tiles(y_ref, _dot(hdn.astype(BF16), wd_scr[...]) + bd_ref[0])


def _experts(xs, block_e, n_used, region, next_e, w_gate, b_gate, w_up, b_up, w_down, b_down):
    E, D, F = w_gate.shape
    assert D == ROW_TILE * LANES and F == D, "the two weight staging slots hold (D, F) and (F, D) alike"
    n_slots = xs.shape[0] // ROW_TILE
    R = MOE_ROWS
    n_blocks = n_slots // R
    blk = lambda j, be, nu, rg, nx: (jnp.minimum(j, nu[0] - 1), 0)
    wsel = lambda j, be, nu, rg, nx: (be[jnp.minimum(j, nu[0] - 1)], 0, 0)
    hbm = pl.BlockSpec(memory_space=pl.ANY)
    grid_spec = pltpu.PrefetchScalarGridSpec(
        num_scalar_prefetch=4,
        grid=(n_blocks,),
        in_specs=[
            pl.BlockSpec((R * ROW_TILE, LANES), blk),
            hbm, pl.BlockSpec((1, 1, F), wsel),
            hbm, pl.BlockSpec((1, 1, F), wsel),
            hbm, pl.BlockSpec((1, 1, D), wsel),
        ],
        out_specs=pl.BlockSpec((R * ROW_TILE, LANES), blk),
        scratch_shapes=[pltpu.VMEM((2, 3, D, F), F32), pltpu.VMEM((D, F), BF16), pltpu.VMEM((D, F), BF16),
                        pltpu.VMEM((F, D), BF16), pltpu.SemaphoreType.DMA((2, 3))],
    )
    return pl.pallas_call(
        _expert_kernel,
        out_shape=jax.ShapeDtypeStruct((n_slots * ROW_TILE, LANES), F32),
        grid_spec=grid_spec,
        compiler_params=pltpu.CompilerParams(dimension_semantics=("arbitrary",), vmem_limit_bytes=VMEM_LIMIT),
        name="experts",
    )(block_e, n_used, region, next_e, xs, w_gate, b_gate.reshape(E, 1, F), w_up, b_up.reshape(E, 1, F),
      w_down, b_down.reshape(E, 1, D))


def _combine_kernel(dcur_ref, dnext_ref, x_ref, gate_ref, gfin_ref, ys_ref, o_ref, buf, sem):
    tc = x_ref.shape[0]
    i = pl.program_id(0)
    slot = i % 2

    def issue_all(d_ref, into):
        def issue(grp, carry):
            t0 = pl.multiple_of(grp * ROW_UNROLL, ROW_UNROLL)
            for kk in range(TOP_K):
                for u in range(ROW_UNROLL):
                    src = pl.multiple_of(d_ref[kk, t0 + u] * ROW_TILE, ROW_TILE)
                    row = pl.multiple_of((t0 + u) * ROW_TILE, ROW_TILE)
                    pltpu.make_async_copy(ys_ref.at[pl.ds(src, ROW_TILE)], buf.at[into, kk, pl.ds(row, ROW_TILE)],
                                          sem.at[into]).start(priority=u % 2)
            return carry

        lax.fori_loop(0, tc // ROW_UNROLL, issue, 0)

    def wait_slot(s):
        for kk in range(TOP_K):
            pltpu.make_async_copy(ys_ref.at[pl.ds(0, tc * ROW_TILE)], buf.at[s, kk], sem.at[s]).wait()

    @pl.when(i == 0)
    def _():
        issue_all(dcur_ref, 0)

    wait_slot(slot)
    other = 1 - slot
    for t in range(tc):
        for kk in range(TOP_K):
            src = pl.multiple_of(dnext_ref[kk, t] * ROW_TILE, ROW_TILE)
            pltpu.make_async_copy(ys_ref.at[pl.ds(src, ROW_TILE)], buf.at[other, kk, pl.ds(t * ROW_TILE, ROW_TILE)],
                                  sem.at[other]).start(priority=t % 2)
    for r0 in range(0, tc, COMBINE_ROWS):
        rows = slice(r0, r0 + COMBINE_ROWS)
        acc = x_ref[rows, :]
        for kk in range(TOP_K):
            acc = acc + gate_ref[rows, kk:kk + 1] * _load_row_tiles(buf.at[slot, kk], COMBINE_ROWS, r0)
        o_ref[rows, :] = _rms(acc, gfin_ref[...])

    @pl.when(i == pl.num_programs(0) - 1)
    def _():
        wait_slot(other)


def _combine(x2, gate_t, dest, ys, g_final):
    T, D = x2.shape
    tc = min(COMBINE_TOKENS, T)
    assert T % tc == 0
    nb = T // tc
    return pl.pallas_call(
        _combine_kernel,
        out_shape=jax.ShapeDtypeStruct((T, D), F32),
        grid=(nb,),
        in_specs=[
            pl.BlockSpec((TOP_K, tc), lambda i: (0, i), memory_space=pltpu.SMEM),
            pl.BlockSpec((TOP_K, tc), lambda i: (0, jnp.minimum(i + 1, nb - 1)), memory_space=pltpu.SMEM),
            pl.BlockSpec((tc, D), lambda i: (i, 0)),
            pl.BlockSpec((tc, TOP_K), lambda i: (i, 0)),
            pl.BlockSpec((1, D), lambda i: (0, 0)),
            pl.BlockSpec(memory_space=pl.ANY),
        ],
        out_specs=pl.BlockSpec((tc, D), lambda i: (i, 0)),
        scratch_shapes=[pltpu.VMEM((2, TOP_K, tc * ROW_TILE, LANES), F32), pltpu.SemaphoreType.DMA((2,))],
        compiler_params=pltpu.CompilerParams(dimension_semantics=("arbitrary",), vmem_limit_bytes=VMEM_LIMIT),
        name="combine",
    )(dest, dest, x2, gate_t, g_final.reshape(1, D), ys)


def _moe(x2, xn, idx, gate, rank, cnt, w_gate, b_gate, w_up, b_up, w_down, b_down, g_final):
    T = x2.shape[0]
    E = w_gate.shape[0]
    R = MOE_ROWS
    n_blocks = -(-T * TOP_K // R) + E
    n_slots = n_blocks * R
    counts = cnt[:, 0].astype(jnp.int32)
    padded = (counts + R - 1) // R * R
    pad_end = jnp.cumsum(padded)
    pad_start = pad_end - padded
    dest = rank
    for e in range(E):
        dest = dest + jnp.where(idx == e, pad_start[e], 0)
    n_used = (pad_end[-1:] // R).astype(jnp.int32)
    block_e = jnp.minimum(jnp.sum(jnp.arange(n_blocks)[:, None] * R >= pad_end[None, :], axis=1), E - 1)
    block_e = block_e.astype(jnp.int32)
    blocks = jnp.arange(n_blocks)
    first = (blocks < n_used[0]) & ((blocks == 0) | (block_e != jnp.roll(block_e, 1)))
    region = (jnp.cumsum(first) - 1).astype(jnp.int32)
    later = jnp.where((counts[None, :] > 0) & (jnp.arange(E)[None, :] > block_e[:, None]), jnp.arange(E)[None, :], E)
    next_e = jnp.min(later, axis=1)
    next_e = jnp.where(next_e == E, -1, next_e).astype(jnp.int32)

    xs = _dispatch(xn, dest, counts, pad_end.astype(jnp.int32), n_slots)
    ys = _experts(xs, block_e, n_used, region, next_e, w_gate, b_gate, w_up, b_up, w_down, b_down)
    return _combine(x2, gate.T, dest, ys, g_final)


def kernel(x, mem, g_mix, w_in, b_gates, g_mlstm_head, w_spatial, b_spatial, g_gmlp_v, g_gmlp_out, w_out,
           g_xattn, g_mem, w_q, w_kv, w_xo, g_moe, w_router, b_router, w_gate, b_gate, w_up, b_up,
           w_down, b_down, g_final):
    B, S, D = x.shape
    assert g_mix.shape[0] == 1, "the combine kernel fuses the closing norm, so exactly one layer is supported"
    x1 = _mixer(x, g_mix[0], w_in[0], b_gates[0], g_mlstm_head[0], w_spatial[0], b_spatial[0],
                g_gmlp_v[0], g_gmlp_out[0], w_out[0])
    k, v = _kv_proj(mem, g_mem[0], w_kv[0])
    x2, xn, idx, gate, rank, cnt = _xattn_router(x1, k, v, g_xattn[0], w_q[0], w_xo[0], g_moe[0],
                                                 w_router[0], b_router[0])
    out = _moe(x2.reshape(B * S, D), xn, idx, gate, rank, cnt, w_gate[0], b_gate[0], w_up[0], b_up[0],
               w_down[0], b_down[0], g_final)
    return out.reshape(B, S, D)
```

```python
import functools

import jax
import jax.numpy as jnp
from jax import lax
from jax.experimental import pallas as pl
from jax.experimental.pallas import tpu as pltpu

F32 = jnp.float32
BF16 = jnp.bfloat16

MLSTM_HEADS = 4
MLSTM_DK = 64
MLSTM_DV = 128
CHUNK = 128
GMLP_GROUPS = 4
GMLP_DG = 128
XATTN_HEADS = 4
TOP_K = 4
GATE_SOFTCAP = 15.0
SWIGLU_LIMIT = 7.0
SWIGLU_ALPHA = 1.702
NORM_EPS = 1e-6

LANES = 128
ROW_TILE = 8

MIX_TOKENS = 512
XATTN_TOKENS = 512
MOE_ROWS = 512
DISPATCH_TOKENS = 2048
COMBINE_TOKENS = 256
COMBINE_ROWS = 32
ROW_UNROLL = 16
VMEM_LIMIT = 52 * 1024 * 1024


def _rms(x, g):
    return x * lax.rsqrt(jnp.mean(x * x, axis=-1, keepdims=True) + NORM_EPS) * g


def _gelu_tanh(x):
    return 0.5 * x * (1.0 + jnp.tanh(0.7978845608028654 * (x + 0.044715 * x * x * x)))


def _dot(a, b):
    return jnp.dot(a, b, preferred_element_type=F32)


def _dot_nt(a, b):
    return lax.dot_general(a, b, (((1,), (1,)), ((), ())), preferred_element_type=F32)


def _load_row_tiles(ref, rows, first=0):
    return jnp.concatenate(
        [ref[pl.ds(first * ROW_TILE + j, rows, stride=ROW_TILE), :] for j in range(ROW_TILE)], axis=1)


def _store_row_tiles(ref, val):
    for j in range(ROW_TILE):
        ref[pl.ds(j, val.shape[0], stride=ROW_TILE), :] = val[:, j * LANES:(j + 1) * LANES]


def _split3(x):
    hi = x.astype(BF16)
    r1 = x - hi.astype(F32)
    mid = r1.astype(BF16)
    lo = (r1 - mid.astype(F32)).astype(BF16)
    return hi, mid, lo


def _mixer_kernel(x_ref, gmix_ref, win_ref, bg_ref, ghead_ref, wsp_ref, bsp_ref, gzv_ref, gzo_ref, wout_ref,
                  o_ref, proj_scr, y_scr, ct_scr, m_scr, *, n_chunks, cols):
    qk0, v0, og0, gu0, gv0, gt0 = cols
    H, G, L = MLSTM_HEADS, GMLP_GROUPS, CHUNK

    @pl.when(pl.program_id(1) == 0)
    def _():
        ct_scr[...] = jnp.zeros_like(ct_scr)
        m_scr[...] = jnp.zeros_like(m_scr)

    row = lax.broadcasted_iota(jnp.int32, (L, L), 0)
    col = lax.broadcasted_iota(jnp.int32, (L, L), 1)
    causal = row >= col
    tril = jnp.where(causal, 1.0, 0.0).astype(BF16)
    triu = jnp.where(row <= col, 1.0, 0.0).astype(BF16)
    lane = lax.broadcasted_iota(jnp.int32, (L, LANES), 1)
    ones_col = jnp.where(lane == 0, 1.0, 0.0).astype(F32)

    x = x_ref[0]
    proj_scr[...] = _dot(_rms(x, gmix_ref[...]).astype(BF16), win_ref[...])

    for c in range(n_chunks):
        rows = slice(c * L, (c + 1) * L)
        pre = proj_scr[rows, gt0:gt0 + LANES] + bg_ref[...]
        capped = GATE_SOFTCAP * jnp.tanh(pre * (1.0 / GATE_SOFTCAP))
        log_sig = jnp.minimum(capped, 0.0) - jnp.log1p(jnp.exp(-jnp.abs(capped)))
        lg = jnp.where(lane < H, capped, log_sig)
        lg_t = lg.T[0:ROW_TILE, :]
        bcol = sum(_dot(tril, p) for p in _split3(lg))
        brow = sum(_dot(p, triu) for p in _split3(lg_t))

        for h in range(H):
            b_c = bcol[:, H + h:H + h + 1]
            i_c = lg[:, h:h + 1]
            b_r = brow[H + h:H + h + 1, :]
            i_r = lg_t[h:h + 1, :]
            m_prev = m_scr[h, 0:1, 0:1]
            qk = proj_scr[rows, qk0 + h * LANES:qk0 + (h + 1) * LANES]
            q = jnp.where(lane < MLSTM_DK, qk * (MLSTM_DK ** -0.5), 0.0)
            k = jnp.where(lane < MLSTM_DK, pltpu.roll(qk, MLSTM_DK, axis=1), 0.0)
            v = proj_scr[rows, v0 + h * MLSTM_DV:v0 + (h + 1) * MLSTM_DV]
            qb = q.astype(BF16)
            kb = k.astype(BF16)
            vaug = jnp.concatenate([v, ones_col], axis=1)

            d = jnp.where(causal, b_c - b_r + i_r, -jnp.inf)
            inter = b_c + m_prev
            m_t = jnp.maximum(inter, jnp.max(d, axis=1, keepdims=True))
            w_intra = jnp.exp(d - m_t)
            w_inter = jnp.exp(inter - m_t)
            s = _dot_nt(qb, kb) * w_intra
            ct = ct_scr[h]
            na = _dot(s.astype(BF16), vaug.astype(BF16)) + w_inter * _dot(qb, ct.astype(BF16))
            num = na[:, :MLSTM_DV]
            den = na[:, MLSTM_DV:MLSTM_DV + 1]
            hh = num / jnp.maximum(jnp.abs(den), jnp.exp(-m_t))

            b_last = b_c[L - 1:L, :]
            g_c = b_last - b_c + i_c
            m_new = jnp.maximum(b_last + m_prev, jnp.max(g_c, axis=0, keepdims=True))
            wk = jnp.exp(g_c - m_new)
            decay = jnp.exp(b_last + m_prev - m_new)
            ct_scr[h] = decay * ct + _dot(k.T.astype(BF16), (wk * vaug).astype(BF16))
            m_scr[h] = jnp.broadcast_to(m_new, m_scr.shape[1:])

            hn = _rms(hh, ghead_ref[:, h * MLSTM_DV:(h + 1) * MLSTM_DV])
            og = proj_scr[rows, og0 + h * MLSTM_DV:og0 + (h + 1) * MLSTM_DV]
            y_scr[rows, h * MLSTM_DV:(h + 1) * MLSTM_DV] = (jax.nn.sigmoid(og) * hn).astype(BF16)

        for g in range(G):
            sl = slice(g * GMLP_DG, (g + 1) * GMLP_DG)
            u = _gelu_tanh(proj_scr[rows, gu0 + g * GMLP_DG:gu0 + (g + 1) * GMLP_DG])
            z = _rms(_gelu_tanh(proj_scr[rows, gv0 + g * GMLP_DG:gv0 + (g + 1) * GMLP_DG]), gzv_ref[:, sl])
            wsp = jnp.where(causal, wsp_ref[g], 0.0)
            mixed = _dot(wsp.astype(BF16), z.astype(BF16)) + bsp_ref[g]
            yg = _rms(u * mixed, gzo_ref[:, sl])
            y_scr[rows, H * MLSTM_DV + g * GMLP_DG:H * MLSTM_DV + (g + 1) * GMLP_DG] = yg.astype(BF16)
    o_ref[0] = x + _dot(y_scr[...], wout_ref[...])


def _mixer(x, g_mix, w_in, b_gates, g_head, w_spatial, b_spatial, g_zv, g_zo, w_out):
    B, S, D = x.shape
    H, G, L = MLSTM_HEADS, GMLP_GROUPS, CHUNK
    qw, vw, gw = H * MLSTM_DK, H * MLSTM_DV, G * GMLP_DG
    ts = min(MIX_TOKENS, S)
    assert S % ts == 0 and ts % L == 0

    assert 2 * MLSTM_DK == LANES
    c = 0
    w_q = w_in[:, c:c + qw].reshape(D, H, MLSTM_DK); c += qw
    w_k = w_in[:, c:c + qw].reshape(D, H, MLSTM_DK); c += qw
    w_qk = jnp.concatenate([w_q, w_k], axis=2).reshape(D, H * LANES)
    w_v = w_in[:, c:c + vw]; c += vw
    w_o = w_in[:, c:c + vw]; c += vw
    w_g = jnp.pad(w_in[:, c:c + 2 * H], ((0, 0), (0, LANES - 2 * H))); c += 2 * H
    w_gu = w_in[:, c:c + gw]; c += gw
    w_gv = w_in[:, c:c + gw]; c += gw
    w_pack = jnp.concatenate([w_qk, w_v, w_o, w_gu, w_gv, w_g], axis=1).astype(BF16)
    hp = H * LANES
    cols = (0, hp, hp + vw, hp + 2 * vw, hp + 2 * vw + gw, hp + 2 * vw + 2 * gw)
    n_cols = w_pack.shape[1]
    bg = jnp.pad(b_gates, (0, LANES - 2 * H)).reshape(1, LANES)
    bsp = jnp.broadcast_to(b_spatial[:, :, None], (G, L, GMLP_DG))

    const = lambda *shape: pl.BlockSpec(shape, lambda b, j: (0,) * len(shape))
    return pl.pallas_call(
        functools.partial(_mixer_kernel, n_chunks=ts // L, cols=cols),
        out_shape=jax.ShapeDtypeStruct((B, S, D), F32),
        grid=(B, S // ts),
        in_specs=[
            pl.BlockSpec((1, ts, D), lambda b, j: (b, j, 0)),
            const(1, D), const(D, n_cols), const(1, LANES), const(1, vw),
            const(G, L, L), const(G, L, GMLP_DG), const(1, gw), const(1, gw), const(vw + gw, D),
        ],
        out_specs=pl.BlockSpec((1, ts, D), lambda b, j: (b, j, 0)),
        scratch_shapes=[
            pltpu.VMEM((ts, n_cols), F32),
            pltpu.VMEM((ts, vw + gw), BF16),
            pltpu.VMEM((H, LANES, 2 * MLSTM_DV), F32),
            pltpu.VMEM((H, ROW_TILE, LANES), F32),
        ],
        compiler_params=pltpu.CompilerParams(
            dimension_semantics=("arbitrary", "arbitrary"), vmem_limit_bytes=VMEM_LIMIT),
        name="mixer",
    )(x, g_mix.reshape(1, D), w_pack, bg, g_head.reshape(1, vw), w_spatial, bsp,
      g_zv.reshape(1, gw), g_zo.reshape(1, gw), w_out.astype(BF16))


def _kv_kernel(mem_ref, g_ref, w_ref, k_ref, v_ref):
    D = mem_ref.shape[-1]
    kv = _dot(_rms(mem_ref[0], g_ref[...]).astype(BF16), w_ref[...])
    k_ref[0] = kv[:, :D].astype(BF16)
    v_ref[0] = kv[:, D:].astype(BF16)


def _kv_proj(mem, g_mem, w_kv):
    B, M, D = mem.shape
    return pl.pallas_call(
        _kv_kernel,
        out_shape=(jax.ShapeDtypeStruct((B, M, D), BF16), jax.ShapeDtypeStruct((B, M, D), BF16)),
        grid=(B,),
        in_specs=[
            pl.BlockSpec((1, M, D), lambda b: (b, 0, 0)),
            pl.BlockSpec((1, D), lambda b: (0, 0)),
            pl.BlockSpec((D, 2 * D), lambda b: (0, 0)),
        ],
        out_specs=(pl.BlockSpec((1, M, D), lambda b: (b, 0, 0)), pl.BlockSpec((1, M, D), lambda b: (b, 0, 0))),
        compiler_params=pltpu.CompilerParams(dimension_semantics=("arbitrary",), vmem_limit_bytes=VMEM_LIMIT),
        name="kv_proj",
    )(mem, g_mem.reshape(1, D), w_kv.astype(BF16))


def _xattn_kernel(x_ref, gx_ref, wq_ref, k_ref, v_ref, wxo_ref, gmoe_ref, wr_ref, br_ref,
                  x2_ref, xn_ref, idx_ref, gate_ref, rank_ref, cnt_ref, o_scr, cnt_scr):
    ts, D = x_ref.shape[1], x_ref.shape[2]
    hd = D // XATTN_HEADS
    E = wr_ref.shape[0]

    @pl.when((pl.program_id(0) == 0) & (pl.program_id(1) == 0))
    def _():
        cnt_scr[...] = jnp.zeros_like(cnt_scr)

    x = x_ref[0]
    q = _dot(_rms(x, gx_ref[...]).astype(BF16), wq_ref[...]) * (hd ** -0.5)
    for h in range(XATTN_HEADS):
        sl = slice(h * hd, (h + 1) * hd)
        s = _dot_nt(q[:, sl].astype(BF16), k_ref[0, :, sl])
        p = jnp.exp(s - jnp.max(s, axis=1, keepdims=True))
        p = p / jnp.sum(p, axis=1, keepdims=True)
        o_scr[:, sl] = _dot(p.astype(BF16), v_ref[0, :, sl]).astype(BF16)
    x2 = x + _dot(o_scr[...], wxo_ref[...])
    x2_ref[0] = x2

    xn = _rms(x2, gmoe_ref[...])
    xh, xm, _ = _split3(xn)
    wh, wm, _ = _split3(wr_ref[...])
    logits = _dot_nt(wh, xh) + (_dot_nt(wh, xm) + _dot_nt(wm, xh)) + br_ref[...]

    e_iota = lax.broadcasted_iota(jnp.int32, (E, ts), 0)
    work = logits
    tops, idxs, hots = [], [], []
    for _ in range(TOP_K):
        m = jnp.max(work, axis=0, keepdims=True)
        i = jnp.min(jnp.where(work == m, e_iota, E), axis=0, keepdims=True)
        hot = e_iota == i
        work = jnp.where(hot, -jnp.inf, work)
        tops.append(m); idxs.append(i); hots.append(hot)
    ex = [jnp.exp(t - tops[0]) for t in tops]
    tot = ex[0] + ex[1] + ex[2] + ex[3]
    gate_ref[...] = jnp.concatenate([e / tot for e in ex], axis=0)
    idx_ref[...] = jnp.concatenate(idxs, axis=0)

    hot_all = jnp.where(hots[0] | hots[1] | hots[2] | hots[3], 1.0, 0.0)
    r = lax.broadcasted_iota(jnp.int32, (ts, ts), 0)
    c = lax.broadcasted_iota(jnp.int32, (ts, ts), 1)
    before = jnp.where(r < c, 1.0, 0.0).astype(BF16)
    base = cnt_scr[:, 0:1] + _dot(hot_all.astype(BF16), before)
    ranks = [jnp.sum(jnp.where(hot, base, 0.0), axis=0, keepdims=True) for hot in hots]
    rank_ref[...] = jnp.concatenate(ranks, axis=0).astype(jnp.int32)
    cnt_new = cnt_scr[...] + jnp.sum(hot_all, axis=1, keepdims=True)
    cnt_scr[...] = cnt_new
    cnt_ref[...] = cnt_new
    _store_row_tiles(xn_ref, xn)


def _xattn_router(x, k, v, g_xattn, w_q, w_xo, g_moe, w_router, b_router):
    B, S, D = x.shape
    M = k.shape[1]
    E = w_router.shape[1]
    ts = min(XATTN_TOKENS, S)
    assert S % ts == 0
    nj = S // ts
    T = B * S
    const = lambda *shape: pl.BlockSpec(shape, lambda b, j: (0,) * len(shape))
    tok = lambda rows: pl.BlockSpec((rows, ts), lambda b, j: (0, b * nj + j))
    return pl.pallas_call(
        _xattn_kernel,
        out_shape=(
            jax.ShapeDtypeStruct((B, S, D), F32),
            jax.ShapeDtypeStruct((T * ROW_TILE, LANES), F32),
            jax.ShapeDtypeStruct((TOP_K, T), jnp.int32),
            jax.ShapeDtypeStruct((TOP_K, T), F32),
            jax.ShapeDtypeStruct((TOP_K, T), jnp.int32),
            jax.ShapeDtypeStruct((E, LANES), F32),
        ),
        grid=(B, nj),
        in_specs=[
            pl.BlockSpec((1, ts, D), lambda b, j: (b, j, 0)),
            const(1, D), const(D, D),
            pl.BlockSpec((1, M, D), lambda b, j: (b, 0, 0)),
            pl.BlockSpec((1, M, D), lambda b, j: (b, 0, 0)),
            const(D, D), const(1, D), const(E, D), const(E, 1),
        ],
        out_specs=(
            pl.BlockSpec((1, ts, D), lambda b, j: (b, j, 0)),
            pl.BlockSpec((ts * ROW_TILE, LANES), lambda b, j: (b * nj + j, 0)),
            tok(TOP_K), tok(TOP_K), tok(TOP_K),
            const(E, LANES),
        ),
        scratch_shapes=[pltpu.VMEM((ts, D), BF16), pltpu.VMEM((E, LANES), F32)],
        compiler_params=pltpu.CompilerParams(
            dimension_semantics=("arbitrary", "arbitrary"), vmem_limit_bytes=VMEM_LIMIT),
        name="xattn_router",
    )(x, g_xattn.reshape(1, D), w_q.astype(BF16), k, v, w_xo.astype(BF16), g_moe.reshape(1, D),
      w_router.T, b_router.reshape(E, 1))


def _dispatch_kernel(dest_ref, cnt_ref, pend_ref, xn_ref, xs_ref, zero_scr, sem, zsem, *, n_experts):
    tt = xn_ref.shape[0] // ROW_TILE
    R = zero_scr.shape[0] // ROW_TILE

    @pl.when(pl.program_id(0) == 0)
    def _():
        zero_scr[...] = jnp.zeros_like(zero_scr)

        def fill_copy(e):
            start = pl.multiple_of((pend_ref[e] - R) * ROW_TILE, R * ROW_TILE)
            return pltpu.make_async_copy(zero_scr, xs_ref.at[pl.ds(start, R * ROW_TILE)], zsem)

        def start_fill(e, carry):
            @pl.when(cnt_ref[e] > 0)
            def _():
                fill_copy(e).start()
            return carry

        def wait_fill(e, carry):
            @pl.when(cnt_ref[e] > 0)
            def _():
                fill_copy(e).wait()
            return carry

        lax.fori_loop(0, n_experts, start_fill, 0)
        lax.fori_loop(0, n_experts, wait_fill, 0)

    def issue(grp, carry):
        t0 = pl.multiple_of(grp * ROW_UNROLL, ROW_UNROLL)
        for kk in range(TOP_K):
            for u in range(ROW_UNROLL):
                dst = pl.multiple_of(dest_ref[0, 0, kk * tt + t0 + u] * ROW_TILE, ROW_TILE)
                row = pl.multiple_of((t0 + u) * ROW_TILE, ROW_TILE)
                pltpu.make_async_copy(xn_ref.at[pl.ds(row, ROW_TILE)], xs_ref.at[pl.ds(dst, ROW_TILE)],
                                      sem).start(priority=u % 2)
        return carry

    lax.fori_loop(0, tt // ROW_UNROLL, issue, 0)
    for kk in range(TOP_K):
        pltpu.make_async_copy(xn_ref, xs_ref.at[pl.ds(0, tt * ROW_TILE)], sem).wait()


def _dispatch(xn, dest, counts, pad_end, n_slots):
    T = dest.shape[1]
    tt = min(DISPATCH_TOKENS, T)
    assert T % tt == 0
    E = counts.shape[0]
    dest = dest.reshape(TOP_K, T // tt, tt).transpose(1, 0, 2).reshape(T // tt, 1, TOP_K * tt)
    return pl.pallas_call(
        functools.partial(_dispatch_kernel, n_experts=E),
        out_shape=jax.ShapeDtypeStruct((n_slots * ROW_TILE, LANES), F32),
        grid=(T // tt,),
        in_specs=[
            pl.BlockSpec((1, 1, TOP_K * tt), lambda i: (i, 0, 0), memory_space=pltpu.SMEM),
            pl.BlockSpec(memory_space=pltpu.SMEM),
            pl.BlockSpec(memory_space=pltpu.SMEM),
            pl.BlockSpec((tt * ROW_TILE, LANES), lambda i: (i, 0)),
        ],
        out_specs=pl.BlockSpec(memory_space=pl.ANY),
        scratch_shapes=[pltpu.VMEM((MOE_ROWS * ROW_TILE, LANES), F32), pltpu.SemaphoreType.DMA,
                        pltpu.SemaphoreType.DMA],
        compiler_params=pltpu.CompilerParams(
            dimension_semantics=("arbitrary",), vmem_limit_bytes=VMEM_LIMIT, has_side_effects=True),
        name="dispatch",
    )(dest, counts, pad_end, xn)


def _expert_kernel(be_ref, nu_ref, reg_ref, nxt_ref, xs_ref, wg_ref, bg_ref, wu_ref, bu_ref, wd_ref, bd_ref, y_ref,
                   wf_scr, wg_scr, wu_scr, wd_scr, sem):
    j = pl.program_id(0)
    active = j < nu_ref[0]
    w_hbm = (wg_ref, wu_ref, wd_ref)

    def fetch(e, slot):
        return [pltpu.make_async_copy(w_hbm[i].at[e], wf_scr.at[slot, i], sem.at[slot, i]) for i in range(3)]

    @pl.when(active & ((j == 0) | (be_ref[j] != be_ref[jnp.maximum(j - 1, 0)])))
    def _():
        slot = reg_ref[j] % 2

        @pl.when(j == 0)
        def _():
            for cp in fetch(be_ref[0], 0):
                cp.start()

        for cp in fetch(be_ref[j], slot):
            cp.wait()
        wg_scr[...] = wf_scr[slot, 0].astype(BF16)
        wu_scr[...] = wf_scr[slot, 1].astype(BF16)
        wd_scr[...] = wf_scr[slot, 2].astype(BF16)

        @pl.when(nxt_ref[j] >= 0)
        def _():
            for cp in fetch(nxt_ref[j], 1 - slot):
                cp.start()

    @pl.when(active)
    def _():
        xb = _load_row_tiles(xs_ref, xs_ref.shape[0] // ROW_TILE).astype(BF16)
        g = jnp.minimum(_dot(xb, wg_scr[...]) + bg_ref[0], SWIGLU_LIMIT)
        u = jnp.clip(_dot(xb, wu_scr[...]) + bu_ref[0], -SWIGLU_LIMIT, SWIGLU_LIMIT)
        hdn = (u + 1.0) * (g * jax.nn.sigmoid(SWIGLU_ALPHA * g))
        _store_row_tiles(y_ref, _dot(hdn.astype(BF16), wd_scr[...]) + bd_ref[0])


def _experts(xs, block_e, n_used, region, next_e, w_gate, b_gate, w_up, b_up, w_down, b_down):
    E, D, F = w_gate.shape
    assert D == ROW_TILE * LANES and F == D, "the two weight staging slots hold (D, F) and (F, D) alike"
    n_slots = xs.shape[0] // ROW_TILE
    R = MOE_ROWS
    n_blocks = n_slots // R
    blk = lambda j, be, nu, rg, nx: (jnp.minimum(j, nu[0] - 1), 0)
    wsel = lambda j, be, nu, rg, nx: (be[jnp.minimum(j, nu[0] - 1)], 0, 0)
    hbm = pl.BlockSpec(memory_space=pl.ANY)
    grid_spec = pltpu.PrefetchScalarGridSpec(
        num_scalar_prefetch=4,
        grid=(n_blocks,),
        in_specs=[
            pl.BlockSpec((R * ROW_TILE, LANES), blk),
            hbm, pl.BlockSpec((1, 1, F), wsel),
            hbm, pl.BlockSpec((1, 1, F), wsel),
            hbm, pl.BlockSpec((1, 1, D), wsel),
        ],
        out_specs=pl.BlockSpec((R * ROW_TILE, LANES), blk),
        scratch_shapes=[pltpu.VMEM((2, 3, D, F), F32), pltpu.VMEM((D, F), BF16), pltpu.VMEM((D, F), BF16),
                        pltpu.VMEM((F, D), BF16), pltpu.SemaphoreType.DMA((2, 3))],
    )
    return pl.pallas_call(
        _expert_kernel,
        out_shape=jax.ShapeDtypeStruct((n_slots * ROW_TILE, LANES), F32),
        grid_spec=grid_spec,
        compiler_params=pltpu.CompilerParams(dimension_semantics=("arbitrary",), vmem_limit_bytes=VMEM_LIMIT),
        name="experts",
    )(block_e, n_used, region, next_e, xs, w_gate, b_gate.reshape(E, 1, F), w_up, b_up.reshape(E, 1, F),
      w_down, b_down.reshape(E, 1, D))


def _combine_kernel(dcur_ref, dnext_ref, x_ref, gate_ref, gfin_ref, ys_ref, o_ref, buf, sem):
    tc = x_ref.shape[0]
    i = pl.program_id(0)
    slot = i % 2

    def issue_all(d_ref, into):
        def issue(grp, carry):
            t0 = pl.multiple_of(grp * ROW_UNROLL, ROW_UNROLL)
            for kk in range(TOP_K):
                for u in range(ROW_UNROLL):
                    src = pl.multiple_of(d_ref[kk, t0 + u] * ROW_TILE, ROW_TILE)
                    row = pl.multiple_of((t0 + u) * ROW_TILE, ROW_TILE)
                    pltpu.make_async_copy(ys_ref.at[pl.ds(src, ROW_TILE)], buf.at[into, kk, pl.ds(row, ROW_TILE)],
                                          sem.at[into]).start(priority=u % 2)
            return carry

        lax.fori_loop(0, tc // ROW_UNROLL, issue, 0)

    def wait_slot(s):
        for kk in range(TOP_K):
            pltpu.make_async_copy(ys_ref.at[pl.ds(0, tc * ROW_TILE)], buf.at[s, kk], sem.at[s]).wait()

    @pl.when(i == 0)
    def _():
        issue_all(dcur_ref, 0)

    wait_slot(slot)
    other = 1 - slot
    for t in range(tc):
        for kk in range(TOP_K):
            src = pl.multiple_of(dnext_ref[kk, t] * ROW_TILE, ROW_TILE)
            pltpu.make_async_copy(ys_ref.at[pl.ds(src, ROW_TILE)], buf.at[other, kk, pl.ds(t * ROW_TILE, ROW_TILE)],
                                  sem.at[other]).start(priority=t % 2)
    for r0 in range(0, tc, COMBINE_ROWS):
        rows = slice(r0, r0 + COMBINE_ROWS)
        acc = x_ref[rows, :]
        for kk in range(TOP_K):
            acc = acc + gate_ref[rows, kk:kk + 1] * _load_row_tiles(buf.at[slot, kk], COMBINE_ROWS, r0)
        o_ref[rows, :] = _rms(acc, gfin_ref[...])

    @pl.when(i == pl.num_programs(0) - 1)
    def _():
        wait_slot(other)


def _combine(x2, gate_t, dest, ys, g_final):
    T, D = x2.shape
    tc = min(COMBINE_TOKENS, T)
    assert T % tc == 0
    nb = T // tc
    return pl.pallas_call(
        _combine_kernel,
        out_shape=jax.ShapeDtypeStruct((T, D), F32),
        grid=(nb,),
        in_specs=[
            pl.BlockSpec((TOP_K, tc), lambda i: (0, i), memory_space=pltpu.SMEM),
            pl.BlockSpec((TOP_K, tc), lambda i: (0, jnp.minimum(i + 1, nb - 1)), memory_space=pltpu.SMEM),
            pl.BlockSpec((tc, D), lambda i: (i, 0)),
            pl.BlockSpec((tc, TOP_K), lambda i: (i, 0)),
            pl.BlockSpec((1, D), lambda i: (0, 0)),
            pl.BlockSpec(memory_space=pl.ANY),
        ],
        out_specs=pl.BlockSpec((tc, D), lambda i: (i, 0)),
        scratch_shapes=[pltpu.VMEM((2, TOP_K, tc * ROW_TILE, LANES), F32), pltpu.SemaphoreType.DMA((2,))],
        compiler_params=pltpu.CompilerParams(dimension_semantics=("arbitrary",), vmem_limit_bytes=VMEM_LIMIT),
        name="combine",
    )(dest, dest, x2, gate_t, g_final.reshape(1, D), ys)


def _moe(x2, xn, idx, gate, rank, cnt, w_gate, b_gate, w_up, b_up, w_down, b_down, g_final):
    T = x2.shape[0]
    E = w_gate.shape[0]
    R = MOE_ROWS
    n_blocks = -(-T * TOP_K // R) + E
    n_slots = n_blocks * R
    counts = cnt[:, 0].astype(jnp.int32)
    padded = (counts + R - 1) // R * R
    pad_end = jnp.cumsum(padded)
    pad_start = pad_end - padded
    dest = rank
    for e in range(E):
        dest = dest + jnp.where(idx == e, pad_start[e], 0)
    n_used = (pad_end[-1:] // R).astype(jnp.int32)
    block_e = jnp.minimum(jnp.sum(jnp.arange(n_blocks)[:, None] * R >= pad_end[None, :], axis=1), E - 1)
    block_e = block_e.astype(jnp.int32)
    blocks = jnp.arange(n_blocks)
    first = (blocks < n_used[0]) & ((blocks == 0) | (block_e != jnp.roll(block_e, 1)))
    region = (jnp.cumsum(first) - 1).astype(jnp.int32)
    later = jnp.where((counts[None, :] > 0) & (jnp.arange(E)[None, :] > block_e[:, None]), jnp.arange(E)[None, :], E)
    next_e = jnp.min(later, axis=1)
    next_e = jnp.where(next_e == E, -1, next_e).astype(jnp.int32)

    xs = _dispatch(xn, dest, counts, pad_end.astype(jnp.int32), n_slots)
    ys = _experts(xs, block_e, n_used, region, next_e, w_gate, b_gate, w_up, b_up, w_down, b_down)
    return _combine(x2, gate.T, dest, ys, g_final)


def kernel(x, mem, g_mix, w_in, b_gates, g_mlstm_head, w_spatial, b_spatial, g_gmlp_v, g_gmlp_out, w_out,
           g_xattn, g_mem, w_q, w_kv, w_xo, g_moe, w_router, b_router, w_gate, b_gate, w_up, b_up,
           w_down, b_down, g_final):
    B, S, D = x.shape
    assert g_mix.shape[0] == 1, "the combine kernel fuses the closing norm, so exactly one layer is supported"
    x1 = _mixer(x, g_mix[0], w_in[0], b_gates[0], g_mlstm_head[0], w_spatial[0], b_spatial[0],
                g_gmlp_v[0], g_gmlp_out[0], w_out[0])
    k, v = _kv_proj(mem, g_mem[0], w_kv[0])
    x2, xn, idx, gate, rank, cnt = _xattn_router(x1, k, v, g_xattn[0], w_q[0], w_xo[0], g_moe[0],
                                                 w_router[0], b_router[0])
    out = _moe(x2.reshape(B * S, D), xn, idx, gate, rank, cnt, w_gate[0], b_gate[0], w_up[0], b_up[0],
               w_down[0], b_down[0], g_final)
    return out.reshape(B, S, D)
```

```python
import functools

import jax
import jax.numpy as jnp
from jax import lax
from jax.experimental import pallas as pl
from jax.experimental.pallas import tpu as pltpu

F32 = jnp.float32
BF16 = jnp.bfloat16

MLSTM_HEADS = 4
MLSTM_DK = 64
MLSTM_DV = 128
CHUNK = 128
GMLP_GROUPS = 4
GMLP_DG = 128
XATTN_HEADS = 4
TOP_K = 4
GATE_SOFTCAP = 15.0
SWIGLU_LIMIT = 7.0
SWIGLU_ALPHA = 1.702
NORM_EPS = 1e-6

LANES = 128
ROW_TILE = 8

MIX_TOKENS = 512
XATTN_TOKENS = 512
MOE_ROWS = 512
DISPATCH_TOKENS = 2048
COMBINE_TOKENS = 256
COMBINE_ROWS = 32
ROW_UNROLL = 16
VMEM_LIMIT = 52 * 1024 * 1024


def _rms(x, g):
    return x * lax.rsqrt(jnp.mean(x * x, axis=-1, keepdims=True) + NORM_EPS) * g


def _gelu_tanh(x):
    return 0.5 * x * (1.0 + jnp.tanh(0.7978845608028654 * (x + 0.044715 * x * x * x)))


def _dot(a, b):
    return jnp.dot(a, b, preferred_element_type=F32)


def _dot_nt(a, b):
    return lax.dot_general(a, b, (((1,), (1,)), ((), ())), preferred_element_type=F32)


def _load_row_tiles(ref, rows, first=0):
    return jnp.concatenate(
        [ref[pl.ds(first * ROW_TILE + j, rows, stride=ROW_TILE), :] for j in range(ROW_TILE)], axis=1)


def _store_row_tiles(ref, val):
    for j in range(ROW_TILE):
        ref[pl.ds(j, val.shape[0], stride=ROW_TILE), :] = val[:, j * LANES:(j + 1) * LANES]


def _split3(x):
    hi = x.astype(BF16)
    r1 = x - hi.astype(F32)
    mid = r1.astype(BF16)
    lo = (r1 - mid.astype(F32)).astype(BF16)
    return hi, mid, lo


def _mixer_kernel(x_ref, gmix_ref, win_ref, bg_ref, ghead_ref, wsp_ref, bsp_ref, gzv_ref, gzo_ref, wout_ref,
                  o_ref, proj_scr, y_scr, ct_scr, m_scr, *, n_chunks, cols):
    qk0, v0, og0, gu0, gv0, gt0 = cols
    H, G, L = MLSTM_HEADS, GMLP_GROUPS, CHUNK

    @pl.when(pl.program_id(1) == 0)
    def _():
        ct_scr[...] = jnp.zeros_like(ct_scr)
        m_scr[...] = jnp.zeros_like(m_scr)

    row = lax.broadcasted_iota(jnp.int32, (L, L), 0)
    col = lax.broadcasted_iota(jnp.int32, (L, L), 1)
    causal = row >= col
    tril = jnp.where(causal, 1.0, 0.0).astype(BF16)
    triu = jnp.where(row <= col, 1.0, 0.0).astype(BF16)
    lane = lax.broadcasted_iota(jnp.int32, (L, LANES), 1)
    ones_col = jnp.where(lane == 0, 1.0, 0.0).astype(F32)

    x = x_ref[0]
    proj_scr[...] = _dot(_rms(x, gmix_ref[...]).astype(BF16), win_ref[...])

    for c in range(n_chunks):
        rows = slice(c * L, (c + 1) * L)
        pre = proj_scr[rows, gt0:gt0 + LANES] + bg_ref[...]
        capped = GATE_SOFTCAP * jnp.tanh(pre * (1.0 / GATE_SOFTCAP))
        log_sig = jnp.minimum(capped, 0.0) - jnp.log1p(jnp.exp(-jnp.abs(capped)))
        lg = jnp.where(lane < H, capped, log_sig)
        lg_t = lg.T[0:ROW_TILE, :]
        bcol = sum(_dot(tril, p) for p in _split3(lg))
        brow = sum(_dot(p, triu) for p in _split3(lg_t))

        hs = range(H)
        b_c = [bcol[:, H + h:H + h + 1] for h in hs]
        i_c = [lg[:, h:h + 1] for h in hs]
        b_r = [brow[H + h:H + h + 1, :] for h in hs]
        i_r = [lg_t[h:h + 1, :] for h in hs]
        m_prev = [m_scr[h, 0:1, 0:1] for h in hs]
        qk = [proj_scr[rows, qk0 + h * LANES:qk0 + (h + 1) * LANES] for h in hs]
        q = [jnp.where(lane < MLSTM_DK, qk[h] * (MLSTM_DK ** -0.5), 0.0) for h in hs]
        k = [jnp.where(lane < MLSTM_DK, pltpu.roll(qk[h], MLSTM_DK, axis=1), 0.0) for h in hs]
        qb = [q[h].astype(BF16) for h in hs]
        kb = [k[h].astype(BF16) for h in hs]
        vaug = [jnp.concatenate([proj_scr[rows, v0 + h * MLSTM_DV:v0 + (h + 1) * MLSTM_DV], ones_col], axis=1)
                for h in hs]

        d = [jnp.where(causal, b_c[h] - b_r[h] + i_r[h], -jnp.inf) for h in hs]
        inter = [b_c[h] + m_prev[h] for h in hs]
        m_t = [jnp.maximum(inter[h], jnp.max(d[h], axis=1, keepdims=True)) for h in hs]
        w_intra = [jnp.exp(d[h] - m_t[h]) for h in hs]
        w_inter = [jnp.exp(inter[h] - m_t[h]) for h in hs]
        s = [_dot_nt(qb[h], kb[h]) * w_intra[h] for h in hs]
        ct = [ct_scr[h] for h in hs]
        na = [_dot(s[h].astype(BF16), vaug[h].astype(BF16)) + w_inter[h] * _dot(qb[h], ct[h].astype(BF16))
              for h in hs]
        hh = [na[h][:, :MLSTM_DV] / jnp.maximum(jnp.abs(na[h][:, MLSTM_DV:MLSTM_DV + 1]), jnp.exp(-m_t[h]))
              for h in hs]

        b_last = [b_c[h][L - 1:L, :] for h in hs]
        g_c = [b_last[h] - b_c[h] + i_c[h] for h in hs]
        m_new = [jnp.maximum(b_last[h] + m_prev[h], jnp.max(g_c[h], axis=0, keepdims=True)) for h in hs]
        wk = [jnp.exp(g_c[h] - m_new[h]) for h in hs]
        decay = [jnp.exp(b_last[h] + m_prev[h] - m_new[h]) for h in hs]
        for h in hs:
            ct_scr[h] = decay[h] * ct[h] + _dot(k[h].T.astype(BF16), (wk[h] * vaug[h]).astype(BF16))
            m_scr[h] = jnp.broadcast_to(m_new[h], m_scr.shape[1:])

        hn = [_rms(hh[h], ghead_ref[:, h * MLSTM_DV:(h + 1) * MLSTM_DV]) for h in hs]
        for h in hs:
            og = proj_scr[rows, og0 + h * MLSTM_DV:og0 + (h + 1) * MLSTM_DV]
            y_scr[rows, h * MLSTM_DV:(h + 1) * MLSTM_DV] = (jax.nn.sigmoid(og) * hn[h]).astype(BF16)

        gs = range(G)
        sl = [slice(g * GMLP_DG, (g + 1) * GMLP_DG) for g in gs]
        z = [_rms(_gelu_tanh(proj_scr[rows, gv0 + g * GMLP_DG:gv0 + (g + 1) * GMLP_DG]), gzv_ref[:, sl[g]]) for g in gs]
        mixed = [_dot(jnp.where(causal, wsp_ref[g], 0.0).astype(BF16), z[g].astype(BF16)) + bsp_ref[g] for g in gs]
        u = [_gelu_tanh(proj_scr[rows, gu0 + g * GMLP_DG:gu0 + (g + 1) * GMLP_DG]) for g in gs]
        yg = [_rms(u[g] * mixed[g], gzo_ref[:, sl[g]]) for g in gs]
        for g in gs:
            y_scr[rows, H * MLSTM_DV + g * GMLP_DG:H * MLSTM_DV + (g + 1) * GMLP_DG] = yg[g].astype(BF16)
    o_ref[0] = x + _dot(y_scr[...], wout_ref[...])


def _mixer(x, g_mix, w_in, b_gates, g_head, w_spatial, b_spatial, g_zv, g_zo, w_out):
    B, S, D = x.shape
    H, G, L = MLSTM_HEADS, GMLP_GROUPS, CHUNK
    qw, vw, gw = H * MLSTM_DK, H * MLSTM_DV, G * GMLP_DG
    ts = min(MIX_TOKENS, S)
    assert S % ts == 0 and ts % L == 0

    assert 2 * MLSTM_DK == LANES
    c = 0
    w_q = w_in[:, c:c + qw].reshape(D, H, MLSTM_DK); c += qw
    w_k = w_in[:, c:c + qw].reshape(D, H, MLSTM_DK); c += qw
    w_qk = jnp.concatenate([w_q, w_k], axis=2).reshape(D, H * LANES)
    w_v = w_in[:, c:c + vw]; c += vw
    w_o = w_in[:, c:c + vw]; c += vw
    w_g = jnp.pad(w_in[:, c:c + 2 * H], ((0, 0), (0, LANES - 2 * H))); c += 2 * H
    w_gu = w_in[:, c:c + gw]; c += gw
    w_gv = w_in[:, c:c + gw]; c += gw
    w_pack = jnp.concatenate([w_qk, w_v, w_o, w_gu, w_gv, w_g], axis=1).astype(BF16)
    hp = H * LANES
    cols = (0, hp, hp + vw, hp + 2 * vw, hp + 2 * vw + gw, hp + 2 * vw + 2 * gw)
    n_cols = w_pack.shape[1]
    bg = jnp.pad(b_gates, (0, LANES - 2 * H)).reshape(1, LANES)
    bsp = jnp.broadcast_to(b_spatial[:, :, None], (G, L, GMLP_DG))

    const = lambda *shape: pl.BlockSpec(shape, lambda b, j: (0,) * len(shape))
    return pl.pallas_call(
        functools.partial(_mixer_kernel, n_chunks=ts // L, cols=cols),
        out_shape=jax.ShapeDtypeStruct((B, S, D), F32),
        grid=(B, S // ts),
        in_specs=[
            pl.BlockSpec((1, ts, D), lambda b, j: (b, j, 0)),
            const(1, D), const(D, n_cols), const(1, LANES), const(1, vw),
            const(G, L, L), const(G, L, GMLP_DG), const(1, gw), const(1, gw), const(vw + gw, D),
        ],
        out_specs=pl.BlockSpec((1, ts, D), lambda b, j: (b, j, 0)),
        scratch_shapes=[
            pltpu.VMEM((ts, n_cols), F32),
            pltpu.VMEM((ts, vw + gw), BF16),
            pltpu.VMEM((H, LANES, 2 * MLSTM_DV), F32),
            pltpu.VMEM((H, ROW_TILE, LANES), F32),
        ],
        compiler_params=pltpu.CompilerParams(
            dimension_semantics=("arbitrary", "arbitrary"), vmem_limit_bytes=VMEM_LIMIT),
        name="mixer",
    )(x, g_mix.reshape(1, D), w_pack, bg, g_head.reshape(1, vw), w_spatial, bsp,
      g_zv.reshape(1, gw), g_zo.reshape(1, gw), w_out.astype(BF16))


def _kv_kernel(mem_ref, g_ref, w_ref, k_ref, v_ref):
    D = mem_ref.shape[-1]
    kv = _dot(_rms(mem_ref[0], g_ref[...]).astype(BF16), w_ref[...])
    k_ref[0] = kv[:, :D].astype(BF16)
    v_ref[0] = kv[:, D:].astype(BF16)


def _kv_proj(mem, g_mem, w_kv):
    B, M, D = mem.shape
    return pl.pallas_call(
        _kv_kernel,
        out_shape=(jax.ShapeDtypeStruct((B, M, D), BF16), jax.ShapeDtypeStruct((B, M, D), BF16)),
        grid=(B,),
        in_specs=[
            pl.BlockSpec((1, M, D), lambda b: (b, 0, 0)),
            pl.BlockSpec((1, D), lambda b: (0, 0)),
            pl.BlockSpec((D, 2 * D), lambda b: (0, 0)),
        ],
        out_specs=(pl.BlockSpec((1, M, D), lambda b: (b, 0, 0)), pl.BlockSpec((1, M, D), lambda b: (b, 0, 0))),
        compiler_params=pltpu.CompilerParams(dimension_semantics=("arbitrary",), vmem_limit_bytes=VMEM_LIMIT),
        name="kv_proj",
    )(mem, g_mem.reshape(1, D), w_kv.astype(BF16))


def _xattn_kernel(x_ref, gx_ref, wq_ref, k_ref, v_ref, wxo_ref, gmoe_ref, wr_ref, br_ref,
                  x2_ref, xn_ref, idx_ref, gate_ref, rank_ref, cnt_ref, o_scr, cnt_scr):
    ts, D = x_ref.shape[1], x_ref.shape[2]
    hd = D // XATTN_HEADS
    E = wr_ref.shape[0]

    @pl.when((pl.program_id(0) == 0) & (pl.program_id(1) == 0))
    def _():
        cnt_scr[...] = jnp.zeros_like(cnt_scr)

    x = x_ref[0]
    q = _dot(_rms(x, gx_ref[...]).astype(BF16), wq_ref[...]) * (hd ** -0.5)
    for h in range(XATTN_HEADS):
        sl = slice(h * hd, (h + 1) * hd)
        s = _dot_nt(q[:, sl].astype(BF16), k_ref[0, :, sl])
        p = jnp.exp(s - jnp.max(s, axis=1, keepdims=True))
        p = p / jnp.sum(p, axis=1, keepdims=True)
        o_scr[:, sl] = _dot(p.astype(BF16), v_ref[0, :, sl]).astype(BF16)
    x2 = x + _dot(o_scr[...], wxo_ref[...])
    x2_ref[0] = x2

    xn = _rms(x2, gmoe_ref[...])
    xh, xm, _ = _split3(xn)
    wh, wm, _ = _split3(wr_ref[...])
    logits = _dot_nt(wh, xh) + (_dot_nt(wh, xm) + _dot_nt(wm, xh)) + br_ref[...]

    e_iota = lax.broadcasted_iota(jnp.int32, (E, ts), 0)
    work = logits
    tops, idxs, hots = [], [], []
    for _ in range(TOP_K):
        m = jnp.max(work, axis=0, keepdims=True)
        i = jnp.min(jnp.where(work == m, e_iota, E), axis=0, keepdims=True)
        hot = e_iota == i
        work = jnp.where(hot, -jnp.inf, work)
        tops.append(m); idxs.append(i); hots.append(hot)
    ex = [jnp.exp(t - tops[0]) for t in tops]
    tot = ex[0] + ex[1] + ex[2] + ex[3]
    gate_ref[...] = jnp.concatenate([e / tot for e in ex], axis=0)
    idx_ref[...] = jnp.concatenate(idxs, axis=0)

    hot_all = jnp.where(hots[0] | hots[1] | hots[2] | hots[3], 1.0, 0.0)
    r = lax.broadcasted_iota(jnp.int32, (ts, ts), 0)
    c = lax.broadcasted_iota(jnp.int32, (ts, ts), 1)
    before = jnp.where(r < c, 1.0, 0.0).astype(BF16)
    base = cnt_scr[:, 0:1] + _dot(hot_all.astype(BF16), before)
    ranks = [jnp.sum(jnp.where(hot, base, 0.0), axis=0, keepdims=True) for hot in hots]
    rank_ref[...] = jnp.concatenate(ranks, axis=0).astype(jnp.int32)
    cnt_new = cnt_scr[...] + jnp.sum(hot_all, axis=1, keepdims=True)
    cnt_scr[...] = cnt_new
    cnt_ref[...] = cnt_new
    _store_row_tiles(xn_ref, xn)


def _xattn_router(x, k, v, g_xattn, w_q, w_xo, g_moe, w_router, b_router):
    B, S, D = x.shape
    M = k.shape[1]
    E = w_router.shape[1]
    ts = min(XATTN_TOKENS, S)
    assert S % ts == 0
    nj = S // ts
    T = B * S
    const = lambda *shape: pl.BlockSpec(shape, lambda b, j: (0,) * len(shape))
    tok = lambda rows: pl.BlockSpec((rows, ts), lambda b, j: (0, b * nj + j))
    return pl.pallas_call(
        _xattn_kernel,
        out_shape=(
            jax.ShapeDtypeStruct((B, S, D), F32),
            jax.ShapeDtypeStruct((T * ROW_TILE, LANES), F32),
            jax.ShapeDtypeStruct((TOP_K, T), jnp.int32),
            jax.ShapeDtypeStruct((TOP_K, T), F32),
            jax.ShapeDtypeStruct((TOP_K, T), jnp.int32),
            jax.ShapeDtypeStruct((E, LANES), F32),
        ),
        grid=(B, nj),
        in_specs=[
            pl.BlockSpec((1, ts, D), lambda b, j: (b, j, 0)),
            const(1, D), const(D, D),
            pl.BlockSpec((1, M, D), lambda b, j: (b, 0, 0)),
            pl.BlockSpec((1, M, D), lambda b, j: (b, 0, 0)),
            const(D, D), const(1, D), const(E, D), const(E, 1),
        ],
        out_specs=(
            pl.BlockSpec((1, ts, D), lambda b, j: (b, j, 0)),
            pl.BlockSpec((ts * ROW_TILE, LANES), lambda b, j: (b * nj + j, 0)),
            tok(TOP_K), tok(TOP_K), tok(TOP_K),
            const(E, LANES),
        ),
        scratch_shapes=[pltpu.VMEM((ts, D), BF16), pltpu.VMEM((E, LANES), F32)],
        compiler_params=pltpu.CompilerParams(
            dimension_semantics=("arbitrary", "arbitrary"), vmem_limit_bytes=VMEM_LIMIT),
        name="xattn_router",
    )(x, g_xattn.reshape(1, D), w_q.astype(BF16), k, v, w_xo.astype(BF16), g_moe.reshape(1, D),
      w_router.T, b_router.reshape(E, 1))


def _dispatch_kernel(dest_ref, cnt_ref, pend_ref, xn_ref, xs_ref, zero_scr, sem, zsem, *, n_experts):
    tt = xn_ref.shape[0] // ROW_TILE
    R = zero_scr.shape[0] // ROW_TILE

    @pl.when(pl.program_id(0) == 0)
    def _():
        zero_scr[...] = jnp.zeros_like(zero_scr)

        def fill_copy(e):
            start = pl.multiple_of((pend_ref[e] - R) * ROW_TILE, R * ROW_TILE)
            return pltpu.make_async_copy(zero_scr, xs_ref.at[pl.ds(start, R * ROW_TILE)], zsem)

        def start_fill(e, carry):
            @pl.when(cnt_ref[e] > 0)
            def _():
                fill_copy(e).start()
            return carry

        def wait_fill(e, carry):
            @pl.when(cnt_ref[e] > 0)
            def _():
                fill_copy(e).wait()
            return carry

        lax.fori_loop(0, n_experts, start_fill, 0)
        lax.fori_loop(0, n_experts, wait_fill, 0)

    def issue(grp, carry):
        t0 = pl.multiple_of(grp * ROW_UNROLL, ROW_UNROLL)
        for kk in range(TOP_K):
            for u in range(ROW_UNROLL):
                dst = pl.multiple_of(dest_ref[0, 0, kk * tt + t0 + u] * ROW_TILE, ROW_TILE)
                row = pl.multiple_of((t0 + u) * ROW_TILE, ROW_TILE)
                pltpu.make_async_copy(xn_ref.at[pl.ds(row, ROW_TILE)], xs_ref.at[pl.ds(dst, ROW_TILE)],
                                      sem).start(priority=u % 2)
        return carry

    lax.fori_loop(0, tt // ROW_UNROLL, issue, 0)
    for kk in range(TOP_K):
        pltpu.make_async_copy(xn_ref, xs_ref.at[pl.ds(0, tt * ROW_TILE)], sem).wait()


def _dispatch(xn, dest, counts, pad_end, n_slots):
    T = dest.shape[1]
    tt = min(DISPATCH_TOKENS, T)
    assert T % tt == 0
    E = counts.shape[0]
    dest = dest.reshape(TOP_K, T // tt, tt).transpose(1, 0, 2).reshape(T // tt, 1, TOP_K * tt)
    return pl.pallas_call(
        functools.partial(_dispatch_kernel, n_experts=E),
        out_shape=jax.ShapeDtypeStruct((n_slots * ROW_TILE, LANES), F32),
        grid=(T // tt,),
        in_specs=[
            pl.BlockSpec((1, 1, TOP_K * tt), lambda i: (i, 0, 0), memory_space=pltpu.SMEM),
            pl.BlockSpec(memory_space=pltpu.SMEM),
            pl.BlockSpec(memory_space=pltpu.SMEM),
            pl.BlockSpec((tt * ROW_TILE, LANES), lambda i: (i, 0)),
        ],
        out_specs=pl.BlockSpec(memory_space=pl.ANY),
        scratch_shapes=[pltpu.VMEM((MOE_ROWS * ROW_TILE, LANES), F32), pltpu.SemaphoreType.DMA,
                        pltpu.SemaphoreType.DMA],
        compiler_params=pltpu.CompilerParams(
            dimension_semantics=("arbitrary",), vmem_limit_bytes=VMEM_LIMIT, has_side_effects=True),
        name="dispatch",
    )(dest, counts, pad_end, xn)


def _expert_kernel(be_ref, nu_ref, reg_ref, nxt_ref, xs_ref, wg_ref, bg_ref, wu_ref, bu_ref, wd_ref, bd_ref, y_ref,
                   wf_scr, wg_scr, wu_scr, wd_scr, sem):
    j = pl.program_id(0)
    active = j < nu_ref[0]
    w_hbm = (wg_ref, wu_ref, wd_ref)

    def fetch(e, slot):
        return [pltpu.make_async_copy(w_hbm[i].at[e], wf_scr.at[slot, i], sem.at[slot, i]) for i in range(3)]

    @pl.when(active & ((j == 0) | (be_ref[j] != be_ref[jnp.maximum(j - 1, 0)])))
    def _():
        slot = reg_ref[j] % 2

        @pl.when(j == 0)
        def _():
            for cp in fetch(be_ref[0], 0):
                cp.start()

        for cp in fetch(be_ref[j], slot):
            cp.wait()
        wg_scr[...] = wf_scr[slot, 0].astype(BF16)
        wu_scr[...] = wf_scr[slot, 1].astype(BF16)
        wd_scr[...] = wf_scr[slot, 2].astype(BF16)

        @pl.when(nxt_ref[j] >= 0)
        def _():
            for cp in fetch(nxt_ref[j], 1 - slot):
                cp.start()

    @pl.when(active)
    def _():
        xb = _load_row_tiles(xs_ref, xs_ref.shape[0] // ROW_TILE).astype(BF16)
        g = jnp.minimum(_dot(xb, wg_scr[...]) + bg_ref[0], SWIGLU_LIMIT)
        u = jnp.clip(_dot(xb, wu_scr[...]) + bu_ref[0], -SWIGLU_LIMIT, SWIGLU_LIMIT)
        hdn = (u + 1.0) * (g * jax.nn.sigmoid(SWIGLU_ALPHA * g))
        _store_row_tiles(y_ref, _dot(hdn.astype(BF16), wd_scr[...]) + bd_ref[0])


def _experts(xs, block_e, n_used, region, next_e, w_gate, b_gate, w_up, b_up, w_down, b_down):
    E, D, F = w_gate.shape
    assert D == ROW_TILE * LANES and F == D, "the two weight staging slots hold (D, F) and (F, D) alike"
    n_slots = xs.shape[0] // ROW_TILE
    R = MOE_ROWS
    n_blocks = n_slots // R
    blk = lambda j, be, nu, rg, nx: (jnp.minimum(j, nu[0] - 1), 0)
    wsel = lambda j, be, nu, rg, nx: (be[jnp.minimum(j, nu[0] - 1)], 0, 0)
    hbm = pl.BlockSpec(memory_space=pl.ANY)
    grid_spec = pltpu.PrefetchScalarGridSpec(
        num_scalar_prefetch=4,
        grid=(n_blocks,),
        in_specs=[
            pl.BlockSpec((R * ROW_TILE, LANES), blk),
            hbm, pl.BlockSpec((1, 1, F), wsel),
            hbm, pl.BlockSpec((1, 1, F), wsel),
            hbm, pl.BlockSpec((1, 1, D), wsel),
        ],
        out_specs=pl.BlockSpec((R * ROW_TILE, LANES), blk),
        scratch_shapes=[pltpu.VMEM((2, 3, D, F), F32), pltpu.VMEM((D, F), BF16), pltpu.VMEM((D, F), BF16),
                        pltpu.VMEM((F, D), BF16), pltpu.SemaphoreType.DMA((2, 3))],
    )
    return pl.pallas_call(
        _expert_kernel,
        out_shape=jax.ShapeDtypeStruct((n_slots * ROW_TILE, LANES), F32),
        grid_spec=grid_spec,
        compiler_params=pltpu.CompilerParams(dimension_semantics=("arbitrary",), vmem_limit_bytes=VMEM_LIMIT),
        name="experts",
    )(block_e, n_used, region, next_e, xs, w_gate, b_gate.reshape(E, 1, F), w_up, b_up.reshape(E, 1, F),
      w_down, b_down.reshape(E, 1, D))


def _combine_kernel(dcur_ref, dnext_ref, x_ref, gate_ref, gfin_ref, ys_ref, o_ref, buf, sem):
    tc = x_ref.shape[0]
    i = pl.program_id(0)
    slot = i % 2

    def issue_all(d_ref, into):
        def issue(grp, carry):
            t0 = pl.multiple_of(grp * ROW_UNROLL, ROW_UNROLL)
            for kk in range(TOP_K):
                for u in range(ROW_UNROLL):
                    src = pl.multiple_of(d_ref[kk, t0 + u] * ROW_TILE, ROW_TILE)
                    row = pl.multiple_of((t0 + u) * ROW_TILE, ROW_TILE)
                    pltpu.make_async_copy(ys_ref.at[pl.ds(src, ROW_TILE)], buf.at[into, kk, pl.ds(row, ROW_TILE)],
                                          sem.at[into]).start(priority=u % 2)
            return carry

        lax.fori_loop(0, tc // ROW_UNROLL, issue, 0)

    def wait_slot(s):
        for kk in range(TOP_K):
            pltpu.make_async_copy(ys_ref.at[pl.ds(0, tc * ROW_TILE)], buf.at[s, kk], sem.at[s]).wait()

    @pl.when(i == 0)
    def _():
        issue_all(dcur_ref, 0)

    wait_slot(slot)
    other = 1 - slot
    for t in range(tc):
        for kk in range(TOP_K):
            src = pl.multiple_of(dnext_ref[kk, t] * ROW_TILE, ROW_TILE)
            pltpu.make_async_copy(ys_ref.at[pl.ds(src, ROW_TILE)], buf.at[other, kk, pl.ds(t * ROW_TILE, ROW_TILE)],
                                  sem.at[other]).start(priority=t % 2)
    for r0 in range(0, tc, COMBINE_ROWS):
        rows = slice(r0, r0 + COMBINE_ROWS)
        acc = x_ref[rows, :]
        for kk in range(TOP_K):
            acc = acc + gate_ref[rows, kk:kk + 1] * _load_row_tiles(buf.at[slot, kk], COMBINE_ROWS, r0)
        o_ref[rows, :] = _rms(acc, gfin_ref[...])

    @pl.when(i == pl.num_programs(0) - 1)
    def _():
        wait_slot(other)


def _combine(x2, gate_t, dest, ys, g_final):
    T, D = x2.shape
    tc = min(COMBINE_TOKENS, T)
    assert T % tc == 0
    nb = T // tc
    return pl.pallas_call(
        _combine_kernel,
        out_shape=jax.ShapeDtypeStruct((T, D), F32),
        grid=(nb,),
        in_specs=[
            pl.BlockSpec((TOP_K, tc), lambda i: (0, i), memory_space=pltpu.SMEM),
            pl.BlockSpec((TOP_K, tc), lambda i: (0, jnp.minimum(i + 1, nb - 1)), memory_space=pltpu.SMEM),
            pl.BlockSpec((tc, D), lambda i: (i, 0)),
            pl.BlockSpec((tc, TOP_K), lambda i: (i, 0)),
            pl.BlockSpec((1, D), lambda i: (0, 0)),
            pl.BlockSpec(memory_space=pl.ANY),
        ],
        out_specs=pl.BlockSpec((tc, D), lambda i: (i, 0)),
        scratch_shapes=[pltpu.VMEM((2, TOP_K, tc * ROW_TILE, LANES), F32), pltpu.SemaphoreType.DMA((2,))],
        compiler_params=pltpu.CompilerParams(dimension_semantics=("arbitrary",), vmem_limit_bytes=VMEM_LIMIT),
        name="combine",
    )(dest, dest, x2, gate_t, g_final.reshape(1, D), ys)


def _moe(x2, xn, idx, gate, rank, cnt, w_gate, b_gate, w_up, b_up, w_down, b_down, g_final):
    T = x2.shape[0]
    E = w_gate.shape[0]
    R = MOE_ROWS
    n_blocks = -(-T * TOP_K // R) + E
    n_slots = n_blocks * R
    counts = cnt[:, 0].astype(jnp.int32)
    padded = (counts + R - 1) // R * R
    pad_end = jnp.cumsum(padded)
    pad_start = pad_end - padded
    dest = rank
    for e in range(E):
        dest = dest + jnp.where(idx == e, pad_start[e], 0)
    n_used = (pad_end[-1:] // R).astype(jnp.int32)
    block_e = jnp.minimum(jnp.sum(jnp.arange(n_blocks)[:, None] * R >= pad_end[None, :], axis=1), E - 1)
    block_e = block_e.astype(jnp.int32)
    blocks = jnp.arange(n_blocks)
    first = (blocks < n_used[0]) & ((blocks == 0) | (block_e != jnp.roll(block_e, 1)))
    region = (jnp.cumsum(first) - 1).astype(jnp.int32)
    later = jnp.where((counts[None, :] > 0) & (jnp.arange(E)[None, :] > block_e[:, None]), jnp.arange(E)[None, :], E)
    next_e = jnp.min(later, axis=1)
    next_e = jnp.where(next_e == E, -1, next_e).astype(jnp.int32)

    xs = _dispatch(xn, dest, counts, pad_end.astype(jnp.int32), n_slots)
    ys = _experts(xs, block_e, n_used, region, next_e, w_gate, b_gate, w_up, b_up, w_down, b_down)
    return _combine(x2, gate.T, dest, ys, g_final)


def kernel(x, mem, g_mix, w_in, b_gates, g_mlstm_head, w_spatial, b_spatial, g_gmlp_v, g_gmlp_out, w_out,
           g_xattn, g_mem, w_q, w_kv, w_xo, g_moe, w_router, b_router, w_gate, b_gate, w_up, b_up,
           w_down, b_down, g_final):
    B, S, D = x.shape
    assert g_mix.shape[0] == 1, "the combine kernel fuses the closing norm, so exactly one layer is supported"
    x1 = _mixer(x, g_mix[0], w_in[0], b_gates[0], g_mlstm_head[0], w_spatial[0], b_spatial[0],
                g_gmlp_v[0], g_gmlp_out[0], w_out[0])
    k, v = _kv_proj(mem, g_mem[0], w_kv[0])
    x2, xn, idx, gate, rank, cnt = _xattn_router(x1, k, v, g_xattn[0], w_q[0], w_xo[0], g_moe[0],
                                                 w_router[0], b_router[0])
    out = _moe(x2.reshape(B * S, D), xn, idx, gate, rank, cnt, w_gate[0], b_gate[0], w_up[0], b_up[0],
               w_down[0], b_down[0], g_final)
    return out.reshape(B, S, D)
```

```python
import functools

import jax
import jax.numpy as jnp
from jax import lax
from jax.experimental import pallas as pl
from jax.experimental.pallas import tpu as pltpu

F32 = jnp.float32
BF16 = jnp.bfloat16

MLSTM_HEADS = 4
MLSTM_DK = 64
MLSTM_DV = 128
CHUNK = 128
GMLP_GROUPS = 4
GMLP_DG = 128
XATTN_HEADS = 4
TOP_K = 4
GATE_SOFTCAP = 15.0
SWIGLU_LIMIT = 7.0
SWIGLU_ALPHA = 1.702
NORM_EPS = 1e-6

LANES = 128
ROW_TILE = 8

MIX_TOKENS = 512
PROJ_PIECE = 512
XATTN_TOKENS = 512
MOE_ROWS = 512
DISPATCH_TOKENS = 2048
COMBINE_TOKENS = 256
COMBINE_ROWS = 32
ROW_UNROLL = 16
VMEM_LIMIT = 52 * 1024 * 1024


def _rms(x, g):
    return x * lax.rsqrt(jnp.mean(x * x, axis=-1, keepdims=True) + NORM_EPS) * g


def _gelu_tanh(x):
    return 0.5 * x * (1.0 + jnp.tanh(0.7978845608028654 * (x + 0.044715 * x * x * x)))


def _dot(a, b):
    return jnp.dot(a, b, preferred_element_type=F32)


def _dot_nt(a, b):
    return lax.dot_general(a, b, (((1,), (1,)), ((), ())), preferred_element_type=F32)


def _load_row_tiles(ref, rows, first=0):
    return jnp.concatenate(
        [ref[pl.ds(first * ROW_TILE + j, rows, stride=ROW_TILE), :] for j in range(ROW_TILE)], axis=1)


def _store_row_tiles(ref, val):
    for j in range(ROW_TILE):
        ref[pl.ds(j, val.shape[0], stride=ROW_TILE), :] = val[:, j * LANES:(j + 1) * LANES]


def _split3(x):
    hi = x.astype(BF16)
    r1 = x - hi.astype(F32)
    mid = r1.astype(BF16)
    lo = (r1 - mid.astype(F32)).astype(BF16)
    return hi, mid, lo


def _mixer_kernel(x_ref, gmix_ref, win_ref, bg_ref, ghead_ref, wsp_ref, bsp_ref, gzv_ref, gzo_ref, wout_ref,
                  o_ref, xnb_scr, proj_scr, y_scr, ct_scr, m_scr, *, n_chunks, cols):
    qk0, v0, og0, gu0, gv0, gt0 = cols
    H, G, L = MLSTM_HEADS, GMLP_GROUPS, CHUNK

    @pl.when(pl.program_id(1) == 0)
    def _():
        ct_scr[...] = jnp.zeros_like(ct_scr)
        m_scr[...] = jnp.zeros_like(m_scr)

    row = lax.broadcasted_iota(jnp.int32, (L, L), 0)
    col = lax.broadcasted_iota(jnp.int32, (L, L), 1)
    causal = row >= col
    tril = jnp.where(causal, 1.0, 0.0).astype(BF16)
    triu = jnp.where(row <= col, 1.0, 0.0).astype(BF16)
    lane = lax.broadcasted_iota(jnp.int32, (L, LANES), 1)
    ones_col = jnp.where(lane == 0, 1.0, 0.0).astype(F32)

    xnb_scr[...] = _rms(x_ref[0], gmix_ref[...]).astype(BF16)

    def in_pieces(c):
        r = slice(c * L, (c + 1) * L)
        for p in range(0, proj_scr.shape[1], PROJ_PIECE):
            cs = slice(p, min(p + PROJ_PIECE, proj_scr.shape[1]))
            yield lambda cs=cs: proj_scr.__setitem__((r, cs), _dot(xnb_scr[r, :], win_ref[:, cs]))

    def out_pieces(c):
        r = slice(c * L, (c + 1) * L)
        for p in range(0, o_ref.shape[2], PROJ_PIECE):
            cs = slice(p, p + PROJ_PIECE)
            yield lambda cs=cs: o_ref.__setitem__((0, r, cs), x_ref[0, r, cs] + _dot(y_scr[r, :], wout_ref[:, cs]))

    fillers = []

    def tick(n=1):
        for _ in range(min(n, len(fillers))):
            fillers.pop(0)()

    for piece in in_pieces(0):
        piece()

    for c in range(n_chunks):
        rows = slice(c * L, (c + 1) * L)
        if c + 1 < n_chunks:
            fillers.extend(in_pieces(c + 1))
        if c > 0:
            fillers.extend(out_pieces(c - 1))
        pre = proj_scr[rows, gt0:gt0 + LANES] + bg_ref[...]
        capped = GATE_SOFTCAP * jnp.tanh(pre * (1.0 / GATE_SOFTCAP))
        log_sig = jnp.minimum(capped, 0.0) - jnp.log1p(jnp.exp(-jnp.abs(capped)))
        lg = jnp.where(lane < H, capped, log_sig)
        lg_t = lg.T[0:ROW_TILE, :]
        bcol = sum(_dot(tril, p) for p in _split3(lg))
        brow = sum(_dot(p, triu) for p in _split3(lg_t))

        hs = range(H)
        b_c = [bcol[:, H + h:H + h + 1] for h in hs]
        i_c = [lg[:, h:h + 1] for h in hs]
        b_r = [brow[H + h:H + h + 1, :] for h in hs]
        i_r = [lg_t[h:h + 1, :] for h in hs]
        m_prev = [m_scr[h, 0:1, 0:1] for h in hs]
        qk = [proj_scr[rows, qk0 + h * LANES:qk0 + (h + 1) * LANES] for h in hs]
        q = [jnp.where(lane < MLSTM_DK, qk[h] * (MLSTM_DK ** -0.5), 0.0) for h in hs]
        k = [jnp.where(lane < MLSTM_DK, pltpu.roll(qk[h], MLSTM_DK, axis=1), 0.0) for h in hs]
        qb = [q[h].astype(BF16) for h in hs]
        kb = [k[h].astype(BF16) for h in hs]
        tick()
        vaug = [jnp.concatenate([proj_scr[rows, v0 + h * MLSTM_DV:v0 + (h + 1) * MLSTM_DV], ones_col], axis=1)
                for h in hs]

        d = [jnp.where(causal, b_c[h] - b_r[h] + i_r[h], -jnp.inf) for h in hs]
        tick()
        inter = [b_c[h] + m_prev[h] for h in hs]
        m_t = [jnp.maximum(inter[h], jnp.max(d[h], axis=1, keepdims=True)) for h in hs]
        tick()
        w_intra = [jnp.exp(d[h] - m_t[h]) for h in hs]
        tick()
        w_inter = [jnp.exp(inter[h] - m_t[h]) for h in hs]
        s = [_dot_nt(qb[h], kb[h]) * w_intra[h] for h in hs]
        tick()
        ct = [ct_scr[h] for h in hs]
        na = [_dot(s[h].astype(BF16), vaug[h].astype(BF16)) + w_inter[h] * _dot(qb[h], ct[h].astype(BF16))
              for h in hs]
        hh = [na[h][:, :MLSTM_DV] / jnp.maximum(jnp.abs(na[h][:, MLSTM_DV:MLSTM_DV + 1]), jnp.exp(-m_t[h]))
              for h in hs]

        b_last = [b_c[h][L - 1:L, :] for h in hs]
        g_c = [b_last[h] - b_c[h] + i_c[h] for h in hs]
        m_new = [jnp.maximum(b_last[h] + m_prev[h], jnp.max(g_c[h], axis=0, keepdims=True)) for h in hs]
        wk = [jnp.exp(g_c[h] - m_new[h]) for h in hs]
        decay = [jnp.exp(b_last[h] + m_prev[h] - m_new[h]) for h in hs]
        tick()
        for h in hs:
            ct_scr[h] = decay[h] * ct[h] + _dot(k[h].T.astype(BF16), (wk[h] * vaug[h]).astype(BF16))
            m_scr[h] = jnp.broadcast_to(m_new[h], m_scr.shape[1:])

        hn = [_rms(hh[h], ghead_ref[:, h * MLSTM_DV:(h + 1) * MLSTM_DV]) for h in hs]
        tick()
        for h in hs:
            og = proj_scr[rows, og0 + h * MLSTM_DV:og0 + (h + 1) * MLSTM_DV]
            y_scr[rows, h * MLSTM_DV:(h + 1) * MLSTM_DV] = (jax.nn.sigmoid(og) * hn[h]).astype(BF16)

        gs = range(G)
        sl = [slice(g * GMLP_DG, (g + 1) * GMLP_DG) for g in gs]
        z = [_rms(_gelu_tanh(proj_scr[rows, gv0 + g * GMLP_DG:gv0 + (g + 1) * GMLP_DG]), gzv_ref[:, sl[g]]) for g in gs]
        mixed = [_dot(jnp.where(causal, wsp_ref[g], 0.0).astype(BF16), z[g].astype(BF16)) + bsp_ref[g] for g in gs]
        u = [_gelu_tanh(proj_scr[rows, gu0 + g * GMLP_DG:gu0 + (g + 1) * GMLP_DG]) for g in gs]
        tick()
        yg = [_rms(u[g] * mixed[g], gzo_ref[:, sl[g]]) for g in gs]
        tick()
        for g in gs:
            y_scr[rows, H * MLSTM_DV + g * GMLP_DG:H * MLSTM_DV + (g + 1) * GMLP_DG] = yg[g].astype(BF16)
        tick(len(fillers))
    for piece in out_pieces(n_chunks - 1):
        piece()


def _mixer(x, g_mix, w_in, b_gates, g_head, w_spatial, b_spatial, g_zv, g_zo, w_out):
    B, S, D = x.shape
    H, G, L = MLSTM_HEADS, GMLP_GROUPS, CHUNK
    qw, vw, gw = H * MLSTM_DK, H * MLSTM_DV, G * GMLP_DG
    ts = min(MIX_TOKENS, S)
    assert S % ts == 0 and ts % L == 0

    assert 2 * MLSTM_DK == LANES
    c = 0
    w_q = w_in[:, c:c + qw].reshape(D, H, MLSTM_DK); c += qw
    w_k = w_in[:, c:c + qw].reshape(D, H, MLSTM_DK); c += qw
    w_qk = jnp.concatenate([w_q, w_k], axis=2).reshape(D, H * LANES)
    w_v = w_in[:, c:c + vw]; c += vw
    w_o = w_in[:, c:c + vw]; c += vw
    w_g = jnp.pad(w_in[:, c:c + 2 * H], ((0, 0), (0, LANES - 2 * H))); c += 2 * H
    w_gu = w_in[:, c:c + gw]; c += gw
    w_gv = w_in[:, c:c + gw]; c += gw
    w_pack = jnp.concatenate([w_qk, w_v, w_o, w_gu, w_gv, w_g], axis=1).astype(BF16)
    hp = H * LANES
    cols = (0, hp, hp + vw, hp + 2 * vw, hp + 2 * vw + gw, hp + 2 * vw + 2 * gw)
    n_cols = w_pack.shape[1]
    bg = jnp.pad(b_gates, (0, LANES - 2 * H)).reshape(1, LANES)
    bsp = jnp.broadcast_to(b_spatial[:, :, None], (G, L, GMLP_DG))

    const = lambda *shape: pl.BlockSpec(shape, lambda b, j: (0,) * len(shape))
    return pl.pallas_call(
        functools.partial(_mixer_kernel, n_chunks=ts // L, cols=cols),
        out_shape=jax.ShapeDtypeStruct((B, S, D), F32),
        grid=(B, S // ts),
        in_specs=[
            pl.BlockSpec((1, ts, D), lambda b, j: (b, j, 0)),
            const(1, D), const(D, n_cols), const(1, LANES), const(1, vw),
            const(G, L, L), const(G, L, GMLP_DG), const(1, gw), const(1, gw), const(vw + gw, D),
        ],
        out_specs=pl.BlockSpec((1, ts, D), lambda b, j: (b, j, 0)),
        scratch_shapes=[
            pltpu.VMEM((ts, D), BF16),
            pltpu.VMEM((ts, n_cols), F32),
            pltpu.VMEM((ts, vw + gw), BF16),
            pltpu.VMEM((H, LANES, 2 * MLSTM_DV), F32),
            pltpu.VMEM((H, ROW_TILE, LANES), F32),
        ],
        compiler_params=pltpu.CompilerParams(
            dimension_semantics=("arbitrary", "arbitrary"), vmem_limit_bytes=VMEM_LIMIT),
        name="mixer",
    )(x, g_mix.reshape(1, D), w_pack, bg, g_head.reshape(1, vw), w_spatial, bsp,
      g_zv.reshape(1, gw), g_zo.reshape(1, gw), w_out.astype(BF16))


def _kv_kernel(mem_ref, g_ref, w_ref, k_ref, v_ref):
    D = mem_ref.shape[-1]
    kv = _dot(_rms(mem_ref[0], g_ref[...]).astype(BF16), w_ref[...])
    k_ref[0] = kv[:, :D].astype(BF16)
    v_ref[0] = kv[:, D:].astype(BF16)


def _kv_proj(mem, g_mem, w_kv):
    B, M, D = mem.shape
    return pl.pallas_call(
        _kv_kernel,
        out_shape=(jax.ShapeDtypeStruct((B, M, D), BF16), jax.ShapeDtypeStruct((B, M, D), BF16)),
        grid=(B,),
        in_specs=[
            pl.BlockSpec((1, M, D), lambda b: (b, 0, 0)),
            pl.BlockSpec((1, D), lambda b: (0, 0)),
            pl.BlockSpec((D, 2 * D), lambda b: (0, 0)),
        ],
        out_specs=(pl.BlockSpec((1, M, D), lambda b: (b, 0, 0)), pl.BlockSpec((1, M, D), lambda b: (b, 0, 0))),
        compiler_params=pltpu.CompilerParams(dimension_semantics=("arbitrary",), vmem_limit_bytes=VMEM_LIMIT),
        name="kv_proj",
    )(mem, g_mem.reshape(1, D), w_kv.astype(BF16))


def _xattn_kernel(x_ref, gx_ref, wq_ref, k_ref, v_ref, wxo_ref, gmoe_ref, wr_ref, br_ref,
                  x2_ref, xn_ref, idx_ref, gate_ref, rank_ref, cnt_ref, o_scr, cnt_scr):
    ts, D = x_ref.shape[1], x_ref.shape[2]
    hd = D // XATTN_HEADS
    E = wr_ref.shape[0]

    @pl.when((pl.program_id(0) == 0) & (pl.program_id(1) == 0))
    def _():
        cnt_scr[...] = jnp.zeros_like(cnt_scr)

    x = x_ref[0]
    q = _dot(_rms(x, gx_ref[...]).astype(BF16), wq_ref[...]) * (hd ** -0.5)
    hs = range(XATTN_HEADS)
    sl = [slice(h * hd, (h + 1) * hd) for h in hs]
    s = [_dot_nt(q[:, sl[h]].astype(BF16), k_ref[0, :, sl[h]]) for h in hs]
    p = [jnp.exp(s[h] - jnp.max(s[h], axis=1, keepdims=True)) for h in hs]
    p = [p[h] / jnp.sum(p[h], axis=1, keepdims=True) for h in hs]
    for h in hs:
        o_scr[:, sl[h]] = _dot(p[h].astype(BF16), v_ref[0, :, sl[h]]).astype(BF16)
    x2 = x + _dot(o_scr[...], wxo_ref[...])
    x2_ref[0] = x2

    xn = _rms(x2, gmoe_ref[...])
    xh, xm, _ = _split3(xn)
    wh, wm, _ = _split3(wr_ref[...])
    logits = _dot_nt(wh, xh) + (_dot_nt(wh, xm) + _dot_nt(wm, xh)) + br_ref[...]

    e_iota = lax.broadcasted_iota(jnp.int32, (E, ts), 0)
    work = logits
    tops, idxs, hots = [], [], []
    for _ in range(TOP_K):
        m = jnp.max(work, axis=0, keepdims=True)
        i = jnp.min(jnp.where(work == m, e_iota, E), axis=0, keepdims=True)
        hot = e_iota == i
        work = jnp.where(hot, -jnp.inf, work)
        tops.append(m); idxs.append(i); hots.append(hot)
    ex = [jnp.exp(t - tops[0]) for t in tops]
    tot = ex[0] + ex[1] + ex[2] + ex[3]
    gate_ref[...] = jnp.concatenate([e / tot for e in ex], axis=0)
    idx_ref[...] = jnp.concatenate(idxs, axis=0)

    hot_all = jnp.where(hots[0] | hots[1] | hots[2] | hots[3], 1.0, 0.0)
    r = lax.broadcasted_iota(jnp.int32, (ts, ts), 0)
    c = lax.broadcasted_iota(jnp.int32, (ts, ts), 1)
    before = jnp.where(r < c, 1.0, 0.0).astype(BF16)
    base = cnt_scr[:, 0:1] + _dot(hot_all.astype(BF16), before)
    ranks = [jnp.sum(jnp.where(hot, base, 0.0), axis=0, keepdims=True) for hot in hots]
    rank_ref[...] = jnp.concatenate(ranks, axis=0).astype(jnp.int32)
    cnt_new = cnt_scr[...] + jnp.sum(hot_all, axis=1, keepdims=True)
    cnt_scr[...] = cnt_new
    cnt_ref[...] = cnt_new
    _store_row_tiles(xn_ref, xn)


def _xattn_router(x, k, v, g_xattn, w_q, w_xo, g_moe, w_router, b_router):
    B, S, D = x.shape
    M = k.shape[1]
    E = w_router.shape[1]
    ts = min(XATTN_TOKENS, S)
    assert S % ts == 0
    nj = S // ts
    T = B * S
    const = lambda *shape: pl.BlockSpec(shape, lambda b, j: (0,) * len(shape))
    tok = lambda rows: pl.BlockSpec((rows, ts), lambda b, j: (0, b * nj + j))
    return pl.pallas_call(
        _xattn_kernel,
        out_shape=(
            jax.ShapeDtypeStruct((B, S, D), F32),
            jax.ShapeDtypeStruct((T * ROW_TILE, LANES), F32),
            jax.ShapeDtypeStruct((TOP_K, T), jnp.int32),
            jax.ShapeDtypeStruct((TOP_K, T), F32),
            jax.ShapeDtypeStruct((TOP_K, T), jnp.int32),
            jax.ShapeDtypeStruct((E, LANES), F32),
        ),
        grid=(B, nj),
        in_specs=[
            pl.BlockSpec((1, ts, D), lambda b, j: (b, j, 0)),
            const(1, D), const(D, D),
            pl.BlockSpec((1, M, D), lambda b, j: (b, 0, 0)),
            pl.BlockSpec((1, M, D), lambda b, j: (b, 0, 0)),
            const(D, D), const(1, D), const(E, D), const(E, 1),
        ],
        out_specs=(
            pl.BlockSpec((1, ts, D), lambda b, j: (b, j, 0)),
            pl.BlockSpec((ts * ROW_TILE, LANES), lambda b, j: (b * nj + j, 0)),
            tok(TOP_K), tok(TOP_K), tok(TOP_K),
            const(E, LANES),
        ),
        scratch_shapes=[pltpu.VMEM((ts, D), BF16), pltpu.VMEM((E, LANES), F32)],
        compiler_params=pltpu.CompilerParams(
            dimension_semantics=("arbitrary", "arbitrary"), vmem_limit_bytes=VMEM_LIMIT),
        name="xattn_router",
    )(x, g_xattn.reshape(1, D), w_q.astype(BF16), k, v, w_xo.astype(BF16), g_moe.reshape(1, D),
      w_router.T, b_router.reshape(E, 1))


def _dispatch_kernel(dest_ref, cnt_ref, pend_ref, xn_ref, xs_ref, zero_scr, sem, zsem, *, n_experts):
    tt = xn_ref.shape[0] // ROW_TILE
    R = zero_scr.shape[0] // ROW_TILE

    @pl.when(pl.program_id(0) == 0)
    def _():
        zero_scr[...] = jnp.zeros_like(zero_scr)

        def fill_copy(e):
            start = pl.multiple_of((pend_ref[e] - R) * ROW_TILE, R * ROW_TILE)
            return pltpu.make_async_copy(zero_scr, xs_ref.at[pl.ds(start, R * ROW_TILE)], zsem)

        def start_fill(e, carry):
            @pl.when(cnt_ref[e] > 0)
            def _():
                fill_copy(e).start()
            return carry

        def wait_fill(e, carry):
            @pl.when(cnt_ref[e] > 0)
            def _():
                fill_copy(e).wait()
            return carry

        lax.fori_loop(0, n_experts, start_fill, 0)
        lax.fori_loop(0, n_experts, wait_fill, 0)

    def issue(grp, carry):
        t0 = pl.multiple_of(grp * ROW_UNROLL, ROW_UNROLL)
        for kk in range(TOP_K):
            for u in range(ROW_UNROLL):
                dst = pl.multiple_of(dest_ref[0, 0, kk * tt + t0 + u] * ROW_TILE, ROW_TILE)
                row = pl.multiple_of((t0 + u) * ROW_TILE, ROW_TILE)
                pltpu.make_async_copy(xn_ref.at[pl.ds(row, ROW_TILE)], xs_ref.at[pl.ds(dst, ROW_TILE)],
                                      sem).start(priority=u % 2)
        return carry

    lax.fori_loop(0, tt // ROW_UNROLL, issue, 0)
    for kk in range(TOP_K):
        pltpu.make_async_copy(xn_ref, xs_ref.at[pl.ds(0, tt * ROW_TILE)], sem).wait()


def _dispatch(xn, dest, counts, pad_end, n_slots):
    T = dest.shape[1]
    tt = min(DISPATCH_TOKENS, T)
    assert T % tt == 0
    E = counts.shape[0]
    dest = dest.reshape(TOP_K, T // tt, tt).transpose(1, 0, 2).reshape(T // tt, 1, TOP_K * tt)
    return pl.pallas_call(
        functools.partial(_dispatch_kernel, n_experts=E),
        out_shape=jax.ShapeDtypeStruct((n_slots * ROW_TILE, LANES), F32),
        grid=(T // tt,),
        in_specs=[
            pl.BlockSpec((1, 1, TOP_K * tt), lambda i: (i, 0, 0), memory_space=pltpu.SMEM),
            pl.BlockSpec(memory_space=pltpu.SMEM),
            pl.BlockSpec(memory_space=pltpu.SMEM),
            pl.BlockSpec((tt * ROW_TILE, LANES), lambda i: (i, 0)),
        ],
        out_specs=pl.BlockSpec(memory_space=pl.ANY),
        scratch_shapes=[pltpu.VMEM((MOE_ROWS * ROW_TILE, LANES), F32), pltpu.SemaphoreType.DMA,
                        pltpu.SemaphoreType.DMA],
        compiler_params=pltpu.CompilerParams(
            dimension_semantics=("arbitrary",), vmem_limit_bytes=VMEM_LIMIT, has_side_effects=True),
        name="dispatch",
    )(dest, counts, pad_end, xn)


def _expert_kernel(be_ref, nu_ref, reg_ref, nxt_ref, xs_ref, wg_ref, bg_ref, wu_ref, bu_ref, wd_ref, bd_ref, y_ref,
                   wf_scr, wg_scr, wu_scr, wd_scr, sem):
    j = pl.program_id(0)
    active = j < nu_ref[0]
    w_hbm = (wg_ref, wu_ref, wd_ref)

    def fetch(e, slot):
        return [pltpu.make_async_copy(w_hbm[i].at[e], wf_scr.at[slot, i], sem.at[slot, i]) for i in range(3)]

    @pl.when(active & ((j == 0) | (be_ref[j] != be_ref[jnp.maximum(j - 1, 0)])))
    def _():
        slot = reg_ref[j] % 2

        @pl.when(j == 0)
        def _():
            for cp in fetch(be_ref[0], 0):
                cp.start()

        for cp in fetch(be_ref[j], slot):
            cp.wait()
        wg_scr[...] = wf_scr[slot, 0].astype(BF16)
        wu_scr[...] = wf_scr[slot, 1].astype(BF16)
        wd_scr[...] = wf_scr[slot, 2].astype(BF16)

        @pl.when(nxt_ref[j] >= 0)
        def _():
            for cp in fetch(nxt_ref[j], 1 - slot):
                cp.start()

    @pl.when(active)
    def _():
        xb = _load_row_tiles(xs_ref, xs_ref.shape[0] // ROW_TILE).astype(BF16)
        g = jnp.minimum(_dot(xb, wg_scr[...]) + bg_ref[0], SWIGLU_LIMIT)
        u = jnp.clip(_dot(xb, wu_scr[...]) + bu_ref[0], -SWIGLU_LIMIT, SWIGLU_LIMIT)
        hdn = (u + 1.0) * (g * jax.nn.sigmoid(SWIGLU_ALPHA * g))
        _store_row_tiles(y_ref, _dot(hdn.astype(BF16), wd_scr[...]) + bd_ref[0])


def _experts(xs, block_e, n_used, region, next_e, w_gate, b_gate, w_up, b_up, w_down, b_down):
    E, D, F = w_gate.shape
    assert D == ROW_TILE * LANES and F == D, "the two weight staging slots hold (D, F) and (F, D) alike"
    n_slots = xs.shape[0] // ROW_TILE
    R = MOE_ROWS
    n_blocks = n_slots // R
    blk = lambda j, be, nu, rg, nx: (jnp.minimum(j, nu[0] - 1), 0)
    wsel = lambda j, be, nu, rg, nx: (be[jnp.minimum(j, nu[0] - 1)], 0, 0)
    hbm = pl.BlockSpec(memory_space=pl.ANY)
    grid_spec = pltpu.PrefetchScalarGridSpec(
        num_scalar_prefetch=4,
        grid=(n_blocks,),
        in_specs=[
            pl.BlockSpec((R * ROW_TILE, LANES), blk),
            hbm, pl.BlockSpec((1, 1, F), wsel),
            hbm, pl.BlockSpec((1, 1, F), wsel),
            hbm, pl.BlockSpec((1, 1, D), wsel),
        ],
        out_specs=pl.BlockSpec((R * ROW_TILE, LANES), blk),
        scratch_shapes=[pltpu.VMEM((2, 3, D, F), F32), pltpu.VMEM((D, F), BF16), pltpu.VMEM((D, F), BF16),
                        pltpu.VMEM((F, D), BF16), pltpu.SemaphoreType.DMA((2, 3))],
    )
    return pl.pallas_call(
        _expert_kernel,
        out_shape=jax.ShapeDtypeStruct((n_slots * ROW_TILE, LANES), F32),
        grid_spec=grid_spec,
        compiler_params=pltpu.CompilerParams(dimension_semantics=("arbitrary",), vmem_limit_bytes=VMEM_LIMIT),
        name="experts",
    )(block_e, n_used, region, next_e, xs, w_gate, b_gate.reshape(E, 1, F), w_up, b_up.reshape(E, 1, F),
      w_down, b_down.reshape(E, 1, D))


def _combine_kernel(dcur_ref, dnext_ref, x_ref, gate_ref, gfin_ref, ys_ref, o_ref, buf, sem):
    tc = x_ref.shape[0]
    i = pl.program_id(0)
    slot = i % 2

    def issue_all(d_ref, into):
        def issue(grp, carry):
            t0 = pl.multiple_of(grp * ROW_UNROLL, ROW_UNROLL)
            for kk in range(TOP_K):
                for u in range(ROW_UNROLL):
                    src = pl.multiple_of(d_ref[kk, t0 + u] * ROW_TILE, ROW_TILE)
                    row = pl.multiple_of((t0 + u) * ROW_TILE, ROW_TILE)
                    pltpu.make_async_copy(ys_ref.at[pl.ds(src, ROW_TILE)], buf.at[into, kk, pl.ds(row, ROW_TILE)],
                                          sem.at[into]).start(priority=u % 2)
            return carry

        lax.fori_loop(0, tc // ROW_UNROLL, issue, 0)

    def wait_slot(s):
        for kk in range(TOP_K):
            pltpu.make_async_copy(ys_ref.at[pl.ds(0, tc * ROW_TILE)], buf.at[s, kk], sem.at[s]).wait()

    @pl.when(i == 0)
    def _():
        issue_all(dcur_ref, 0)

    wait_slot(slot)
    other = 1 - slot
    for t in range(tc):
        for kk in range(TOP_K):
            src = pl.multiple_of(dnext_ref[kk, t] * ROW_TILE, ROW_TILE)
            pltpu.make_async_copy(ys_ref.at[pl.ds(src, ROW_TILE)], buf.at[other, kk, pl.ds(t * ROW_TILE, ROW_TILE)],
                                  sem.at[other]).start(priority=t % 2)
    for r0 in range(0, tc, COMBINE_ROWS):
        rows = slice(r0, r0 + COMBINE_ROWS)
        acc = x_ref[rows, :]
        for kk in range(TOP_K):
            acc = acc + gate_ref[rows, kk:kk + 1] * _load_row_tiles(buf.at[slot, kk], COMBINE_ROWS, r0)
        o_ref[rows, :] = _rms(acc, gfin_ref[...])

    @pl.when(i == pl.num_programs(0) - 1)
    def _():
        wait_slot(other)


def _combine(x2, gate_t, dest, ys, g_final):
    T, D = x2.shape
    tc = min(COMBINE_TOKENS, T)
    assert T % tc == 0
    nb = T // tc
    return pl.pallas_call(
        _combine_kernel,
        out_shape=jax.ShapeDtypeStruct((T, D), F32),
        grid=(nb,),
        in_specs=[
            pl.BlockSpec((TOP_K, tc), lambda i: (0, i), memory_space=pltpu.SMEM),
            pl.BlockSpec((TOP_K, tc), lambda i: (0, jnp.minimum(i + 1, nb - 1)), memory_space=pltpu.SMEM),
            pl.BlockSpec((tc, D), lambda i: (i, 0)),
            pl.BlockSpec((tc, TOP_K), lambda i: (i, 0)),
            pl.BlockSpec((1, D), lambda i: (0, 0)),
            pl.BlockSpec(memory_space=pl.ANY),
        ],
        out_specs=pl.BlockSpec((tc, D), lambda i: (i, 0)),
        scratch_shapes=[pltpu.VMEM((2, TOP_K, tc * ROW_TILE, LANES), F32), pltpu.SemaphoreType.DMA((2,))],
        compiler_params=pltpu.CompilerParams(dimension_semantics=("arbitrary",), vmem_limit_bytes=VMEM_LIMIT),
        name="combine",
    )(dest, dest, x2, gate_t, g_final.reshape(1, D), ys)


def _moe(x2, xn, idx, gate, rank, cnt, w_gate, b_gate, w_up, b_up, w_down, b_down, g_final):
    T = x2.shape[0]
    E = w_gate.shape[0]
    R = MOE_ROWS
    n_blocks = -(-T * TOP_K // R) + E
    n_slots = n_blocks * R
    counts = cnt[:, 0].astype(jnp.int32)
    padded = (counts + R - 1) // R * R
    pad_end = jnp.cumsum(padded)
    pad_start = pad_end - padded
    dest = rank
    for e in range(E):
        dest = dest + jnp.where(idx == e, pad_start[e], 0)
    n_used = (pad_end[-1:] // R).astype(jnp.int32)
    block_e = jnp.minimum(jnp.sum(jnp.arange(n_blocks)[:, None] * R >= pad_end[None, :], axis=1), E - 1)
    block_e = block_e.astype(jnp.int32)
    blocks = jnp.arange(n_blocks)
    first = (blocks < n_used[0]) & ((blocks == 0) | (block_e != jnp.roll(block_e, 1)))
    region = (jnp.cumsum(first) - 1).astype(jnp.int32)
    later = jnp.where((counts[None, :] > 0) & (jnp.arange(E)[None, :] > block_e[:, None]), jnp.arange(E)[None, :], E)
    next_e = jnp.min(later, axis=1)
    next_e = jnp.where(next_e == E, -1, next_e).astype(jnp.int32)

    xs = _dispatch(xn, dest, counts, pad_end.astype(jnp.int32), n_slots)
    ys = _experts(xs, block_e, n_used, region, next_e, w_gate, b_gate, w_up, b_up, w_down, b_down)
    return _combine(x2, gate.T, dest, ys, g_final)


def kernel(x, mem, g_mix, w_in, b_gates, g_mlstm_head, w_spatial, b_spatial, g_gmlp_v, g_gmlp_out, w_out,
           g_xattn, g_mem, w_q, w_kv, w_xo, g_moe, w_router, b_router, w_gate, b_gate, w_up, b_up,
           w_down, b_down, g_final):
    B, S, D = x.shape
    assert g_mix.shape[0] == 1, "the combine kernel fuses the closing norm, so exactly one layer is supported"
    x1 = _mixer(x, g_mix[0], w_in[0], b_gates[0], g_mlstm_head[0], w_spatial[0], b_spatial[0],
                g_gmlp_v[0], g_gmlp_out[0], w_out[0])
    k, v = _kv_proj(mem, g_mem[0], w_kv[0])
    x2, xn, idx, gate, rank, cnt = _xattn_router(x1, k, v, g_xattn[0], w_q[0], w_xo[0], g_moe[0],
                                                 w_router[0], b_router[0])
    out = _moe(x2.reshape(B * S, D), xn, idx, gate, rank, cnt, w_gate[0], b_gate[0], w_up[0], b_up[0],
               w_down[0], b_down[0], g_final)
    return out.reshape(B, S, D)
```

```python
import functools

import jax
import jax.numpy as jnp
from jax import lax
from jax.experimental import pallas as pl
from jax.experimental.pallas import tpu as pltpu

F32 = jnp.float32
BF16 = jnp.bfloat16

MLSTM_HEADS = 4
MLSTM_DK = 64
MLSTM_DV = 128
CHUNK = 128
GMLP_GROUPS = 4
GMLP_DG = 128
XATTN_HEADS = 4
TOP_K = 4
GATE_SOFTCAP = 15.0
SWIGLU_LIMIT = 7.0
SWIGLU_ALPHA = 1.702
NORM_EPS = 1e-6

LANES = 128
ROW_TILE = 8

MIX_TOKENS = 512
PROJ_PIECE = 512
XATTN_TOKENS = 512
MOE_ROWS = 512
FFN_PIECE = 256
DISPATCH_TOKENS = 2048
COMBINE_TOKENS = 256
COMBINE_ROWS = 32
ROW_UNROLL = 16
VMEM_LIMIT = 52 * 1024 * 1024


def _rms(x, g):
    return x * lax.rsqrt(jnp.mean(x * x, axis=-1, keepdims=True) + NORM_EPS) * g


def _gelu_tanh(x):
    return 0.5 * x * (1.0 + jnp.tanh(0.7978845608028654 * (x + 0.044715 * x * x * x)))


def _dot(a, b):
    return jnp.dot(a, b, preferred_element_type=F32)


def _dot_nt(a, b):
    return lax.dot_general(a, b, (((1,), (1,)), ((), ())), preferred_element_type=F32)


def _load_row_tiles(ref, rows, first=0):
    return jnp.concatenate(
        [ref[pl.ds(first * ROW_TILE + j, rows, stride=ROW_TILE), :] for j in range(ROW_TILE)], axis=1)


def _store_row_tiles(ref, val):
    for j in range(ROW_TILE):
        ref[pl.ds(j, val.shape[0], stride=ROW_TILE), :] = val[:, j * LANES:(j + 1) * LANES]


def _split3(x):
    hi = x.astype(BF16)
    r1 = x - hi.astype(F32)
    mid = r1.astype(BF16)
    lo = (r1 - mid.astype(F32)).astype(BF16)
    return hi, mid, lo


def _mixer_kernel(x_ref, gmix_ref, win_ref, bg_ref, ghead_ref, wsp_ref, bsp_ref, gzv_ref, gzo_ref, wout_ref,
                  o_ref, xnb_scr, proj_scr, y_scr, ct_scr, m_scr, *, n_chunks, cols):
    qk0, v0, og0, gu0, gv0, gt0 = cols
    H, G, L = MLSTM_HEADS, GMLP_GROUPS, CHUNK

    @pl.when(pl.program_id(1) == 0)
    def _():
        ct_scr[...] = jnp.zeros_like(ct_scr)
        m_scr[...] = jnp.zeros_like(m_scr)

    row = lax.broadcasted_iota(jnp.int32, (L, L), 0)
    col = lax.broadcasted_iota(jnp.int32, (L, L), 1)
    causal = row >= col
    tril = jnp.where(causal, 1.0, 0.0).astype(BF16)
    triu = jnp.where(row <= col, 1.0, 0.0).astype(BF16)
    lane = lax.broadcasted_iota(jnp.int32, (L, LANES), 1)
    ones_col = jnp.where(lane == 0, 1.0, 0.0).astype(F32)

    xnb_scr[...] = _rms(x_ref[0], gmix_ref[...]).astype(BF16)

    def in_pieces(c):
        r = slice(c * L, (c + 1) * L)
        for p in range(0, proj_scr.shape[1], PROJ_PIECE):
            cs = slice(p, min(p + PROJ_PIECE, proj_scr.shape[1]))
            yield lambda cs=cs: proj_scr.__setitem__((r, cs), _dot(xnb_scr[r, :], win_ref[:, cs]))

    def out_pieces(c):
        r = slice(c * L, (c + 1) * L)
        for p in range(0, o_ref.shape[2], PROJ_PIECE):
            cs = slice(p, p + PROJ_PIECE)
            yield lambda cs=cs: o_ref.__setitem__((0, r, cs), x_ref[0, r, cs] + _dot(y_scr[r, :], wout_ref[:, cs]))

    fillers = []

    def tick(n=1):
        for _ in range(min(n, len(fillers))):
            fillers.pop(0)()

    for piece in in_pieces(0):
        piece()

    for c in range(n_chunks):
        rows = slice(c * L, (c + 1) * L)
        if c + 1 < n_chunks:
            fillers.extend(in_pieces(c + 1))
        if c > 0:
            fillers.extend(out_pieces(c - 1))
        pre = proj_scr[rows, gt0:gt0 + LANES] + bg_ref[...]
        capped = GATE_SOFTCAP * jnp.tanh(pre * (1.0 / GATE_SOFTCAP))
        log_sig = jnp.minimum(capped, 0.0) - jnp.log1p(jnp.exp(-jnp.abs(capped)))
        lg = jnp.where(lane < H, capped, log_sig)
        lg_t = lg.T[0:ROW_TILE, :]
        bcol = sum(_dot(tril, p) for p in _split3(lg))
        brow = sum(_dot(p, triu) for p in _split3(lg_t))

        hs = range(H)
        b_c = [bcol[:, H + h:H + h + 1] for h in hs]
        i_c = [lg[:, h:h + 1] for h in hs]
        b_r = [brow[H + h:H + h + 1, :] for h in hs]
        i_r = [lg_t[h:h + 1, :] for h in hs]
        m_prev = [m_scr[h, 0:1, 0:1] for h in hs]
        qk = [proj_scr[rows, qk0 + h * LANES:qk0 + (h + 1) * LANES] for h in hs]
        q = [jnp.where(lane < MLSTM_DK, qk[h] * (MLSTM_DK ** -0.5), 0.0) for h in hs]
        k = [jnp.where(lane < MLSTM_DK, pltpu.roll(qk[h], MLSTM_DK, axis=1), 0.0) for h in hs]
        qb = [q[h].astype(BF16) for h in hs]
        kb = [k[h].astype(BF16) for h in hs]
        tick()
        vaug = [jnp.concatenate([proj_scr[rows, v0 + h * MLSTM_DV:v0 + (h + 1) * MLSTM_DV], ones_col], axis=1)
                for h in hs]

        d = [jnp.where(causal, b_c[h] - b_r[h] + i_r[h], -jnp.inf) for h in hs]
        tick()
        inter = [b_c[h] + m_prev[h] for h in hs]
        m_t = [jnp.maximum(inter[h], jnp.max(d[h], axis=1, keepdims=True)) for h in hs]
        tick()
        w_intra = [jnp.exp(d[h] - m_t[h]) for h in hs]
        tick()
        w_inter = [jnp.exp(inter[h] - m_t[h]) for h in hs]
        s = [_dot_nt(qb[h], kb[h]) * w_intra[h] for h in hs]
        tick()
        ct = [ct_scr[h] for h in hs]
        na = [_dot(s[h].astype(BF16), vaug[h].astype(BF16)) + w_inter[h] * _dot(qb[h], ct[h].astype(BF16))
              for h in hs]
        hh = [na[h][:, :MLSTM_DV] / jnp.maximum(jnp.abs(na[h][:, MLSTM_DV:MLSTM_DV + 1]), jnp.exp(-m_t[h]))
              for h in hs]

        b_last = [b_c[h][L - 1:L, :] for h in hs]
        g_c = [b_last[h] - b_c[h] + i_c[h] for h in hs]
        m_new = [jnp.maximum(b_last[h] + m_prev[h], jnp.max(g_c[h], axis=0, keepdims=True)) for h in hs]
        wk = [jnp.exp(g_c[h] - m_new[h]) for h in hs]
        decay = [jnp.exp(b_last[h] + m_prev[h] - m_new[h]) for h in hs]
        tick()
        for h in hs:
            ct_scr[h] = decay[h] * ct[h] + _dot(k[h].T.astype(BF16), (wk[h] * vaug[h]).astype(BF16))
            m_scr[h] = jnp.broadcast_to(m_new[h], m_scr.shape[1:])

        hn = [_rms(hh[h], ghead_ref[:, h * MLSTM_DV:(h + 1) * MLSTM_DV]) for h in hs]
        tick()
        for h in hs:
            og = proj_scr[rows, og0 + h * MLSTM_DV:og0 + (h + 1) * MLSTM_DV]
            y_scr[rows, h * MLSTM_DV:(h + 1) * MLSTM_DV] = (jax.nn.sigmoid(og) * hn[h]).astype(BF16)

        gs = range(G)
        sl = [slice(g * GMLP_DG, (g + 1) * GMLP_DG) for g in gs]
        z = [_rms(_gelu_tanh(proj_scr[rows, gv0 + g * GMLP_DG:gv0 + (g + 1) * GMLP_DG]), gzv_ref[:, sl[g]]) for g in gs]
        mixed = [_dot(jnp.where(causal, wsp_ref[g], 0.0).astype(BF16), z[g].astype(BF16)) + bsp_ref[g] for g in gs]
        u = [_gelu_tanh(proj_scr[rows, gu0 + g * GMLP_DG:gu0 + (g + 1) * GMLP_DG]) for g in gs]
        tick()
        yg = [_rms(u[g] * mixed[g], gzo_ref[:, sl[g]]) for g in gs]
        tick()
        for g in gs:
            y_scr[rows, H * MLSTM_DV + g * GMLP_DG:H * MLSTM_DV + (g + 1) * GMLP_DG] = yg[g].astype(BF16)
        tick(len(fillers))
    for piece in out_pieces(n_chunks - 1):
        piece()


def _mixer(x, g_mix, w_in, b_gates, g_head, w_spatial, b_spatial, g_zv, g_zo, w_out):
    B, S, D = x.shape
    H, G, L = MLSTM_HEADS, GMLP_GROUPS, CHUNK
    qw, vw, gw = H * MLSTM_DK, H * MLSTM_DV, G * GMLP_DG
    ts = min(MIX_TOKENS, S)
    assert S % ts == 0 and ts % L == 0

    assert 2 * MLSTM_DK == LANES
    c = 0
    w_q = w_in[:, c:c + qw].reshape(D, H, MLSTM_DK); c += qw
    w_k = w_in[:, c:c + qw].reshape(D, H, MLSTM_DK); c += qw
    w_qk = jnp.concatenate([w_q, w_k], axis=2).reshape(D, H * LANES)
    w_v = w_in[:, c:c + vw]; c += vw
    w_o = w_in[:, c:c + vw]; c += vw
    w_g = jnp.pad(w_in[:, c:c + 2 * H], ((0, 0), (0, LANES - 2 * H))); c += 2 * H
    w_gu = w_in[:, c:c + gw]; c += gw
    w_gv = w_in[:, c:c + gw]; c += gw
    w_pack = jnp.concatenate([w_qk, w_v, w_o, w_gu, w_gv, w_g], axis=1).astype(BF16)
    hp = H * LANES
    cols = (0, hp, hp + vw, hp + 2 * vw, hp + 2 * vw + gw, hp + 2 * vw + 2 * gw)
    n_cols = w_pack.shape[1]
    bg = jnp.pad(b_gates, (0, LANES - 2 * H)).reshape(1, LANES)
    bsp = jnp.broadcast_to(b_spatial[:, :, None], (G, L, GMLP_DG))

    const = lambda *shape: pl.BlockSpec(shape, lambda b, j: (0,) * len(shape))
    return pl.pallas_call(
        functools.partial(_mixer_kernel, n_chunks=ts // L, cols=cols),
        out_shape=jax.ShapeDtypeStruct((B, S, D), F32),
        grid=(B, S // ts),
        in_specs=[
            pl.BlockSpec((1, ts, D), lambda b, j: (b, j, 0)),
            const(1, D), const(D, n_cols), const(1, LANES), const(1, vw),
            const(G, L, L), const(G, L, GMLP_DG), const(1, gw), const(1, gw), const(vw + gw, D),
        ],
        out_specs=pl.BlockSpec((1, ts, D), lambda b, j: (b, j, 0)),
        scratch_shapes=[
            pltpu.VMEM((ts, D), BF16),
            pltpu.VMEM((ts, n_cols), F32),
            pltpu.VMEM((ts, vw + gw), BF16),
            pltpu.VMEM((H, LANES, 2 * MLSTM_DV), F32),
            pltpu.VMEM((H, ROW_TILE, LANES), F32),
        ],
        compiler_params=pltpu.CompilerParams(
            dimension_semantics=("arbitrary", "arbitrary"), vmem_limit_bytes=VMEM_LIMIT),
        name="mixer",
    )(x, g_mix.reshape(1, D), w_pack, bg, g_head.reshape(1, vw), w_spatial, bsp,
      g_zv.reshape(1, gw), g_zo.reshape(1, gw), w_out.astype(BF16))


def _kv_kernel(mem_ref, g_ref, w_ref, k_ref, v_ref):
    D = mem_ref.shape[-1]
    kv = _dot(_rms(mem_ref[0], g_ref[...]).astype(BF16), w_ref[...])
    k_ref[0] = kv[:, :D].astype(BF16)
    v_ref[0] = kv[:, D:].astype(BF16)


def _kv_proj(mem, g_mem, w_kv):
    B, M, D = mem.shape
    return pl.pallas_call(
        _kv_kernel,
        out_shape=(jax.ShapeDtypeStruct((B, M, D), BF16), jax.ShapeDtypeStruct((B, M, D), BF16)),
        grid=(B,),
        in_specs=[
            pl.BlockSpec((1, M, D), lambda b: (b, 0, 0)),
            pl.BlockSpec((1, D), lambda b: (0, 0)),
            pl.BlockSpec((D, 2 * D), lambda b: (0, 0)),
        ],
        out_specs=(pl.BlockSpec((1, M, D), lambda b: (b, 0, 0)), pl.BlockSpec((1, M, D), lambda b: (b, 0, 0))),
        compiler_params=pltpu.CompilerParams(dimension_semantics=("arbitrary",), vmem_limit_bytes=VMEM_LIMIT),
        name="kv_proj",
    )(mem, g_mem.reshape(1, D), w_kv.astype(BF16))


def _xattn_kernel(x_ref, gx_ref, wq_ref, k_ref, v_ref, wxo_ref, gmoe_ref, wr_ref, br_ref,
                  x2_ref, xn_ref, idx_ref, gate_ref, rank_ref, cnt_ref, xnb_scr, q_scr, o_scr, cnt_scr):
    ts, D = x_ref.shape[1], x_ref.shape[2]
    hd = D // XATTN_HEADS
    E = wr_ref.shape[0]

    @pl.when((pl.program_id(0) == 0) & (pl.program_id(1) == 0))
    def _():
        cnt_scr[...] = jnp.zeros_like(cnt_scr)

    xnb_scr[...] = _rms(x_ref[0], gx_ref[...]).astype(BF16)

    half = ts // 2
    halves = [slice(0, half), slice(half, ts)]
    pieces = [slice(p, p + PROJ_PIECE) for p in range(0, D, PROJ_PIECE)]

    def q_pieces(r):
        for cs in pieces:
            yield lambda cs=cs: q_scr.__setitem__(
                (r, cs), (_dot(xnb_scr[r, :], wq_ref[:, cs]) * (hd ** -0.5)).astype(BF16))

    def xo_pieces(r):
        for cs in pieces:
            yield lambda cs=cs: x2_ref.__setitem__(
                (0, r, cs), x_ref[0, r, cs] + _dot(o_scr[r, :], wxo_ref[:, cs]))

    fillers = []

    def tick(n=1):
        for _ in range(min(n, len(fillers))):
            fillers.pop(0)()

    def attention(r):
        hs = range(XATTN_HEADS)
        sl = [slice(h * hd, (h + 1) * hd) for h in hs]
        s = [_dot_nt(q_scr[r, sl[h]], k_ref[0, :, sl[h]]) for h in hs]
        tick()
        p = [jnp.exp(s[h] - jnp.max(s[h], axis=1, keepdims=True)) for h in hs]
        tick()
        p = [p[h] / jnp.sum(p[h], axis=1, keepdims=True) for h in hs]
        for h in hs:
            o_scr[r, sl[h]] = _dot(p[h].astype(BF16), v_ref[0, :, sl[h]]).astype(BF16)
        tick(len(fillers))

    for piece in q_pieces(halves[0]):
        piece()
    fillers.extend(q_pieces(halves[1]))
    attention(halves[0])
    fillers.extend(xo_pieces(halves[0]))
    attention(halves[1])
    for piece in xo_pieces(halves[1]):
        piece()
    x2 = x2_ref[0]

    xn = _rms(x2, gmoe_ref[...])
    xh, xm, _ = _split3(xn)
    wh, wm, _ = _split3(wr_ref[...])
    logits = _dot_nt(wh, xh) + (_dot_nt(wh, xm) + _dot_nt(wm, xh)) + br_ref[...]

    e_iota = lax.broadcasted_iota(jnp.int32, (E, ts), 0)
    work = logits
    tops, idxs, hots = [], [], []
    for _ in range(TOP_K):
        m = jnp.max(work, axis=0, keepdims=True)
        i = jnp.min(jnp.where(work == m, e_iota, E), axis=0, keepdims=True)
        hot = e_iota == i
        work = jnp.where(hot, -jnp.inf, work)
        tops.append(m); idxs.append(i); hots.append(hot)
    ex = [jnp.exp(t - tops[0]) for t in tops]
    tot = ex[0] + ex[1] + ex[2] + ex[3]
    gate_ref[...] = jnp.concatenate([e / tot for e in ex], axis=0)
    idx_ref[...] = jnp.concatenate(idxs, axis=0)

    hot_all = jnp.where(hots[0] | hots[1] | hots[2] | hots[3], 1.0, 0.0)
    r = lax.broadcasted_iota(jnp.int32, (ts, ts), 0)
    c = lax.broadcasted_iota(jnp.int32, (ts, ts), 1)
    before = jnp.where(r < c, 1.0, 0.0).astype(BF16)
    base = cnt_scr[:, 0:1] + _dot(hot_all.astype(BF16), before)
    ranks = [jnp.sum(jnp.where(hot, base, 0.0), axis=0, keepdims=True) for hot in hots]
    rank_ref[...] = jnp.concatenate(ranks, axis=0).astype(jnp.int32)
    cnt_new = cnt_scr[...] + jnp.sum(hot_all, axis=1, keepdims=True)
    cnt_scr[...] = cnt_new
    cnt_ref[...] = cnt_new
    _store_row_tiles(xn_ref, xn)


def _xattn_router(x, k, v, g_xattn, w_q, w_xo, g_moe, w_router, b_router):
    B, S, D = x.shape
    M = k.shape[1]
    E = w_router.shape[1]
    ts = min(XATTN_TOKENS, S)
    assert S % ts == 0
    nj = S // ts
    T = B * S
    const = lambda *shape: pl.BlockSpec(shape, lambda b, j: (0,) * len(shape))
    tok = lambda rows: pl.BlockSpec((rows, ts), lambda b, j: (0, b * nj + j))
    return pl.pallas_call(
        _xattn_kernel,
        out_shape=(
            jax.ShapeDtypeStruct((B, S, D), F32),
            jax.ShapeDtypeStruct((T * ROW_TILE, LANES), F32),
            jax.ShapeDtypeStruct((TOP_K, T), jnp.int32),
            jax.ShapeDtypeStruct((TOP_K, T), F32),
            jax.ShapeDtypeStruct((TOP_K, T), jnp.int32),
            jax.ShapeDtypeStruct((E, LANES), F32),
        ),
        grid=(B, nj),
        in_specs=[
            pl.BlockSpec((1, ts, D), lambda b, j: (b, j, 0)),
            const(1, D), const(D, D),
            pl.BlockSpec((1, M, D), lambda b, j: (b, 0, 0)),
            pl.BlockSpec((1, M, D), lambda b, j: (b, 0, 0)),
            const(D, D), const(1, D), const(E, D), const(E, 1),
        ],
        out_specs=(
            pl.BlockSpec((1, ts, D), lambda b, j: (b, j, 0)),
            pl.BlockSpec((ts * ROW_TILE, LANES), lambda b, j: (b * nj + j, 0)),
            tok(TOP_K), tok(TOP_K), tok(TOP_K),
            const(E, LANES),
        ),
        scratch_shapes=[pltpu.VMEM((ts, D), BF16), pltpu.VMEM((ts, D), BF16), pltpu.VMEM((ts, D), BF16),
                        pltpu.VMEM((E, LANES), F32)],
        compiler_params=pltpu.CompilerParams(
            dimension_semantics=("arbitrary", "arbitrary"), vmem_limit_bytes=VMEM_LIMIT),
        name="xattn_router",
    )(x, g_xattn.reshape(1, D), w_q.astype(BF16), k, v, w_xo.astype(BF16), g_moe.reshape(1, D),
      w_router.T, b_router.reshape(E, 1))


def _dispatch_kernel(dest_ref, cnt_ref, pend_ref, xn_ref, xs_ref, zero_scr, sem, zsem, *, n_experts):
    tt = xn_ref.shape[0] // ROW_TILE
    R = zero_scr.shape[0] // ROW_TILE

    @pl.when(pl.program_id(0) == 0)
    def _():
        zero_scr[...] = jnp.zeros_like(zero_scr)

        def fill_copy(e):
            start = pl.multiple_of((pend_ref[e] - R) * ROW_TILE, R * ROW_TILE)
            return pltpu.make_async_copy(zero_scr, xs_ref.at[pl.ds(start, R * ROW_TILE)], zsem)

        def start_fill(e, carry):
            @pl.when(cnt_ref[e] > 0)
            def _():
                fill_copy(e).start()
            return carry

        def wait_fill(e, carry):
            @pl.when(cnt_ref[e] > 0)
            def _():
                fill_copy(e).wait()
            return carry

        lax.fori_loop(0, n_experts, start_fill, 0)
        lax.fori_loop(0, n_experts, wait_fill, 0)

    def issue(grp, carry):
        t0 = pl.multiple_of(grp * ROW_UNROLL, ROW_UNROLL)
        for kk in range(TOP_K):
            for u in range(ROW_UNROLL):
                dst = pl.multiple_of(dest_ref[0, 0, kk * tt + t0 + u] * ROW_TILE, ROW_TILE)
                row = pl.multiple_of((t0 + u) * ROW_TILE, ROW_TILE)
                pltpu.make_async_copy(xn_ref.at[pl.ds(row, ROW_TILE)], xs_ref.at[pl.ds(dst, ROW_TILE)],
                                      sem).start(priority=u % 2)
        return carry

    lax.fori_loop(0, tt // ROW_UNROLL, issue, 0)
    for kk in range(TOP_K):
        pltpu.make_async_copy(xn_ref, xs_ref.at[pl.ds(0, tt * ROW_TILE)], sem).wait()


def _dispatch(xn, dest, counts, pad_end, n_slots):
    T = dest.shape[1]
    tt = min(DISPATCH_TOKENS, T)
    assert T % tt == 0
    E = counts.shape[0]
    dest = dest.reshape(TOP_K, T // tt, tt).transpose(1, 0, 2).reshape(T // tt, 1, TOP_K * tt)
    return pl.pallas_call(
        functools.partial(_dispatch_kernel, n_experts=E),
        out_shape=jax.ShapeDtypeStruct((n_slots * ROW_TILE, LANES), F32),
        grid=(T // tt,),
        in_specs=[
            pl.BlockSpec((1, 1, TOP_K * tt), lambda i: (i, 0, 0), memory_space=pltpu.SMEM),
            pl.BlockSpec(memory_space=pltpu.SMEM),
            pl.BlockSpec(memory_space=pltpu.SMEM),
            pl.BlockSpec((tt * ROW_TILE, LANES), lambda i: (i, 0)),
        ],
        out_specs=pl.BlockSpec(memory_space=pl.ANY),
        scratch_shapes=[pltpu.VMEM((MOE_ROWS * ROW_TILE, LANES), F32), pltpu.SemaphoreType.DMA,
                        pltpu.SemaphoreType.DMA],
        compiler_params=pltpu.CompilerParams(
            dimension_semantics=("arbitrary",), vmem_limit_bytes=VMEM_LIMIT, has_side_effects=True),
        name="dispatch",
    )(dest, counts, pad_end, xn)


def _expert_kernel(be_ref, nu_ref, reg_ref, nxt_ref, xs_ref, wg_ref, bg_ref, wu_ref, bu_ref, wd_ref, bd_ref, y_ref,
                   wf_scr, wg_scr, wu_scr, wd_scr, h_scr, sem):
    j = pl.program_id(0)
    active = j < nu_ref[0]
    w_hbm = (wg_ref, wu_ref, wd_ref)

    def fetch(e, slot):
        return [pltpu.make_async_copy(w_hbm[i].at[e], wf_scr.at[slot, i], sem.at[slot, i]) for i in range(3)]

    @pl.when(active & ((j == 0) | (be_ref[j] != be_ref[jnp.maximum(j - 1, 0)])))
    def _():
        slot = reg_ref[j] % 2

        @pl.when(j == 0)
        def _():
            for cp in fetch(be_ref[0], 0):
                cp.start()

        for cp in fetch(be_ref[j], slot):
            cp.wait()
        wg_scr[...] = wf_scr[slot, 0].astype(BF16)
        wu_scr[...] = wf_scr[slot, 1].astype(BF16)
        wd_scr[...] = wf_scr[slot, 2].astype(BF16)

        @pl.when(nxt_ref[j] >= 0)
        def _():
            for cp in fetch(nxt_ref[j], 1 - slot):
                cp.start()

    @pl.when(active)
    def _():
        rows = xs_ref.shape[0] // ROW_TILE
        F, D = wg_scr.shape[1], wd_scr.shape[1]
        xb = _load_row_tiles(xs_ref, rows).astype(BF16)

        def act(gu, cs):
            g, u = gu
            g = jnp.minimum(g + bg_ref[0, :, cs], SWIGLU_LIMIT)
            u = jnp.clip(u + bu_ref[0, :, cs], -SWIGLU_LIMIT, SWIGLU_LIMIT)
            h_scr[:, cs] = ((u + 1.0) * (g * jax.nn.sigmoid(SWIGLU_ALPHA * g))).astype(BF16)

        def put(y, cs):
            y = y + bd_ref[0, :, cs]
            for j in range(cs.start // LANES, cs.stop // LANES):
                y_ref[pl.ds(j, rows, stride=ROW_TILE), :] = y[:, j * LANES - cs.start:(j + 1) * LANES - cs.start]

        f_pieces = [slice(p, p + FFN_PIECE) for p in range(0, F, FFN_PIECE)]
        d_pieces = [slice(p, p + FFN_PIECE) for p in range(0, D, FFN_PIECE)]
        pending = None
        for cs in f_pieces:
            gu = (_dot(xb, wg_scr[:, cs]), _dot(xb, wu_scr[:, cs]))
            if pending is not None:
                act(*pending)
            pending = (gu, cs)
        act(*pending)
        pending = None
        for cs in d_pieces:
            y = _dot(h_scr[...], wd_scr[:, cs])
            if pending is not None:
                put(*pending)
            pending = (y, cs)
        put(*pending)


def _experts(xs, block_e, n_used, region, next_e, w_gate, b_gate, w_up, b_up, w_down, b_down):
    E, D, F = w_gate.shape
    assert D == ROW_TILE * LANES and F == D, "the two weight staging slots hold (D, F) and (F, D) alike"
    n_slots = xs.shape[0] // ROW_TILE
    R = MOE_ROWS
    n_blocks = n_slots // R
    blk = lambda j, be, nu, rg, nx: (jnp.minimum(j, nu[0] - 1), 0)
    wsel = lambda j, be, nu, rg, nx: (be[jnp.minimum(j, nu[0] - 1)], 0, 0)
    hbm = pl.BlockSpec(memory_space=pl.ANY)
    grid_spec = pltpu.PrefetchScalarGridSpec(
        num_scalar_prefetch=4,
        grid=(n_blocks,),
        in_specs=[
            pl.BlockSpec((R * ROW_TILE, LANES), blk),
            hbm, pl.BlockSpec((1, 1, F), wsel),
            hbm, pl.BlockSpec((1, 1, F), wsel),
            hbm, pl.BlockSpec((1, 1, D), wsel),
        ],
        out_specs=pl.BlockSpec((R * ROW_TILE, LANES), blk),
        scratch_shapes=[pltpu.VMEM((2, 3, D, F), F32), pltpu.VMEM((D, F), BF16), pltpu.VMEM((D, F), BF16),
                        pltpu.VMEM((F, D), BF16), pltpu.VMEM((R, F), BF16), pltpu.SemaphoreType.DMA((2, 3))],
    )
    return pl.pallas_call(
        _expert_kernel,
        out_shape=jax.ShapeDtypeStruct((n_slots * ROW_TILE, LANES), F32),
        grid_spec=grid_spec,
        compiler_params=pltpu.CompilerParams(dimension_semantics=("arbitrary",), vmem_limit_bytes=VMEM_LIMIT),
        name="experts",
    )(block_e, n_used, region, next_e, xs, w_gate, b_gate.reshape(E, 1, F), w_up, b_up.reshape(E, 1, F),
      w_down, b_down.reshape(E, 1, D))


def _combine_kernel(dcur_ref, dnext_ref, x_ref, gate_ref, gfin_ref, ys_ref, o_ref, buf, sem):
    tc = x_ref.shape[0]
    i = pl.program_id(0)
    slot = i % 2

    def issue_all(d_ref, into):
        def issue(grp, carry):
            t0 = pl.multiple_of(grp * ROW_UNROLL, ROW_UNROLL)
            for kk in range(TOP_K):
                for u in range(ROW_UNROLL):
                    src = pl.multiple_of(d_ref[kk, t0 + u] * ROW_TILE, ROW_TILE)
                    row = pl.multiple_of((t0 + u) * ROW_TILE, ROW_TILE)
                    pltpu.make_async_copy(ys_ref.at[pl.ds(src, ROW_TILE)], buf.at[into, kk, pl.ds(row, ROW_TILE)],
                                          sem.at[into]).start(priority=u % 2)
            return carry

        lax.fori_loop(0, tc // ROW_UNROLL, issue, 0)

    def wait_slot(s):
        for kk in range(TOP_K):
            pltpu.make_async_copy(ys_ref.at[pl.ds(0, tc * ROW_TILE)], buf.at[s, kk], sem.at[s]).wait()

    @pl.when(i == 0)
    def _():
        issue_all(dcur_ref, 0)

    wait_slot(slot)
    other = 1 - slot
    for t in range(tc):
        for kk in range(TOP_K):
            src = pl.multiple_of(dnext_ref[kk, t] * ROW_TILE, ROW_TILE)
            pltpu.make_async_copy(ys_ref.at[pl.ds(src, ROW_TILE)], buf.at[other, kk, pl.ds(t * ROW_TILE, ROW_TILE)],
                                  sem.at[other]).start(priority=t % 2)
    for r0 in range(0, tc, COMBINE_ROWS):
        rows = slice(r0, r0 + COMBINE_ROWS)
        acc = x_ref[rows, :]
        for kk in range(TOP_K):
            acc = acc + gate_ref[rows, kk:kk + 1] * _load_row_tiles(buf.at[slot, kk], COMBINE_ROWS, r0)
        o_ref[rows, :] = _rms(acc, gfin_ref[...])

    @pl.when(i == pl.num_programs(0) - 1)
    def _():
        wait_slot(other)


def _combine(x2, gate_t, dest, ys, g_final):
    T, D = x2.shape
    tc = min(COMBINE_TOKENS, T)
    assert T % tc == 0
    nb = T // tc
    return pl.pallas_call(
        _combine_kernel,
        out_shape=jax.ShapeDtypeStruct((T, D), F32),
        grid=(nb,),
        in_specs=[
            pl.BlockSpec((TOP_K, tc), lambda i: (0, i), memory_space=pltpu.SMEM),
            pl.BlockSpec((TOP_K, tc), lambda i: (0, jnp.minimum(i + 1, nb - 1)), memory_space=pltpu.SMEM),
            pl.BlockSpec((tc, D), lambda i: (i, 0)),
            pl.BlockSpec((tc, TOP_K), lambda i: (i, 0)),
            pl.BlockSpec((1, D), lambda i: (0, 0)),
            pl.BlockSpec(memory_space=pl.ANY),
        ],
        out_specs=pl.BlockSpec((tc, D), lambda i: (i, 0)),
        scratch_shapes=[pltpu.VMEM((2, TOP_K, tc * ROW_TILE, LANES), F32), pltpu.SemaphoreType.DMA((2,))],
        compiler_params=pltpu.CompilerParams(dimension_semantics=("arbitrary",), vmem_limit_bytes=VMEM_LIMIT),
        name="combine",
    )(dest, dest, x2, gate_t, g_final.reshape(1, D), ys)


def _moe(x2, xn, idx, gate, rank, cnt, w_gate, b_gate, w_up, b_up, w_down, b_down, g_final):
    T = x2.shape[0]
    E = w_gate.shape[0]
    R = MOE_ROWS
    n_blocks = -(-T * TOP_K // R) + E
    n_slots = n_blocks * R
    counts = cnt[:, 0].astype(jnp.int32)
    padded = (counts + R - 1) // R * R
    pad_end = jnp.cumsum(padded)
    pad_start = pad_end - padded
    dest = rank
    for e in range(E):
        dest = dest + jnp.where(idx == e, pad_start[e], 0)
    n_used = (pad_end[-1:] // R).astype(jnp.int32)
    block_e = jnp.minimum(jnp.sum(jnp.arange(n_blocks)[:, None] * R >= pad_end[None, :], axis=1), E - 1)
    block_e = block_e.astype(jnp.int32)
    blocks = jnp.arange(n_blocks)
    first = (blocks < n_used[0]) & ((blocks == 0) | (block_e != jnp.roll(block_e, 1)))
    region = (jnp.cumsum(first) - 1).astype(jnp.int32)
    later = jnp.where((counts[None, :] > 0) & (jnp.arange(E)[None, :] > block_e[:, None]), jnp.arange(E)[None, :], E)
    next_e = jnp.min(later, axis=1)
    next_e = jnp.where(next_e == E, -1, next_e).astype(jnp.int32)

    xs = _dispatch(xn, dest, counts, pad_end.astype(jnp.int32), n_slots)
    ys = _experts(xs, block_e, n_used, region, next_e, w_gate, b_gate, w_up, b_up, w_down, b_down)
    return _combine(x2, gate.T, dest, ys, g_final)


def kernel(x, mem, g_mix, w_in, b_gates, g_mlstm_head, w_spatial, b_spatial, g_gmlp_v, g_gmlp_out, w_out,
           g_xattn, g_mem, w_q, w_kv, w_xo, g_moe, w_router, b_router, w_gate, b_gate, w_up, b_up,
           w_down, b_down, g_final):
    B, S, D = x.shape
    assert g_mix.shape[0] == 1, "the combine kernel fuses the closing norm, so exactly one layer is supported"
    x1 = _mixer(x, g_mix[0], w_in[0], b_gates[0], g_mlstm_head[0], w_spatial[0], b_spatial[0],
                g_gmlp_v[0], g_gmlp_out[0], w_out[0])
    k, v = _kv_proj(mem, g_mem[0], w_kv[0])
    x2, xn, idx, gate, rank, cnt = _xattn_router(x1, k, v, g_xattn[0], w_q[0], w_xo[0], g_moe[0],
                                                 w_router[0], b_router[0])
    out = _moe(x2.reshape(B * S, D), xn, idx, gate, rank, cnt, w_gate[0], b_gate[0], w_up[0], b_up[0],
               w_down[0], b_down[0], g_final)
    return out.reshape(B, S, D)
```

```python
import functools

import jax
import jax.numpy as jnp
from jax import lax
from jax.experimental import pallas as pl
from jax.experimental.pallas import tpu as pltpu

F32 = jnp.float32
BF16 = jnp.bfloat16

MLSTM_HEADS = 4
MLSTM_DK = 64
MLSTM_DV = 128
CHUNK = 128
GMLP_GROUPS = 4
GMLP_DG = 128
XATTN_HEADS = 4
TOP_K = 4
GATE_SOFTCAP = 15.0
SWIGLU_LIMIT = 7.0
SWIGLU_ALPHA = 1.702
NORM_EPS = 1e-6

LANES = 128
ROW_TILE = 8

MIX_TOKENS = 512
PROJ_PIECE = 512
XATTN_TOKENS = 512
MOE_ROWS = 512
DISPATCH_TOKENS = 2048
COMBINE_TOKENS = 256
COMBINE_ROWS = 32
ROW_UNROLL = 16
VMEM_LIMIT = 52 * 1024 * 1024


def _rms(x, g):
    return x * lax.rsqrt(jnp.mean(x * x, axis=-1, keepdims=True) + NORM_EPS) * g


def _gelu_tanh(x):
    return 0.5 * x * (1.0 + jnp.tanh(0.7978845608028654 * (x + 0.044715 * x * x * x)))


def _dot(a, b):
    return jnp.dot(a, b, preferred_element_type=F32)


def _dot_nt(a, b):
    return lax.dot_general(a, b, (((1,), (1,)), ((), ())), preferred_element_type=F32)


def _load_row_tiles(ref, rows, first=0):
    return jnp.concatenate(
        [ref[pl.ds(first * ROW_TILE + j, rows, stride=ROW_TILE), :] for j in range(ROW_TILE)], axis=1)


def _store_row_tiles(ref, val):
    for j in range(ROW_TILE):
        ref[pl.ds(j, val.shape[0], stride=ROW_TILE), :] = val[:, j * LANES:(j + 1) * LANES]


def _split3(x):
    hi = x.astype(BF16)
    r1 = x - hi.astype(F32)
    mid = r1.astype(BF16)
    lo = (r1 - mid.astype(F32)).astype(BF16)
    return hi, mid, lo


def _mixer_kernel(x_ref, gmix_ref, win_ref, bg_ref, ghead_ref, wsp_ref, bsp_ref, gzv_ref, gzo_ref, wout_ref,
                  o_ref, xnb_scr, proj_scr, y_scr, ct_scr, m_scr, *, n_chunks, cols):
    qk0, v0, og0, gu0, gv0, gt0 = cols
    H, G, L = MLSTM_HEADS, GMLP_GROUPS, CHUNK

    @pl.when(pl.program_id(1) == 0)
    def _():
        ct_scr[...] = jnp.zeros_like(ct_scr)
        m_scr[...] = jnp.zeros_like(m_scr)

    row = lax.broadcasted_iota(jnp.int32, (L, L), 0)
    col = lax.broadcasted_iota(jnp.int32, (L, L), 1)
    causal = row >= col
    tril = jnp.where(causal, 1.0, 0.0).astype(BF16)
    triu = jnp.where(row <= col, 1.0, 0.0).astype(BF16)
    lane = lax.broadcasted_iota(jnp.int32, (L, LANES), 1)
    ones_col = jnp.where(lane == 0, 1.0, 0.0).astype(F32)

    xnb_scr[...] = _rms(x_ref[0], gmix_ref[...]).astype(BF16)

    def in_pieces(c):
        r = slice(c * L, (c + 1) * L)
        for p in range(0, proj_scr.shape[1], PROJ_PIECE):
            cs = slice(p, min(p + PROJ_PIECE, proj_scr.shape[1]))
            yield lambda cs=cs: proj_scr.__setitem__((r, cs), _dot(xnb_scr[r, :], win_ref[:, cs]))

    def out_pieces(c):
        r = slice(c * L, (c + 1) * L)
        for p in range(0, o_ref.shape[2], PROJ_PIECE):
            cs = slice(p, p + PROJ_PIECE)
            yield lambda cs=cs: o_ref.__setitem__((0, r, cs), x_ref[0, r, cs] + _dot(y_scr[r, :], wout_ref[:, cs]))

    fillers = []

    def tick(n=1):
        for _ in range(min(n, len(fillers))):
            fillers.pop(0)()

    for piece in in_pieces(0):
        piece()

    for c in range(n_chunks):
        rows = slice(c * L, (c + 1) * L)
        if c + 1 < n_chunks:
            fillers.extend(in_pieces(c + 1))
        if c > 0:
            fillers.extend(out_pieces(c - 1))
        pre = proj_scr[rows, gt0:gt0 + LANES] + bg_ref[...]
        capped = GATE_SOFTCAP * jnp.tanh(pre * (1.0 / GATE_SOFTCAP))
        log_sig = jnp.minimum(capped, 0.0) - jnp.log1p(jnp.exp(-jnp.abs(capped)))
        lg = jnp.where(lane < H, capped, log_sig)
        lg_t = lg.T[0:ROW_TILE, :]
        bcol = sum(_dot(tril, p) for p in _split3(lg))
        brow = sum(_dot(p, triu) for p in _split3(lg_t))

        hs = range(H)
        b_c = [bcol[:, H + h:H + h + 1] for h in hs]
        i_c = [lg[:, h:h + 1] for h in hs]
        b_r = [brow[H + h:H + h + 1, :] for h in hs]
        i_r = [lg_t[h:h + 1, :] for h in hs]
        m_prev = [m_scr[h, 0:1, 0:1] for h in hs]
        qk = [proj_scr[rows, qk0 + h * LANES:qk0 + (h + 1) * LANES] for h in hs]
        q = [jnp.where(lane < MLSTM_DK, qk[h] * (MLSTM_DK ** -0.5), 0.0) for h in hs]
        k = [jnp.where(lane < MLSTM_DK, pltpu.roll(qk[h], MLSTM_DK, axis=1), 0.0) for h in hs]
        qb = [q[h].astype(BF16) for h in hs]
        kb = [k[h].astype(BF16) for h in hs]
        tick()
        vaug = [jnp.concatenate([proj_scr[rows, v0 + h * MLSTM_DV:v0 + (h + 1) * MLSTM_DV], ones_col], axis=1)
                for h in hs]

        d = [jnp.where(causal, b_c[h] - b_r[h] + i_r[h], -jnp.inf) for h in hs]
        tick()
        inter = [b_c[h] + m_prev[h] for h in hs]
        m_t = [jnp.maximum(inter[h], jnp.max(d[h], axis=1, keepdims=True)) for h in hs]
        tick()
        w_intra = [jnp.exp(d[h] - m_t[h]) for h in hs]
        tick()
        w_inter = [jnp.exp(inter[h] - m_t[h]) for h in hs]
        s = [_dot_nt(qb[h], kb[h]) * w_intra[h] for h in hs]
        tick()
        ct = [ct_scr[h] for h in hs]
        na = [_dot(s[h].astype(BF16), vaug[h].astype(BF16)) + w_inter[h] * _dot(qb[h], ct[h].astype(BF16))
              for h in hs]
        hh = [na[h][:, :MLSTM_DV] / jnp.maximum(jnp.abs(na[h][:, MLSTM_DV:MLSTM_DV + 1]), jnp.exp(-m_t[h]))
              for h in hs]

        b_last = [b_c[h][L - 1:L, :] for h in hs]
        g_c = [b_last[h] - b_c[h] + i_c[h] for h in hs]
        m_new = [jnp.maximum(b_last[h] + m_prev[h], jnp.max(g_c[h], axis=0, keepdims=True)) for h in hs]
        wk = [jnp.exp(g_c[h] - m_new[h]) for h in hs]
        decay = [jnp.exp(b_last[h] + m_prev[h] - m_new[h]) for h in hs]
        tick()
        for h in hs:
            ct_scr[h] = decay[h] * ct[h] + _dot(k[h].T.astype(BF16), (wk[h] * vaug[h]).astype(BF16))
            m_scr[h] = jnp.broadcast_to(m_new[h], m_scr.shape[1:])

        hn = [_rms(hh[h], ghead_ref[:, h * MLSTM_DV:(h + 1) * MLSTM_DV]) for h in hs]
        tick()
        for h in hs:
            og = proj_scr[rows, og0 + h * MLSTM_DV:og0 + (h + 1) * MLSTM_DV]
            y_scr[rows, h * MLSTM_DV:(h + 1) * MLSTM_DV] = (jax.nn.sigmoid(og) * hn[h]).astype(BF16)

        gs = range(G)
        sl = [slice(g * GMLP_DG, (g + 1) * GMLP_DG) for g in gs]
        z = [_rms(_gelu_tanh(proj_scr[rows, gv0 + g * GMLP_DG:gv0 + (g + 1) * GMLP_DG]), gzv_ref[:, sl[g]]) for g in gs]
        mixed = [_dot(jnp.where(causal, wsp_ref[g], 0.0).astype(BF16), z[g].astype(BF16)) + bsp_ref[g] for g in gs]
        u = [_gelu_tanh(proj_scr[rows, gu0 + g * GMLP_DG:gu0 + (g + 1) * GMLP_DG]) for g in gs]
        tick()
        yg = [_rms(u[g] * mixed[g], gzo_ref[:, sl[g]]) for g in gs]
        tick()
        for g in gs:
            y_scr[rows, H * MLSTM_DV + g * GMLP_DG:H * MLSTM_DV + (g + 1) * GMLP_DG] = yg[g].astype(BF16)
        tick(len(fillers))
    for piece in out_pieces(n_chunks - 1):
        piece()


def _mixer(x, g_mix, w_in, b_gates, g_head, w_spatial, b_spatial, g_zv, g_zo, w_out):
    B, S, D = x.shape
    H, G, L = MLSTM_HEADS, GMLP_GROUPS, CHUNK
    qw, vw, gw = H * MLSTM_DK, H * MLSTM_DV, G * GMLP_DG
    ts = min(MIX_TOKENS, S)
    assert S % ts == 0 and ts % L == 0

    assert 2 * MLSTM_DK == LANES
    c = 0
    w_q = w_in[:, c:c + qw].reshape(D, H, MLSTM_DK); c += qw
    w_k = w_in[:, c:c + qw].reshape(D, H, MLSTM_DK); c += qw
    w_qk = jnp.concatenate([w_q, w_k], axis=2).reshape(D, H * LANES)
    w_v = w_in[:, c:c + vw]; c += vw
    w_o = w_in[:, c:c + vw]; c += vw
    w_g = jnp.pad(w_in[:, c:c + 2 * H], ((0, 0), (0, LANES - 2 * H))); c += 2 * H
    w_gu = w_in[:, c:c + gw]; c += gw
    w_gv = w_in[:, c:c + gw]; c += gw
    w_pack = jnp.concatenate([w_qk, w_v, w_o, w_gu, w_gv, w_g], axis=1).astype(BF16)
    hp = H * LANES
    cols = (0, hp, hp + vw, hp + 2 * vw, hp + 2 * vw + gw, hp + 2 * vw + 2 * gw)
    n_cols = w_pack.shape[1]
    bg = jnp.pad(b_gates, (0, LANES - 2 * H)).reshape(1, LANES)
    bsp = jnp.broadcast_to(b_spatial[:, :, None], (G, L, GMLP_DG))

    const = lambda *shape: pl.BlockSpec(shape, lambda b, j: (0,) * len(shape))
    return pl.pallas_call(
        functools.partial(_mixer_kernel, n_chunks=ts // L, cols=cols),
        out_shape=jax.ShapeDtypeStruct((B, S, D), F32),
        grid=(B, S // ts),
        in_specs=[
            pl.BlockSpec((1, ts, D), lambda b, j: (b, j, 0)),
            const(1, D), const(D, n_cols), const(1, LANES), const(1, vw),
            const(G, L, L), const(G, L, GMLP_DG), const(1, gw), const(1, gw), const(vw + gw, D),
        ],
        out_specs=pl.BlockSpec((1, ts, D), lambda b, j: (b, j, 0)),
        scratch_shapes=[
            pltpu.VMEM((ts, D), BF16),
            pltpu.VMEM((ts, n_cols), F32),
            pltpu.VMEM((ts, vw + gw), BF16),
            pltpu.VMEM((H, LANES, 2 * MLSTM_DV), F32),
            pltpu.VMEM((H, ROW_TILE, LANES), F32),
        ],
        compiler_params=pltpu.CompilerParams(
            dimension_semantics=("arbitrary", "arbitrary"), vmem_limit_bytes=VMEM_LIMIT),
        name="mixer",
    )(x, g_mix.reshape(1, D), w_pack, bg, g_head.reshape(1, vw), w_spatial, bsp,
      g_zv.reshape(1, gw), g_zo.reshape(1, gw), w_out.astype(BF16))


def _kv_kernel(mem_ref, g_ref, w_ref, k_ref, v_ref):
    D = mem_ref.shape[-1]
    kv = _dot(_rms(mem_ref[0], g_ref[...]).astype(BF16), w_ref[...])
    k_ref[0] = kv[:, :D].astype(BF16)
    v_ref[0] = kv[:, D:].astype(BF16)


def _kv_proj(mem, g_mem, w_kv):
    B, M, D = mem.shape
    return pl.pallas_call(
        _kv_kernel,
        out_shape=(jax.ShapeDtypeStruct((B, M, D), BF16), jax.ShapeDtypeStruct((B, M, D), BF16)),
        grid=(B,),
        in_specs=[
            pl.BlockSpec((1, M, D), lambda b: (b, 0, 0)),
            pl.BlockSpec((1, D), lambda b: (0, 0)),
            pl.BlockSpec((D, 2 * D), lambda b: (0, 0)),
        ],
        out_specs=(pl.BlockSpec((1, M, D), lambda b: (b, 0, 0)), pl.BlockSpec((1, M, D), lambda b: (b, 0, 0))),
        compiler_params=pltpu.CompilerParams(dimension_semantics=("arbitrary",), vmem_limit_bytes=VMEM_LIMIT),
        name="kv_proj",
    )(mem, g_mem.reshape(1, D), w_kv.astype(BF16))


def _xattn_kernel(x_ref, gx_ref, wq_ref, k_ref, v_ref, wxo_ref, gmoe_ref, wr_ref, br_ref,
                  x2_ref, xn_ref, idx_ref, gate_ref, rank_ref, cnt_ref, xnb_scr, q_scr, o_scr, cnt_scr):
    ts, D = x_ref.shape[1], x_ref.shape[2]
    hd = D // XATTN_HEADS
    E = wr_ref.shape[0]

    @pl.when((pl.program_id(0) == 0) & (pl.program_id(1) == 0))
    def _():
        cnt_scr[...] = jnp.zeros_like(cnt_scr)

    xnb_scr[...] = _rms(x_ref[0], gx_ref[...]).astype(BF16)

    half = ts // 2
    halves = [slice(0, half), slice(half, ts)]
    pieces = [slice(p, p + PROJ_PIECE) for p in range(0, D, PROJ_PIECE)]

    def q_pieces(r):
        for cs in pieces:
            yield lambda cs=cs: q_scr.__setitem__(
                (r, cs), (_dot(xnb_scr[r, :], wq_ref[:, cs]) * (hd ** -0.5)).astype(BF16))

    def xo_pieces(r):
        for cs in pieces:
            yield lambda cs=cs: x2_ref.__setitem__(
                (0, r, cs), x_ref[0, r, cs] + _dot(o_scr[r, :], wxo_ref[:, cs]))

    fillers = []

    def tick(n=1):
        for _ in range(min(n, len(fillers))):
            fillers.pop(0)()

    def attention(r):
        hs = range(XATTN_HEADS)
        sl = [slice(h * hd, (h + 1) * hd) for h in hs]
        s = [_dot_nt(q_scr[r, sl[h]], k_ref[0, :, sl[h]]) for h in hs]
        tick()
        p = [jnp.exp(s[h] - jnp.max(s[h], axis=1, keepdims=True)) for h in hs]
        tick()
        p = [p[h] / jnp.sum(p[h], axis=1, keepdims=True) for h in hs]
        for h in hs:
            o_scr[r, sl[h]] = _dot(p[h].astype(BF16), v_ref[0, :, sl[h]]).astype(BF16)
        tick(len(fillers))

    for piece in q_pieces(halves[0]):
        piece()
    fillers.extend(q_pieces(halves[1]))
    attention(halves[0])
    fillers.extend(xo_pieces(halves[0]))
    attention(halves[1])
    for piece in xo_pieces(halves[1]):
        piece()
    x2 = x2_ref[0]

    xn = _rms(x2, gmoe_ref[...])
    xh, xm, _ = _split3(xn)
    wh, wm, _ = _split3(wr_ref[...])
    logits = _dot_nt(wh, xh) + (_dot_nt(wh, xm) + _dot_nt(wm, xh)) + br_ref[...]

    e_iota = lax.broadcasted_iota(jnp.int32, (E, ts), 0)
    work = logits
    tops, idxs, hots = [], [], []
    for _ in range(TOP_K):
        m = jnp.max(work, axis=0, keepdims=True)
        i = jnp.min(jnp.where(work == m, e_iota, E), axis=0, keepdims=True)
        hot = e_iota == i
        work = jnp.where(hot, -jnp.inf, work)
        tops.append(m); idxs.append(i); hots.append(hot)
    ex = [jnp.exp(t - tops[0]) for t in tops]
    tot = ex[0] + ex[1] + ex[2] + ex[3]
    gate_ref[...] = jnp.concatenate([e / tot for e in ex], axis=0)
    idx_ref[...] = jnp.concatenate(idxs, axis=0)

    hot_all = jnp.where(hots[0] | hots[1] | hots[2] | hots[3], 1.0, 0.0)
    r = lax.broadcasted_iota(jnp.int32, (ts, ts), 0)
    c = lax.broadcasted_iota(jnp.int32, (ts, ts), 1)
    before = jnp.where(r < c, 1.0, 0.0).astype(BF16)
    base = cnt_scr[:, 0:1] + _dot(hot_all.astype(BF16), before)
    ranks = [jnp.sum(jnp.where(hot, base, 0.0), axis=0, keepdims=True) for hot in hots]
    rank_ref[...] = jnp.concatenate(ranks, axis=0).astype(jnp.int32)
    cnt_new = cnt_scr[...] + jnp.sum(hot_all, axis=1, keepdims=True)
    cnt_scr[...] = cnt_new
    cnt_ref[...] = cnt_new
    _store_row_tiles(xn_ref, xn)


def _xattn_router(x, k, v, g_xattn, w_q, w_xo, g_moe, w_router, b_router):
    B, S, D = x.shape
    M = k.shape[1]
    E = w_router.shape[1]
    ts = min(XATTN_TOKENS, S)
    assert S % ts == 0
    nj = S // ts
    T = B * S
    const = lambda *shape: pl.BlockSpec(shape, lambda b, j: (0,) * len(shape))
    tok = lambda rows: pl.BlockSpec((rows, ts), lambda b, j: (0, b * nj + j))
    return pl.pallas_call(
        _xattn_kernel,
        out_shape=(
            jax.ShapeDtypeStruct((B, S, D), F32),
            jax.ShapeDtypeStruct((T * ROW_TILE, LANES), F32),
            jax.ShapeDtypeStruct((TOP_K, T), jnp.int32),
            jax.ShapeDtypeStruct((TOP_K, T), F32),
            jax.ShapeDtypeStruct((TOP_K, T), jnp.int32),
            jax.ShapeDtypeStruct((E, LANES), F32),
        ),
        grid=(B, nj),
        in_specs=[
            pl.BlockSpec((1, ts, D), lambda b, j: (b, j, 0)),
            const(1, D), const(D, D),
            pl.BlockSpec((1, M, D), lambda b, j: (b, 0, 0)),
            pl.BlockSpec((1, M, D), lambda b, j: (b, 0, 0)),
            const(D, D), const(1, D), const(E, D), const(E, 1),
        ],
        out_specs=(
            pl.BlockSpec((1, ts, D), lambda b, j: (b, j, 0)),
            pl.BlockSpec((ts * ROW_TILE, LANES), lambda b, j: (b * nj + j, 0)),
            tok(TOP_K), tok(TOP_K), tok(TOP_K),
            const(E, LANES),
        ),
        scratch_shapes=[pltpu.VMEM((ts, D), BF16), pltpu.VMEM((ts, D), BF16), pltpu.VMEM((ts, D), BF16),
                        pltpu.VMEM((E, LANES), F32)],
        compiler_params=pltpu.CompilerParams(
            dimension_semantics=("arbitrary", "arbitrary"), vmem_limit_bytes=VMEM_LIMIT),
        name="xattn_router",
    )(x, g_xattn.reshape(1, D), w_q.astype(BF16), k, v, w_xo.astype(BF16), g_moe.reshape(1, D),
      w_router.T, b_router.reshape(E, 1))


def _dispatch_kernel(dest_ref, cnt_ref, pend_ref, xn_ref, xs_ref, zero_scr, sem, zsem, *, n_experts):
    tt = xn_ref.shape[0] // ROW_TILE
    R = zero_scr.shape[0] // ROW_TILE

    @pl.when(pl.program_id(0) == 0)
    def _():
        zero_scr[...] = jnp.zeros_like(zero_scr)

        def fill_copy(e):
            start = pl.multiple_of((pend_ref[e] - R) * ROW_TILE, R * ROW_TILE)
            return pltpu.make_async_copy(zero_scr, xs_ref.at[pl.ds(start, R * ROW_TILE)], zsem)

        def start_fill(e, carry):
            @pl.when(cnt_ref[e] > 0)
            def _():
                fill_copy(e).start()
            return carry

        def wait_fill(e, carry):
            @pl.when(cnt_ref[e] > 0)
            def _():
                fill_copy(e).wait()
            return carry

        lax.fori_loop(0, n_experts, start_fill, 0)
        lax.fori_loop(0, n_experts, wait_fill, 0)

    def issue(grp, carry):
        t0 = pl.multiple_of(grp * ROW_UNROLL, ROW_UNROLL)
        for kk in range(TOP_K):
            for u in range(ROW_UNROLL):
                dst = pl.multiple_of(dest_ref[0, 0, kk * tt + t0 + u] * ROW_TILE, ROW_TILE)
                row = pl.multiple_of((t0 + u) * ROW_TILE, ROW_TILE)
                pltpu.make_async_copy(xn_ref.at[pl.ds(row, ROW_TILE)], xs_ref.at[pl.ds(dst, ROW_TILE)],
                                      sem).start(priority=u % 2)
        return carry

    lax.fori_loop(0, tt // ROW_UNROLL, issue, 0)
    for kk in range(TOP_K):
        pltpu.make_async_copy(xn_ref, xs_ref.at[pl.ds(0, tt * ROW_TILE)], sem).wait()


def _dispatch(xn, dest, counts, pad_end, n_slots):
    T = dest.shape[1]
    tt = min(DISPATCH_TOKENS, T)
    assert T % tt == 0
    E = counts.shape[0]
    dest = dest.reshape(TOP_K, T // tt, tt).transpose(1, 0, 2).reshape(T // tt, 1, TOP_K * tt)
    return pl.pallas_call(
        functools.partial(_dispatch_kernel, n_experts=E),
        out_shape=jax.ShapeDtypeStruct((n_slots * ROW_TILE, LANES), F32),
        grid=(T // tt,),
        in_specs=[
            pl.BlockSpec((1, 1, TOP_K * tt), lambda i: (i, 0, 0), memory_space=pltpu.SMEM),
            pl.BlockSpec(memory_space=pltpu.SMEM),
            pl.BlockSpec(memory_space=pltpu.SMEM),
            pl.BlockSpec((tt * ROW_TILE, LANES), lambda i: (i, 0)),
        ],
        out_specs=pl.BlockSpec(memory_space=pl.ANY),
        scratch_shapes=[pltpu.VMEM((MOE_ROWS * ROW_TILE, LANES), F32), pltpu.SemaphoreType.DMA,
                        pltpu.SemaphoreType.DMA],
        compiler_params=pltpu.CompilerParams(
            dimension_semantics=("arbitrary",), vmem_limit_bytes=VMEM_LIMIT, has_side_effects=True),
        name="dispatch",
    )(dest, counts, pad_end, xn)


def _expert_kernel(be_ref, nu_ref, reg_ref, nxt_ref, xs_ref, wg_ref, bg_ref, wu_ref, bu_ref, wd_ref, bd_ref, y_ref,
                   wf_scr, wg_scr, wu_scr, wd_scr, sem):
    j = pl.program_id(0)
    active = j < nu_ref[0]
    w_hbm = (wg_ref, wu_ref, wd_ref)

    def fetch(e, slot):
        return [pltpu.make_async_copy(w_hbm[i].at[e], wf_scr.at[slot, i], sem.at[slot, i]) for i in range(3)]

    @pl.when(active & ((j == 0) | (be_ref[j] != be_ref[jnp.maximum(j - 1, 0)])))
    def _():
        slot = reg_ref[j] % 2

        @pl.when(j == 0)
        def _():
            for cp in fetch(be_ref[0], 0):
                cp.start()

        for cp in fetch(be_ref[j], slot):
            cp.wait()
        wg_scr[...] = wf_scr[slot, 0].astype(BF16)
        wu_scr[...] = wf_scr[slot, 1].astype(BF16)
        wd_scr[...] = wf_scr[slot, 2].astype(BF16)

        @pl.when(nxt_ref[j] >= 0)
        def _():
            for cp in fetch(nxt_ref[j], 1 - slot):
                cp.start()

    @pl.when(active)
    def _():
        xb = _load_row_tiles(xs_ref, xs_ref.shape[0] // ROW_TILE).astype(BF16)
        g = jnp.minimum(_dot(xb, wg_scr[...]) + bg_ref[0], SWIGLU_LIMIT)
        u = jnp.clip(_dot(xb, wu_scr[...]) + bu_ref[0], -SWIGLU_LIMIT, SWIGLU_LIMIT)
        hdn = (u + 1.0) * (g * jax.nn.sigmoid(SWIGLU_ALPHA * g))
        _store_row_tiles(y_ref, _dot(hdn.astype(BF16), wd_scr[...]) + bd_ref[0])


def _experts(xs, block_e, n_used, region, next_e, w_gate, b_gate, w_up, b_up, w_down, b_down):
    E, D, F = w_gate.shape
    assert D == ROW_TILE * LANES and F == D, "the two weight staging slots hold (D, F) and (F, D) alike"
    n_slots = xs.shape[0] // ROW_TILE
    R = MOE_ROWS
    n_blocks = n_slots // R
    blk = lambda j, be, nu, rg, nx: (jnp.minimum(j, nu[0] - 1), 0)
    wsel = lambda j, be, nu, rg, nx: (be[jnp.minimum(j, nu[0] - 1)], 0, 0)
    hbm = pl.BlockSpec(memory_space=pl.ANY)
    grid_spec = pltpu.PrefetchScalarGridSpec(
        num_scalar_prefetch=4,
        grid=(n_blocks,),
        in_specs=[
            pl.BlockSpec((R * ROW_TILE, LANES), blk),
            hbm, pl.BlockSpec((1, 1, F), wsel),
            hbm, pl.BlockSpec((1, 1, F), wsel),
            hbm, pl.BlockSpec((1, 1, D), wsel),
        ],
        out_specs=pl.BlockSpec((R * ROW_TILE, LANES), blk),
        scratch_shapes=[pltpu.VMEM((2, 3, D, F), F32), pltpu.VMEM((D, F), BF16), pltpu.VMEM((D, F), BF16),
                        pltpu.VMEM((F, D), BF16), pltpu.SemaphoreType.DMA((2, 3))],
    )
    return pl.pallas_call(
        _expert_kernel,
        out_shape=jax.ShapeDtypeStruct((n_slots * ROW_TILE, LANES), F32),
        grid_spec=grid_spec,
        compiler_params=pltpu.CompilerParams(dimension_semantics=("arbitrary",), vmem_limit_bytes=VMEM_LIMIT),
        name="experts",
    )(block_e, n_used, region, next_e, xs, w_gate, b_gate.reshape(E, 1, F), w_up, b_up.reshape(E, 1, F),
      w_down, b_down.reshape(E, 1, D))


def _combine_kernel(dcur_ref, dnext_ref, x_ref, gate_ref, gfin_ref, ys_ref, o_ref, buf, sem):
    tc = x_ref.shape[0]
    i = pl.program_id(0)
    slot = i % 2

    def issue_all(d_ref, into):
        def issue(grp, carry):
            t0 = pl.multiple_of(grp * ROW_UNROLL, ROW_UNROLL)
            for kk in range(TOP_K):
                for u in range(ROW_UNROLL):
                    src = pl.multiple_of(d_ref[kk, t0 + u] * ROW_TILE, ROW_TILE)
                    row = pl.multiple_of((t0 + u) * ROW_TILE, ROW_TILE)
                    pltpu.make_async_copy(ys_ref.at[pl.ds(src, ROW_TILE)], buf.at[into, kk, pl.ds(row, ROW_TILE)],
                                          sem.at[into]).start(priority=u % 2)
            return carry

        lax.fori_loop(0, tc // ROW_UNROLL, issue, 0)

    def wait_slot(s):
        for kk in range(TOP_K):
            pltpu.make_async_copy(ys_ref.at[pl.ds(0, tc * ROW_TILE)], buf.at[s, kk], sem.at[s]).wait()

    @pl.when(i == 0)
    def _():
        issue_all(dcur_ref, 0)

    wait_slot(slot)
    other = 1 - slot
    for t in range(tc):
        for kk in range(TOP_K):
            src = pl.multiple_of(dnext_ref[kk, t] * ROW_TILE, ROW_TILE)
            pltpu.make_async_copy(ys_ref.at[pl.ds(src, ROW_TILE)], buf.at[other, kk, pl.ds(t * ROW_TILE, ROW_TILE)],
                                  sem.at[other]).start(priority=t % 2)
    for r0 in range(0, tc, COMBINE_ROWS):
        rows = slice(r0, r0 + COMBINE_ROWS)
        acc = x_ref[rows, :]
        for kk in range(TOP_K):
            acc = acc + gate_ref[rows, kk:kk + 1] * _load_row_tiles(buf.at[slot, kk], COMBINE_ROWS, r0)
        o_ref[rows, :] = _rms(acc, gfin_ref[...])

    @pl.when(i == pl.num_programs(0) - 1)
    def _():
        wait_slot(other)


def _combine(x2, gate_t, dest, ys, g_final):
    T, D = x2.shape
    tc = min(COMBINE_TOKENS, T)
    assert T % tc == 0
    nb = T // tc
    return pl.pallas_call(
        _combine_kernel,
        out_shape=jax.ShapeDtypeStruct((T, D), F32),
        grid=(nb,),
        in_specs=[
            pl.BlockSpec((TOP_K, tc), lambda i: (0, i), memory_space=pltpu.SMEM),
            pl.BlockSpec((TOP_K, tc), lambda i: (0, jnp.minimum(i + 1, nb - 1)), memory_space=pltpu.SMEM),
            pl.BlockSpec((tc, D), lambda i: (i, 0)),
            pl.BlockSpec((tc, TOP_K), lambda i: (i, 0)),
            pl.BlockSpec((1, D), lambda i: (0, 0)),
            pl.BlockSpec(memory_space=pl.ANY),
        ],
        out_specs=pl.BlockSpec((tc, D), lambda i: (i, 0)),
        scratch_shapes=[pltpu.VMEM((2, TOP_K, tc * ROW_TILE, LANES), F32), pltpu.SemaphoreType.DMA((2,))],
        compiler_params=pltpu.CompilerParams(dimension_semantics=("arbitrary",), vmem_limit_bytes=VMEM_LIMIT),
        name="combine",
    )(dest, dest, x2, gate_t, g_final.reshape(1, D), ys)


def _moe(x2, xn, idx, gate, rank, cnt, w_gate, b_gate, w_up, b_up, w_down, b_down, g_final):
    T = x2.shape[0]
    E = w_gate.shape[0]
    R = MOE_ROWS
    n_blocks = -(-T * TOP_K // R) + E
    n_slots = n_blocks * R
    counts = cnt[:, 0].astype(jnp.int32)
    padded = (counts + R - 1) // R * R
    pad_end = jnp.cumsum(padded)
    pad_start = pad_end - padded
    dest = rank
    for e in range(E):
        dest = dest + jnp.where(idx == e, pad_start[e], 0)
    n_used = (pad_end[-1:] // R).astype(jnp.int32)
    block_e = jnp.minimum(jnp.sum(jnp.arange(n_blocks)[:, None] * R >= pad_end[None, :], axis=1), E - 1)
    block_e = block_e.astype(jnp.int32)
    blocks = jnp.arange(n_blocks)
    first = (blocks < n_used[0]) & ((blocks == 0) | (block_e != jnp.roll(block_e, 1)))
    region = (jnp.cumsum(first) - 1).astype(jnp.int32)
    later = jnp.where((counts[None, :] > 0) & (jnp.arange(E)[None, :] > block_e[:, None]), jnp.arange(E)[None, :], E)
    next_e = jnp.min(later, axis=1)
    next_e = jnp.where(next_e == E, -1, next_e).astype(jnp.int32)

    xs = _dispatch(xn, dest, counts, pad_end.astype(jnp.int32), n_slots)
    ys = _experts(xs, block_e, n_used, region, next_e, w_gate, b_gate, w_up, b_up, w_down, b_down)
    return _combine(x2, gate.T, dest, ys, g_final)


def kernel(x, mem, g_mix, w_in, b_gates, g_mlstm_head, w_spatial, b_spatial, g_gmlp_v, g_gmlp_out, w_out,
           g_xattn, g_mem, w_q, w_kv, w_xo, g_moe, w_router, b_router, w_gate, b_gate, w_up, b_up,
           w_down, b_down, g_final):
    B, S, D = x.shape
    assert g_mix.shape[0] == 1, "the combine kernel fuses the closing norm, so exactly one layer is supported"
    x1 = _mixer(x, g_mix[0], w_in[0], b_gates[0], g_mlstm_head[0], w_spatial[0], b_spatial[0],
                g_gmlp_v[0], g_gmlp_out[0], w_out[0])
    k, v = _kv_proj(mem, g_mem[0], w_kv[0])
    x2, xn, idx, gate, rank, cnt = _xattn_router(x1, k, v, g_xattn[0], w_q[0], w_xo[0], g_moe[0],
                                                 w_router[0], b_router[0])
    out = _moe(x2.reshape(B * S, D), xn, idx, gate, rank, cnt, w_gate[0], b_gate[0], w_up[0], b_up[0],
               w_down[0], b_down[0], g_final)
    return out.reshape(B, S, D)
```

```python
import functools

import jax
import jax.numpy as jnp
from jax import lax
from jax.experimental import pallas as pl
from jax.experimental.pallas import tpu as pltpu

F32 = jnp.float32
BF16 = jnp.bfloat16

MLSTM_HEADS = 4
MLSTM_DK = 64
MLSTM_DV = 128
CHUNK = 128
GMLP_GROUPS = 4
GMLP_DG = 128
XATTN_HEADS = 4
TOP_K = 4
GATE_SOFTCAP = 15.0
SWIGLU_LIMIT = 7.0
SWIGLU_ALPHA = 1.702
NORM_EPS = 1e-6

LANES = 128
ROW_TILE = 8

MIX_TOKENS = 1024
PROJ_PIECE = 512
XATTN_TOKENS = 512
MOE_ROWS = 512
DISPATCH_TOKENS = 2048
COMBINE_TOKENS = 256
COMBINE_ROWS = 32
ROW_UNROLL = 16
VMEM_LIMIT = 52 * 1024 * 1024


def _rms(x, g):
    return x * lax.rsqrt(jnp.mean(x * x, axis=-1, keepdims=True) + NORM_EPS) * g


def _gelu_tanh(x):
    return 0.5 * x * (1.0 + jnp.tanh(0.7978845608028654 * (x + 0.044715 * x * x * x)))


def _dot(a, b):
    return jnp.dot(a, b, preferred_element_type=F32)


def _dot_nt(a, b):
    return lax.dot_general(a, b, (((1,), (1,)), ((), ())), preferred_element_type=F32)


def _load_row_tiles(ref, rows, first=0):
    return jnp.concatenate(
        [ref[pl.ds(first * ROW_TILE + j, rows, stride=ROW_TILE), :] for j in range(ROW_TILE)], axis=1)


def _store_row_tiles(ref, val):
    for j in range(ROW_TILE):
        ref[pl.ds(j, val.shape[0], stride=ROW_TILE), :] = val[:, j * LANES:(j + 1) * LANES]


def _split3(x):
    hi = x.astype(BF16)
    r1 = x - hi.astype(F32)
    mid = r1.astype(BF16)
    lo = (r1 - mid.astype(F32)).astype(BF16)
    return hi, mid, lo


def _mixer_kernel(x_ref, gmix_ref, win_ref, bg_ref, ghead_ref, wsp_ref, bsp_ref, gzv_ref, gzo_ref, wout_ref,
                  o_ref, xnb_scr, proj_scr, y_scr, ct_scr, m_scr, *, n_chunks, cols):
    qk0, v0, og0, gu0, gv0, gt0 = cols
    H, G, L = MLSTM_HEADS, GMLP_GROUPS, CHUNK

    @pl.when(pl.program_id(1) == 0)
    def _():
        ct_scr[...] = jnp.zeros_like(ct_scr)
        m_scr[...] = jnp.zeros_like(m_scr)

    row = lax.broadcasted_iota(jnp.int32, (L, L), 0)
    col = lax.broadcasted_iota(jnp.int32, (L, L), 1)
    causal = row >= col
    tril = jnp.where(causal, 1.0, 0.0).astype(BF16)
    triu = jnp.where(row <= col, 1.0, 0.0).astype(BF16)
    lane = lax.broadcasted_iota(jnp.int32, (L, LANES), 1)
    ones_col = jnp.where(lane == 0, 1.0, 0.0).astype(F32)

    xnb_scr[...] = _rms(x_ref[0], gmix_ref[...]).astype(BF16)

    def in_pieces(c):
        r = slice(c * L, (c + 1) * L)
        for p in range(0, proj_scr.shape[1], PROJ_PIECE):
            cs = slice(p, min(p + PROJ_PIECE, proj_scr.shape[1]))
            yield lambda cs=cs: proj_scr.__setitem__((r, cs), _dot(xnb_scr[r, :], win_ref[:, cs]))

    def out_pieces(c):
        r = slice(c * L, (c + 1) * L)
        for p in range(0, o_ref.shape[2], PROJ_PIECE):
            cs = slice(p, p + PROJ_PIECE)
            yield lambda cs=cs: o_ref.__setitem__((0, r, cs), x_ref[0, r, cs] + _dot(y_scr[r, :], wout_ref[:, cs]))

    fillers = []

    def tick(n=1):
        for _ in range(min(n, len(fillers))):
            fillers.pop(0)()

    for piece in in_pieces(0):
        piece()

    for c in range(n_chunks):
        rows = slice(c * L, (c + 1) * L)
        if c + 1 < n_chunks:
            fillers.extend(in_pieces(c + 1))
        if c > 0:
            fillers.extend(out_pieces(c - 1))
        pre = proj_scr[rows, gt0:gt0 + LANES] + bg_ref[...]
        capped = GATE_SOFTCAP * jnp.tanh(pre * (1.0 / GATE_SOFTCAP))
        log_sig = jnp.minimum(capped, 0.0) - jnp.log1p(jnp.exp(-jnp.abs(capped)))
        lg = jnp.where(lane < H, capped, log_sig)
        lg_t = lg.T[0:ROW_TILE, :]
        bcol = sum(_dot(tril, p) for p in _split3(lg))
        brow = sum(_dot(p, triu) for p in _split3(lg_t))

        hs = range(H)
        b_c = [bcol[:, H + h:H + h + 1] for h in hs]
        i_c = [lg[:, h:h + 1] for h in hs]
        b_r = [brow[H + h:H + h + 1, :] for h in hs]
        i_r = [lg_t[h:h + 1, :] for h in hs]
        m_prev = [m_scr[h, 0:1, 0:1] for h in hs]
        qk = [proj_scr[rows, qk0 + h * LANES:qk0 + (h + 1) * LANES] for h in hs]
        q = [jnp.where(lane < MLSTM_DK, qk[h] * (MLSTM_DK ** -0.5), 0.0) for h in hs]
        k = [jnp.where(lane < MLSTM_DK, pltpu.roll(qk[h], MLSTM_DK, axis=1), 0.0) for h in hs]
        qb = [q[h].astype(BF16) for h in hs]
        kb = [k[h].astype(BF16) for h in hs]
        tick()
        vaug = [jnp.concatenate([proj_scr[rows, v0 + h * MLSTM_DV:v0 + (h + 1) * MLSTM_DV], ones_col], axis=1)
                for h in hs]

        d = [jnp.where(causal, b_c[h] - b_r[h] + i_r[h], -jnp.inf) for h in hs]
        tick()
        inter = [b_c[h] + m_prev[h] for h in hs]
        m_t = [jnp.maximum(inter[h], jnp.max(d[h], axis=1, keepdims=True)) for h in hs]
        tick()
        w_intra = [jnp.exp(d[h] - m_t[h]) for h in hs]
        tick()
        w_inter = [jnp.exp(inter[h] - m_t[h]) for h in hs]
        s = [_dot_nt(qb[h], kb[h]) * w_intra[h] for h in hs]
        tick()
        ct = [ct_scr[h] for h in hs]
        na = [_dot(s[h].astype(BF16), vaug[h].astype(BF16)) + w_inter[h] * _dot(qb[h], ct[h].astype(BF16))
              for h in hs]
        hh = [na[h][:, :MLSTM_DV] / jnp.maximum(jnp.abs(na[h][:, MLSTM_DV:MLSTM_DV + 1]), jnp.exp(-m_t[h]))
              for h in hs]

        b_last = [b_c[h][L - 1:L, :] for h in hs]
        g_c = [b_last[h] - b_c[h] + i_c[h] for h in hs]
        m_new = [jnp.maximum(b_last[h] + m_prev[h], jnp.max(g_c[h], axis=0, keepdims=True)) for h in hs]
        wk = [jnp.exp(g_c[h] - m_new[h]) for h in hs]
        decay = [jnp.exp(b_last[h] + m_prev[h] - m_new[h]) for h in hs]
        tick()
        for h in hs:
            ct_scr[h] = decay[h] * ct[h] + _dot(k[h].T.astype(BF16), (wk[h] * vaug[h]).astype(BF16))
            m_scr[h] = jnp.broadcast_to(m_new[h], m_scr.shape[1:])

        hn = [_rms(hh[h], ghead_ref[:, h * MLSTM_DV:(h + 1) * MLSTM_DV]) for h in hs]
        tick()
        for h in hs:
            og = proj_scr[rows, og0 + h * MLSTM_DV:og0 + (h + 1) * MLSTM_DV]
            y_scr[rows, h * MLSTM_DV:(h + 1) * MLSTM_DV] = (jax.nn.sigmoid(og) * hn[h]).astype(BF16)

        gs = range(G)
        sl = [slice(g * GMLP_DG, (g + 1) * GMLP_DG) for g in gs]
        z = [_rms(_gelu_tanh(proj_scr[rows, gv0 + g * GMLP_DG:gv0 + (g + 1) * GMLP_DG]), gzv_ref[:, sl[g]]) for g in gs]
        mixed = [_dot(jnp.where(causal, wsp_ref[g], 0.0).astype(BF16), z[g].astype(BF16)) + bsp_ref[g] for g in gs]
        u = [_gelu_tanh(proj_scr[rows, gu0 + g * GMLP_DG:gu0 + (g + 1) * GMLP_DG]) for g in gs]
        tick()
        yg = [_rms(u[g] * mixed[g], gzo_ref[:, sl[g]]) for g in gs]
        tick()
        for g in gs:
            y_scr[rows, H * MLSTM_DV + g * GMLP_DG:H * MLSTM_DV + (g + 1) * GMLP_DG] = yg[g].astype(BF16)
        tick(len(fillers))
    for piece in out_pieces(n_chunks - 1):
        piece()


def _mixer(x, g_mix, w_in, b_gates, g_head, w_spatial, b_spatial, g_zv, g_zo, w_out):
    B, S, D = x.shape
    H, G, L = MLSTM_HEADS, GMLP_GROUPS, CHUNK
    qw, vw, gw = H * MLSTM_DK, H * MLSTM_DV, G * GMLP_DG
    ts = min(MIX_TOKENS, S)
    assert S % ts == 0 and ts % L == 0

    assert 2 * MLSTM_DK == LANES
    c = 0
    w_q = w_in[:, c:c + qw].reshape(D, H, MLSTM_DK); c += qw
    w_k = w_in[:, c:c + qw].reshape(D, H, MLSTM_DK); c += qw
    w_qk = jnp.concatenate([w_q, w_k], axis=2).reshape(D, H * LANES)
    w_v = w_in[:, c:c + vw]; c += vw
    w_o = w_in[:, c:c + vw]; c += vw
    w_g = jnp.pad(w_in[:, c:c + 2 * H], ((0, 0), (0, LANES - 2 * H))); c += 2 * H
    w_gu = w_in[:, c:c + gw]; c += gw
    w_gv = w_in[:, c:c + gw]; c += gw
    w_pack = jnp.concatenate([w_qk, w_v, w_o, w_gu, w_gv, w_g], axis=1).astype(BF16)
    hp = H * LANES
    cols = (0, hp, hp + vw, hp + 2 * vw, hp + 2 * vw + gw, hp + 2 * vw + 2 * gw)
    n_cols = w_pack.shape[1]
    bg = jnp.pad(b_gates, (0, LANES - 2 * H)).reshape(1, LANES)
    bsp = jnp.broadcast_to(b_spatial[:, :, None], (G, L, GMLP_DG))

    const = lambda *shape: pl.BlockSpec(shape, lambda b, j: (0,) * len(shape))
    return pl.pallas_call(
        functools.partial(_mixer_kernel, n_chunks=ts // L, cols=cols),
        out_shape=jax.ShapeDtypeStruct((B, S, D), F32),
        grid=(B, S // ts),
        in_specs=[
            pl.BlockSpec((1, ts, D), lambda b, j: (b, j, 0)),
            const(1, D), const(D, n_cols), const(1, LANES), const(1, vw),
            const(G, L, L), const(G, L, GMLP_DG), const(1, gw), const(1, gw), const(vw + gw, D),
        ],
        out_specs=pl.BlockSpec((1, ts, D), lambda b, j: (b, j, 0)),
        scratch_shapes=[
            pltpu.VMEM((ts, D), BF16),
            pltpu.VMEM((ts, n_cols), F32),
            pltpu.VMEM((ts, vw + gw), BF16),
            pltpu.VMEM((H, LANES, 2 * MLSTM_DV), F32),
            pltpu.VMEM((H, ROW_TILE, LANES), F32),
        ],
        compiler_params=pltpu.CompilerParams(
            dimension_semantics=("arbitrary", "arbitrary"), vmem_limit_bytes=VMEM_LIMIT),
        name="mixer",
    )(x, g_mix.reshape(1, D), w_pack, bg, g_head.reshape(1, vw), w_spatial, bsp,
      g_zv.reshape(1, gw), g_zo.reshape(1, gw), w_out.astype(BF16))


def _kv_kernel(mem_ref, g_ref, w_ref, k_ref, v_ref):
    D = mem_ref.shape[-1]
    kv = _dot(_rms(mem_ref[0], g_ref[...]).astype(BF16), w_ref[...])
    k_ref[0] = kv[:, :D].astype(BF16)
    v_ref[0] = kv[:, D:].astype(BF16)


def _kv_proj(mem, g_mem, w_kv):
    B, M, D = mem.shape
    return pl.pallas_call(
        _kv_kernel,
        out_shape=(jax.ShapeDtypeStruct((B, M, D), BF16), jax.ShapeDtypeStruct((B, M, D), BF16)),
        grid=(B,),
        in_specs=[
            pl.BlockSpec((1, M, D), lambda b: (b, 0, 0)),
            pl.BlockSpec((1, D), lambda b: (0, 0)),
            pl.BlockSpec((D, 2 * D), lambda b: (0, 0)),
        ],
        out_specs=(pl.BlockSpec((1, M, D), lambda b: (b, 0, 0)), pl.BlockSpec((1, M, D), lambda b: (b, 0, 0))),
        compiler_params=pltpu.CompilerParams(dimension_semantics=("arbitrary",), vmem_limit_bytes=VMEM_LIMIT),
        name="kv_proj",
    )(mem, g_mem.reshape(1, D), w_kv.astype(BF16))


def _xattn_kernel(x_ref, gx_ref, wq_ref, k_ref, v_ref, wxo_ref, gmoe_ref, wr_ref, br_ref,
                  x2_ref, xn_ref, idx_ref, gate_ref, rank_ref, cnt_ref, xnb_scr, q_scr, o_scr, cnt_scr):
    ts, D = x_ref.shape[1], x_ref.shape[2]
    hd = D // XATTN_HEADS
    E = wr_ref.shape[0]

    @pl.when((pl.program_id(0) == 0) & (pl.program_id(1) == 0))
    def _():
        cnt_scr[...] = jnp.zeros_like(cnt_scr)

    xnb_scr[...] = _rms(x_ref[0], gx_ref[...]).astype(BF16)

    half = ts // 2
    halves = [slice(0, half), slice(half, ts)]
    pieces = [slice(p, p + PROJ_PIECE) for p in range(0, D, PROJ_PIECE)]

    def q_pieces(r):
        for cs in pieces:
            yield lambda cs=cs: q_scr.__setitem__(
                (r, cs), (_dot(xnb_scr[r, :], wq_ref[:, cs]) * (hd ** -0.5)).astype(BF16))

    def xo_pieces(r):
        for cs in pieces:
            yield lambda cs=cs: x2_ref.__setitem__(
                (0, r, cs), x_ref[0, r, cs] + _dot(o_scr[r, :], wxo_ref[:, cs]))

    fillers = []

    def tick(n=1):
        for _ in range(min(n, len(fillers))):
            fillers.pop(0)()

    def attention(r):
        hs = range(XATTN_HEADS)
        sl = [slice(h * hd, (h + 1) * hd) for h in hs]
        s = [_dot_nt(q_scr[r, sl[h]], k_ref[0, :, sl[h]]) for h in hs]
        tick()
        p = [jnp.exp(s[h] - jnp.max(s[h], axis=1, keepdims=True)) for h in hs]
        tick()
        p = [p[h] / jnp.sum(p[h], axis=1, keepdims=True) for h in hs]
        for h in hs:
            o_scr[r, sl[h]] = _dot(p[h].astype(BF16), v_ref[0, :, sl[h]]).astype(BF16)
        tick(len(fillers))

    for piece in q_pieces(halves[0]):
        piece()
    fillers.extend(q_pieces(halves[1]))
    attention(halves[0])
    fillers.extend(xo_pieces(halves[0]))
    attention(halves[1])
    for piece in xo_pieces(halves[1]):
        piece()
    x2 = x2_ref[0]

    xn = _rms(x2, gmoe_ref[...])
    xh, xm, _ = _split3(xn)
    wh, wm, _ = _split3(wr_ref[...])
    logits = _dot_nt(wh, xh) + (_dot_nt(wh, xm) + _dot_nt(wm, xh)) + br_ref[...]

    e_iota = lax.broadcasted_iota(jnp.int32, (E, ts), 0)
    work = logits
    tops, idxs, hots = [], [], []
    for _ in range(TOP_K):
        m = jnp.max(work, axis=0, keepdims=True)
        i = jnp.min(jnp.where(work == m, e_iota, E), axis=0, keepdims=True)
        hot = e_iota == i
        work = jnp.where(hot, -jnp.inf, work)
        tops.append(m); idxs.append(i); hots.append(hot)
    ex = [jnp.exp(t - tops[0]) for t in tops]
    tot = ex[0] + ex[1] + ex[2] + ex[3]
    gate_ref[...] = jnp.concatenate([e / tot for e in ex], axis=0)
    idx_ref[...] = jnp.concatenate(idxs, axis=0)

    hot_all = jnp.where(hots[0] | hots[1] | hots[2] | hots[3], 1.0, 0.0)
    r = lax.broadcasted_iota(jnp.int32, (ts, ts), 0)
    c = lax.broadcasted_iota(jnp.int32, (ts, ts), 1)
    before = jnp.where(r < c, 1.0, 0.0).astype(BF16)
    base = cnt_scr[:, 0:1] + _dot(hot_all.astype(BF16), before)
    ranks = [jnp.sum(jnp.where(hot, base, 0.0), axis=0, keepdims=True) for hot in hots]
    rank_ref[...] = jnp.concatenate(ranks, axis=0).astype(jnp.int32)
    cnt_new = cnt_scr[...] + jnp.sum(hot_all, axis=1, keepdims=True)
    cnt_scr[...] = cnt_new
    cnt_ref[...] = cnt_new
    _store_row_tiles(xn_ref, xn)


def _xattn_router(x, k, v, g_xattn, w_q, w_xo, g_moe, w_router, b_router):
    B, S, D = x.shape
    M = k.shape[1]
    E = w_router.shape[1]
    ts = min(XATTN_TOKENS, S)
    assert S % ts == 0
    nj = S // ts
    T = B * S
    const = lambda *shape: pl.BlockSpec(shape, lambda b, j: (0,) * len(shape))
    tok = lambda rows: pl.BlockSpec((rows, ts), lambda b, j: (0, b * nj + j))
    return pl.pallas_call(
        _xattn_kernel,
        out_shape=(
            jax.ShapeDtypeStruct((B, S, D), F32),
            jax.ShapeDtypeStruct((T * ROW_TILE, LANES), F32),
            jax.ShapeDtypeStruct((TOP_K, T), jnp.int32),
            jax.ShapeDtypeStruct((TOP_K, T), F32),
            jax.ShapeDtypeStruct((TOP_K, T), jnp.int32),
            jax.ShapeDtypeStruct((E, LANES), F32),
        ),
        grid=(B, nj),
        in_specs=[
            pl.BlockSpec((1, ts, D), lambda b, j: (b, j, 0)),
            const(1, D), const(D, D),
            pl.BlockSpec((1, M, D), lambda b, j: (b, 0, 0)),
            pl.BlockSpec((1, M, D), lambda b, j: (b, 0, 0)),
            const(D, D), const(1, D), const(E, D), const(E, 1),
        ],
        out_specs=(
            pl.BlockSpec((1, ts, D), lambda b, j: (b, j, 0)),
            pl.BlockSpec((ts * ROW_TILE, LANES), lambda b, j: (b * nj + j, 0)),
            tok(TOP_K), tok(TOP_K), tok(TOP_K),
            const(E, LANES),
        ),
        scratch_shapes=[pltpu.VMEM((ts, D), BF16), pltpu.VMEM((ts, D), BF16), pltpu.VMEM((ts, D), BF16),
                        pltpu.VMEM((E, LANES), F32)],
        compiler_params=pltpu.CompilerParams(
            dimension_semantics=("arbitrary", "arbitrary"), vmem_limit_bytes=VMEM_LIMIT),
        name="xattn_router",
    )(x, g_xattn.reshape(1, D), w_q.astype(BF16), k, v, w_xo.astype(BF16), g_moe.reshape(1, D),
      w_router.T, b_router.reshape(E, 1))


def _dispatch_kernel(dest_ref, cnt_ref, pend_ref, xn_ref, xs_ref, zero_scr, sem, zsem, *, n_experts):
    tt = xn_ref.shape[0] // ROW_TILE
    R = zero_scr.shape[0] // ROW_TILE

    @pl.when(pl.program_id(0) == 0)
    def _():
        zero_scr[...] = jnp.zeros_like(zero_scr)

        def fill_copy(e):
            start = pl.multiple_of((pend_ref[e] - R) * ROW_TILE, R * ROW_TILE)
            return pltpu.make_async_copy(zero_scr, xs_ref.at[pl.ds(start, R * ROW_TILE)], zsem)

        def start_fill(e, carry):
            @pl.when(cnt_ref[e] > 0)
            def _():
                fill_copy(e).start()
            return carry

        def wait_fill(e, carry):
            @pl.when(cnt_ref[e] > 0)
            def _():
                fill_copy(e).wait()
            return carry

        lax.fori_loop(0, n_experts, start_fill, 0)
        lax.fori_loop(0, n_experts, wait_fill, 0)

    def issue(grp, carry):
        t0 = pl.multiple_of(grp * ROW_UNROLL, ROW_UNROLL)
        for kk in range(TOP_K):
            for u in range(ROW_UNROLL):
                dst = pl.multiple_of(dest_ref[0, 0, kk * tt + t0 + u] * ROW_TILE, ROW_TILE)
                row = pl.multiple_of((t0 + u) * ROW_TILE, ROW_TILE)
                pltpu.make_async_copy(xn_ref.at[pl.ds(row, ROW_TILE)], xs_ref.at[pl.ds(dst, ROW_TILE)],
                                      sem).start(priority=u % 2)
        return carry

    lax.fori_loop(0, tt // ROW_UNROLL, issue, 0)
    for kk in range(TOP_K):
        pltpu.make_async_copy(xn_ref, xs_ref.at[pl.ds(0, tt * ROW_TILE)], sem).wait()


def _dispatch(xn, dest, counts, pad_end, n_slots):
    T = dest.shape[1]
    tt = min(DISPATCH_TOKENS, T)
    assert T % tt == 0
    E = counts.shape[0]
    dest = dest.reshape(TOP_K, T // tt, tt).transpose(1, 0, 2).reshape(T // tt, 1, TOP_K * tt)
    return pl.pallas_call(
        functools.partial(_dispatch_kernel, n_experts=E),
        out_shape=jax.ShapeDtypeStruct((n_slots * ROW_TILE, LANES), F32),
        grid=(T // tt,),
        in_specs=[
            pl.BlockSpec((1, 1, TOP_K * tt), lambda i: (i, 0, 0), memory_space=pltpu.SMEM),
            pl.BlockSpec(memory_space=pltpu.SMEM),
            pl.BlockSpec(memory_space=pltpu.SMEM),
            pl.BlockSpec((tt * ROW_TILE, LANES), lambda i: (i, 0)),
        ],
        out_specs=pl.BlockSpec(memory_space=pl.ANY),
        scratch_shapes=[pltpu.VMEM((MOE_ROWS * ROW_TILE, LANES), F32), pltpu.SemaphoreType.DMA,
                        pltpu.SemaphoreType.DMA],
        compiler_params=pltpu.CompilerParams(
            dimension_semantics=("arbitrary",), vmem_limit_bytes=VMEM_LIMIT, has_side_effects=True),
        name="dispatch",
    )(dest, counts, pad_end, xn)


def _expert_kernel(be_ref, nu_ref, reg_ref, nxt_ref, xs_ref, wg_ref, bg_ref, wu_ref, bu_ref, wd_ref, bd_ref, y_ref,
                   wf_scr, wg_scr, wu_scr, wd_scr, sem):
    j = pl.program_id(0)
    active = j < nu_ref[0]
    w_hbm = (wg_ref, wu_ref, wd_ref)

    def fetch(e, slot):
        return [pltpu.make_async_copy(w_hbm[i].at[e], wf_scr.at[slot, i], sem.at[slot, i]) for i in range(3)]

    @pl.when(active & ((j == 0) | (be_ref[j] != be_ref[jnp.maximum(j - 1, 0)])))
    def _():
        slot = reg_ref[j] % 2

        @pl.when(j == 0)
        def _():
            for cp in fetch(be_ref[0], 0):
                cp.start()

        for cp in fetch(be_ref[j], slot):
            cp.wait()
        wg_scr[...] = wf_scr[slot, 0].astype(BF16)
        wu_scr[...] = wf_scr[slot, 1].astype(BF16)
        wd_scr[...] = wf_scr[slot, 2].astype(BF16)

        @pl.when(nxt_ref[j] >= 0)
        def _():
            for cp in fetch(nxt_ref[j], 1 - slot):
                cp.start()

    @pl.when(active)
    def _():
        xb = _load_row_tiles(xs_ref, xs_ref.shape[0] // ROW_TILE).astype(BF16)
        g = jnp.minimum(_dot(xb, wg_scr[...]) + bg_ref[0], SWIGLU_LIMIT)
        u = jnp.clip(_dot(xb, wu_scr[...]) + bu_ref[0], -SWIGLU_LIMIT, SWIGLU_LIMIT)
        hdn = (u + 1.0) * (g * jax.nn.sigmoid(SWIGLU_ALPHA * g))
        _store_row_tiles(y_ref, _dot(hdn.astype(BF16), wd_scr[...]) + bd_ref[0])


def _experts(xs, block_e, n_used, region, next_e, w_gate, b_gate, w_up, b_up, w_down, b_down):
    E, D, F = w_gate.shape
    assert D == ROW_TILE * LANES and F == D, "the two weight staging slots hold (D, F) and (F, D) alike"
    n_slots = xs.shape[0] // ROW_TILE
    R = MOE_ROWS
    n_blocks = n_slots // R
    blk = lambda j, be, nu, rg, nx: (jnp.minimum(j, nu[0] - 1), 0)
    wsel = lambda j, be, nu, rg, nx: (be[jnp.minimum(j, nu[0] - 1)], 0, 0)
    hbm = pl.BlockSpec(memory_space=pl.ANY)
    grid_spec = pltpu.PrefetchScalarGridSpec(
        num_scalar_prefetch=4,
        grid=(n_blocks,),
        in_specs=[
            pl.BlockSpec((R * ROW_TILE, LANES), blk),
            hbm, pl.BlockSpec((1, 1, F), wsel),
            hbm, pl.BlockSpec((1, 1, F), wsel),
            hbm, pl.BlockSpec((1, 1, D), wsel),
        ],
        out_specs=pl.BlockSpec((R * ROW_TILE, LANES), blk),
        scratch_shapes=[pltpu.VMEM((2, 3, D, F), F32), pltpu.VMEM((D, F), BF16), pltpu.VMEM((D, F), BF16),
                        pltpu.VMEM((F, D), BF16), pltpu.SemaphoreType.DMA((2, 3))],
    )
    return pl.pallas_call(
        _expert_kernel,
        out_shape=jax.ShapeDtypeStruct((n_slots * ROW_TILE, LANES), F32),
        grid_spec=grid_spec,
        compiler_params=pltpu.CompilerParams(dimension_semantics=("arbitrary",), vmem_limit_bytes=VMEM_LIMIT),
        name="experts",
    )(block_e, n_used, region, next_e, xs, w_gate, b_gate.reshape(E, 1, F), w_up, b_up.reshape(E, 1, F),
      w_down, b_down.reshape(E, 1, D))


def _combine_kernel(dcur_ref, dnext_ref, x_ref, gate_ref, gfin_ref, ys_ref, o_ref, buf, sem):
    tc = x_ref.shape[0]
    i = pl.program_id(0)
    slot = i % 2

    def issue_all(d_ref, into):
        def issue(grp, carry):
            t0 = pl.multiple_of(grp * ROW_UNROLL, ROW_UNROLL)
            for kk in range(TOP_K):
                for u in range(ROW_UNROLL):
                    src = pl.multiple_of(d_ref[kk, t0 + u] * ROW_TILE, ROW_TILE)
                    row = pl.multiple_of((t0 + u) * ROW_TILE, ROW_TILE)
                    pltpu.make_async_copy(ys_ref.at[pl.ds(src, ROW_TILE)], buf.at[into, kk, pl.ds(row, ROW_TILE)],
                                          sem.at[into]).start(priority=u % 2)
            return carry

        lax.fori_loop(0, tc // ROW_UNROLL, issue, 0)

    def wait_slot(s):
        for kk in range(TOP_K):
            pltpu.make_async_copy(ys_ref.at[pl.ds(0, tc * ROW_TILE)], buf.at[s, kk], sem.at[s]).wait()

    @pl.when(i == 0)
    def _():
        issue_all(dcur_ref, 0)

    wait_slot(slot)
    other = 1 - slot
    for t in range(tc):
        for kk in range(TOP_K):
            src = pl.multiple_of(dnext_ref[kk, t] * ROW_TILE, ROW_TILE)
            pltpu.make_async_copy(ys_ref.at[pl.ds(src, ROW_TILE)], buf.at[other, kk, pl.ds(t * ROW_TILE, ROW_TILE)],
                                  sem.at[other]).start(priority=t % 2)
    for r0 in range(0, tc, COMBINE_ROWS):
        rows = slice(r0, r0 + COMBINE_ROWS)
        acc = x_ref[rows, :]
        for kk in range(TOP_K):
            acc = acc + gate_ref[rows, kk:kk + 1] * _load_row_tiles(buf.at[slot, kk], COMBINE_ROWS, r0)
        o_ref[rows, :] = _rms(acc, gfin_ref[...])

    @pl.when(i == pl.num_programs(0) - 1)
    def _():
        wait_slot(other)


def _combine(x2, gate_t, dest, ys, g_final):
    T, D = x2.shape
    tc = min(COMBINE_TOKENS, T)
    assert T % tc == 0
    nb = T // tc
    return pl.pallas_call(
        _combine_kernel,
        out_shape=jax.ShapeDtypeStruct((T, D), F32),
        grid=(nb,),
        in_specs=[
            pl.BlockSpec((TOP_K, tc), lambda i: (0, i), memory_space=pltpu.SMEM),
            pl.BlockSpec((TOP_K, tc), lambda i: (0, jnp.minimum(i + 1, nb - 1)), memory_space=pltpu.SMEM),
            pl.BlockSpec((tc, D), lambda i: (i, 0)),
            pl.BlockSpec((tc, TOP_K), lambda i: (i, 0)),
            pl.BlockSpec((1, D), lambda i: (0, 0)),
            pl.BlockSpec(memory_space=pl.ANY),
        ],
        out_specs=pl.BlockSpec((tc, D), lambda i: (i, 0)),
        scratch_shapes=[pltpu.VMEM((2, TOP_K, tc * ROW_TILE, LANES), F32), pltpu.SemaphoreType.DMA((2,))],
        compiler_params=pltpu.CompilerParams(dimension_semantics=("arbitrary",), vmem_limit_bytes=VMEM_LIMIT),
        name="combine",
    )(dest, dest, x2, gate_t, g_final.reshape(1, D), ys)


def _moe(x2, xn, idx, gate, rank, cnt, w_gate, b_gate, w_up, b_up, w_down, b_down, g_final):
    T = x2.shape[0]
    E = w_gate.shape[0]
    R = MOE_ROWS
    n_blocks = -(-T * TOP_K // R) + E
    n_slots = n_blocks * R
    counts = cnt[:, 0].astype(jnp.int32)
    padded = (counts + R - 1) // R * R
    pad_end = jnp.cumsum(padded)
    pad_start = pad_end - padded
    dest = rank
    for e in range(E):
        dest = dest + jnp.where(idx == e, pad_start[e], 0)
    n_used = (pad_end[-1:] // R).astype(jnp.int32)
    block_e = jnp.minimum(jnp.sum(jnp.arange(n_blocks)[:, None] * R >= pad_end[None, :], axis=1), E - 1)
    block_e = block_e.astype(jnp.int32)
    blocks = jnp.arange(n_blocks)
    first = (blocks < n_used[0]) & ((blocks == 0) | (block_e != jnp.roll(block_e, 1)))
    region = (jnp.cumsum(first) - 1).astype(jnp.int32)
    later = jnp.where((counts[None, :] > 0) & (jnp.arange(E)[None, :] > block_e[:, None]), jnp.arange(E)[None, :], E)
    next_e = jnp.min(later, axis=1)
    next_e = jnp.where(next_e == E, -1, next_e).astype(jnp.int32)

    xs = _dispatch(xn, dest, counts, pad_end.astype(jnp.int32), n_slots)
    ys = _experts(xs, block_e, n_used, region, next_e, w_gate, b_gate, w_up, b_up, w_down, b_down)
    return _combine(x2, gate.T, dest, ys, g_final)


def kernel(x, mem, g_mix, w_in, b_gates, g_mlstm_head, w_spatial, b_spatial, g_gmlp_v, g_gmlp_out, w_out,
           g_xattn, g_mem, w_q, w_kv, w_xo, g_moe, w_router, b_router, w_gate, b_gate, w_up, b_up,
           w_down, b_down, g_final):
    B, S, D = x.shape
    assert g_mix.shape[0] == 1, "the combine kernel fuses the closing norm, so exactly one layer is supported"
    x1 = _mixer(x, g_mix[0], w_in[0], b_gates[0], g_mlstm_head[0], w_spatial[0], b_spatial[0],
                g_gmlp_v[0], g_gmlp_out[0], w_out[0])
    k, v = _kv_proj(mem, g_mem[0], w_kv[0])
    x2, xn, idx, gate, rank, cnt = _xattn_router(x1, k, v, g_xattn[0], w_q[0], w_xo[0], g_moe[0],
                                                 w_router[0], b_router[0])
    out = _moe(x2.reshape(B * S, D), xn, idx, gate, rank, cnt, w_gate[0], b_gate[0], w_up[0], b_up[0],
               w_down[0], b_down[0], g_final)
    return out.reshape(B, S, D)
```

```python
import functools

import jax
import jax.numpy as jnp
from jax import lax
from jax.experimental import pallas as pl
from jax.experimental.pallas import tpu as pltpu

F32 = jnp.float32
BF16 = jnp.bfloat16

MLSTM_HEADS = 4
MLSTM_DK = 64
MLSTM_DV = 128
CHUNK = 128
GMLP_GROUPS = 4
GMLP_DG = 128
XATTN_HEADS = 4
TOP_K = 4
GATE_SOFTCAP = 15.0
SWIGLU_LIMIT = 7.0
SWIGLU_ALPHA = 1.702
NORM_EPS = 1e-6

LANES = 128
ROW_TILE = 8

MIX_TOKENS = 512
PROJ_PIECE = 512
XATTN_TOKENS = 512
MOE_ROWS = 512
DISPATCH_TOKENS = 2048
COMBINE_TOKENS = 256
COMBINE_ROWS = 32
ROW_UNROLL = 16
VMEM_LIMIT = 52 * 1024 * 1024


def _rms(x, g):
    return x * lax.rsqrt(jnp.mean(x * x, axis=-1, keepdims=True) + NORM_EPS) * g


def _gelu_tanh(x):
    return 0.5 * x * (1.0 + jnp.tanh(0.7978845608028654 * (x + 0.044715 * x * x * x)))


def _dot(a, b):
    return jnp.dot(a, b, preferred_element_type=F32)


def _dot_nt(a, b):
    return lax.dot_general(a, b, (((1,), (1,)), ((), ())), preferred_element_type=F32)


def _load_row_tiles(ref, rows, first=0):
    return jnp.concatenate(
        [ref[pl.ds(first * ROW_TILE + j, rows, stride=ROW_TILE), :] for j in range(ROW_TILE)], axis=1)


def _store_row_tiles(ref, val):
    for j in range(ROW_TILE):
        ref[pl.ds(j, val.shape[0], stride=ROW_TILE), :] = val[:, j * LANES:(j + 1) * LANES]


def _split3(x):
    hi = x.astype(BF16)
    r1 = x - hi.astype(F32)
    mid = r1.astype(BF16)
    lo = (r1 - mid.astype(F32)).astype(BF16)
    return hi, mid, lo


def _mixer_kernel(x_ref, gmix_ref, win_ref, bg_ref, ghead_ref, wsp_ref, bsp_ref, gzv_ref, gzo_ref, wout_ref,
                  o_ref, xnb_scr, proj_scr, y_scr, ct_scr, m_scr, *, n_chunks, cols):
    qk0, v0, og0, gu0, gv0, gt0 = cols
    H, G, L = MLSTM_HEADS, GMLP_GROUPS, CHUNK

    @pl.when(pl.program_id(1) == 0)
    def _():
        ct_scr[...] = jnp.zeros_like(ct_scr)
        m_scr[...] = jnp.zeros_like(m_scr)

    row = lax.broadcasted_iota(jnp.int32, (L, L), 0)
    col = lax.broadcasted_iota(jnp.int32, (L, L), 1)
    causal = row >= col
    tril = jnp.where(causal, 1.0, 0.0).astype(BF16)
    triu = jnp.where(row <= col, 1.0, 0.0).astype(BF16)
    lane = lax.broadcasted_iota(jnp.int32, (L, LANES), 1)
    ones_col = jnp.where(lane == 0, 1.0, 0.0).astype(F32)

    xnb_scr[...] = _rms(x_ref[0], gmix_ref[...]).astype(BF16)

    def in_pieces(c):
        r = slice(c * L, (c + 1) * L)
        for p in range(0, proj_scr.shape[1], PROJ_PIECE):
            cs = slice(p, min(p + PROJ_PIECE, proj_scr.shape[1]))
            yield lambda cs=cs: proj_scr.__setitem__((r, cs), _dot(xnb_scr[r, :], win_ref[:, cs]))

    def out_pieces(c):
        r = slice(c * L, (c + 1) * L)
        for p in range(0, o_ref.shape[2], PROJ_PIECE):
            cs = slice(p, p + PROJ_PIECE)
            yield lambda cs=cs: o_ref.__setitem__((0, r, cs), x_ref[0, r, cs] + _dot(y_scr[r, :], wout_ref[:, cs]))

    fillers = []

    def tick(n=1):
        for _ in range(min(n, len(fillers))):
            fillers.pop(0)()

    for piece in in_pieces(0):
        piece()

    for c in range(n_chunks):
        rows = slice(c * L, (c + 1) * L)
        if c + 1 < n_chunks:
            fillers.extend(in_pieces(c + 1))
        if c > 0:
            fillers.extend(out_pieces(c - 1))
        pre = proj_scr[rows, gt0:gt0 + LANES] + bg_ref[...]
        capped = GATE_SOFTCAP * jnp.tanh(pre * (1.0 / GATE_SOFTCAP))
        log_sig = jnp.minimum(capped, 0.0) - jnp.log1p(jnp.exp(-jnp.abs(capped)))
        lg = jnp.where(lane < H, capped, log_sig)
        lg_t = lg.T[0:ROW_TILE, :]
        bcol = sum(_dot(tril, p) for p in _split3(lg))
        brow = sum(_dot(p, triu) for p in _split3(lg_t))

        hs = range(H)
        b_c = [bcol[:, H + h:H + h + 1] for h in hs]
        i_c = [lg[:, h:h + 1] for h in hs]
        b_r = [brow[H + h:H + h + 1, :] for h in hs]
        i_r = [lg_t[h:h + 1, :] for h in hs]
        m_prev = [m_scr[h, 0:1, 0:1] for h in hs]
        qk = [proj_scr[rows, qk0 + h * LANES:qk0 + (h + 1) * LANES] for h in hs]
        q = [jnp.where(lane < MLSTM_DK, qk[h] * (MLSTM_DK ** -0.5), 0.0) for h in hs]
        k = [jnp.where(lane < MLSTM_DK, pltpu.roll(qk[h], MLSTM_DK, axis=1), 0.0) for h in hs]
        qb = [q[h].astype(BF16) for h in hs]
        kb = [k[h].astype(BF16) for h in hs]
        tick()
        vaug = [jnp.concatenate([proj_scr[rows, v0 + h * MLSTM_DV:v0 + (h + 1) * MLSTM_DV], ones_col], axis=1)
                for h in hs]

        d = [jnp.where(causal, b_c[h] - b_r[h] + i_r[h], -jnp.inf) for h in hs]
        tick()
        inter = [b_c[h] + m_prev[h] for h in hs]
        m_t = [jnp.maximum(inter[h], jnp.max(d[h], axis=1, keepdims=True)) for h in hs]
        tick()
        w_intra = [jnp.exp(d[h] - m_t[h]) for h in hs]
        tick()
        w_inter = [jnp.exp(inter[h] - m_t[h]) for h in hs]
        s = [_dot_nt(qb[h], kb[h]) * w_intra[h] for h in hs]
        tick()
        ct = [ct_scr[h] for h in hs]
        na = [_dot(s[h].astype(BF16), vaug[h].astype(BF16)) + w_inter[h] * _dot(qb[h], ct[h].astype(BF16))
              for h in hs]
        hh = [na[h][:, :MLSTM_DV] / jnp.maximum(jnp.abs(na[h][:, MLSTM_DV:MLSTM_DV + 1]), jnp.exp(-m_t[h]))
              for h in hs]

        b_last = [b_c[h][L - 1:L, :] for h in hs]
        g_c = [b_last[h] - b_c[h] + i_c[h] for h in hs]
        m_new = [jnp.maximum(b_last[h] + m_prev[h], jnp.max(g_c[h], axis=0, keepdims=True)) for h in hs]
        wk = [jnp.exp(g_c[h] - m_new[h]) for h in hs]
        decay = [jnp.exp(b_last[h] + m_prev[h] - m_new[h]) for h in hs]
        tick()
        for h in hs:
            ct_scr[h] = decay[h] * ct[h] + _dot(k[h].T.astype(BF16), (wk[h] * vaug[h]).astype(BF16))
            m_scr[h] = jnp.broadcast_to(m_new[h], m_scr.shape[1:])

        hn = [_rms(hh[h], ghead_ref[:, h * MLSTM_DV:(h + 1) * MLSTM_DV]) for h in hs]
        tick()
        for h in hs:
            og = proj_scr[rows, og0 + h * MLSTM_DV:og0 + (h + 1) * MLSTM_DV]
            y_scr[rows, h * MLSTM_DV:(h + 1) * MLSTM_DV] = (jax.nn.sigmoid(og) * hn[h]).astype(BF16)

        gs = range(G)
        sl = [slice(g * GMLP_DG, (g + 1) * GMLP_DG) for g in gs]
        z = [_rms(_gelu_tanh(proj_scr[rows, gv0 + g * GMLP_DG:gv0 + (g + 1) * GMLP_DG]), gzv_ref[:, sl[g]]) for g in gs]
        mixed = [_dot(jnp.where(causal, wsp_ref[g], 0.0).astype(BF16), z[g].astype(BF16)) + bsp_ref[g] for g in gs]
        u = [_gelu_tanh(proj_scr[rows, gu0 + g * GMLP_DG:gu0 + (g + 1) * GMLP_DG]) for g in gs]
        tick()
        yg = [_rms(u[g] * mixed[g], gzo_ref[:, sl[g]]) for g in gs]
        tick()
        for g in gs:
            y_scr[rows, H * MLSTM_DV + g * GMLP_DG:H * MLSTM_DV + (g + 1) * GMLP_DG] = yg[g].astype(BF16)
        tick(len(fillers))
    for piece in out_pieces(n_chunks - 1):
        piece()


def _mixer(x, g_mix, w_in, b_gates, g_head, w_spatial, b_spatial, g_zv, g_zo, w_out):
    B, S, D = x.shape
    H, G, L = MLSTM_HEADS, GMLP_GROUPS, CHUNK
    qw, vw, gw = H * MLSTM_DK, H * MLSTM_DV, G * GMLP_DG
    ts = min(MIX_TOKENS, S)
    assert S % ts == 0 and ts % L == 0

    assert 2 * MLSTM_DK == LANES
    c = 0
    w_q = w_in[:, c:c + qw].reshape(D, H, MLSTM_DK); c += qw
    w_k = w_in[:, c:c + qw].reshape(D, H, MLSTM_DK); c += qw
    w_qk = jnp.concatenate([w_q, w_k], axis=2).reshape(D, H * LANES)
    w_v = w_in[:, c:c + vw]; c += vw
    w_o = w_in[:, c:c + vw]; c += vw
    w_g = jnp.pad(w_in[:, c:c + 2 * H], ((0, 0), (0, LANES - 2 * H))); c += 2 * H
    w_gu = w_in[:, c:c + gw]; c += gw
    w_gv = w_in[:, c:c + gw]; c += gw
    w_pack = jnp.concatenate([w_qk, w_v, w_o, w_gu, w_gv, w_g], axis=1).astype(BF16)
    hp = H * LANES
    cols = (0, hp, hp + vw, hp + 2 * vw, hp + 2 * vw + gw, hp + 2 * vw + 2 * gw)
    n_cols = w_pack.shape[1]
    bg = jnp.pad(b_gates, (0, LANES - 2 * H)).reshape(1, LANES)
    bsp = jnp.broadcast_to(b_spatial[:, :, None], (G, L, GMLP_DG))

    const = lambda *shape: pl.BlockSpec(shape, lambda b, j: (0,) * len(shape))
    return pl.pallas_call(
        functools.partial(_mixer_kernel, n_chunks=ts // L, cols=cols),
        out_shape=jax.ShapeDtypeStruct((B, S, D), F32),
        grid=(B, S // ts),
        in_specs=[
            pl.BlockSpec((1, ts, D), lambda b, j: (b, j, 0)),
            const(1, D), const(D, n_cols), const(1, LANES), const(1, vw),
            const(G, L, L), const(G, L, GMLP_DG), const(1, gw), const(1, gw), const(vw + gw, D),
        ],
        out_specs=pl.BlockSpec((1, ts, D), lambda b, j: (b, j, 0)),
        scratch_shapes=[
            pltpu.VMEM((ts, D), BF16),
            pltpu.VMEM((ts, n_cols), F32),
            pltpu.VMEM((ts, vw + gw), BF16),
            pltpu.VMEM((H, LANES, 2 * MLSTM_DV), F32),
            pltpu.VMEM((H, ROW_TILE, LANES), F32),
        ],
        compiler_params=pltpu.CompilerParams(
            dimension_semantics=("arbitrary", "arbitrary"), vmem_limit_bytes=VMEM_LIMIT),
        name="mixer",
    )(x, g_mix.reshape(1, D), w_pack, bg, g_head.reshape(1, vw), w_spatial, bsp,
      g_zv.reshape(1, gw), g_zo.reshape(1, gw), w_out.astype(BF16))


def _kv_kernel(mem_ref, g_ref, w_ref, k_ref, v_ref):
    D = mem_ref.shape[-1]
    kv = _dot(_rms(mem_ref[0], g_ref[...]).astype(BF16), w_ref[...])
    k_ref[0] = kv[:, :D].astype(BF16)
    v_ref[0] = kv[:, D:].astype(BF16)


def _kv_proj(mem, g_mem, w_kv):
    B, M, D = mem.shape
    return pl.pallas_call(
        _kv_kernel,
        out_shape=(jax.ShapeDtypeStruct((B, M, D), BF16), jax.ShapeDtypeStruct((B, M, D), BF16)),
        grid=(B,),
        in_specs=[
            pl.BlockSpec((1, M, D), lambda b: (b, 0, 0)),
            pl.BlockSpec((1, D), lambda b: (0, 0)),
            pl.BlockSpec((D, 2 * D), lambda b: (0, 0)),
        ],
        out_specs=(pl.BlockSpec((1, M, D), lambda b: (b, 0, 0)), pl.BlockSpec((1, M, D), lambda b: (b, 0, 0))),
        compiler_params=pltpu.CompilerParams(dimension_semantics=("arbitrary",), vmem_limit_bytes=VMEM_LIMIT),
        name="kv_proj",
    )(mem, g_mem.reshape(1, D), w_kv.astype(BF16))


def _xattn_kernel(x_ref, gx_ref, wq_ref, k_ref, v_ref, wxo_ref, gmoe_ref, wr_ref, br_ref,
                  x2_ref, xn_ref, idx_ref, gate_ref, rank_ref, cnt_ref, xnb_scr, q_scr, o_scr, cnt_scr):
    ts, D = x_ref.shape[1], x_ref.shape[2]
    hd = D // XATTN_HEADS
    E = wr_ref.shape[0]

    @pl.when((pl.program_id(0) == 0) & (pl.program_id(1) == 0))
    def _():
        cnt_scr[...] = jnp.zeros_like(cnt_scr)

    xnb_scr[...] = _rms(x_ref[0], gx_ref[...]).astype(BF16)

    half = ts // 2
    halves = [slice(0, half), slice(half, ts)]
    pieces = [slice(p, p + PROJ_PIECE) for p in range(0, D, PROJ_PIECE)]

    def q_pieces(r):
        for cs in pieces:
            yield lambda cs=cs: q_scr.__setitem__(
                (r, cs), (_dot(xnb_scr[r, :], wq_ref[:, cs]) * (hd ** -0.5)).astype(BF16))

    def xo_pieces(r):
        for cs in pieces:
            yield lambda cs=cs: x2_ref.__setitem__(
                (0, r, cs), x_ref[0, r, cs] + _dot(o_scr[r, :], wxo_ref[:, cs]))

    fillers = []

    def tick(n=1):
        for _ in range(min(n, len(fillers))):
            fillers.pop(0)()

    def attention(r):
        hs = range(XATTN_HEADS)
        sl = [slice(h * hd, (h + 1) * hd) for h in hs]
        s = [_dot_nt(q_scr[r, sl[h]], k_ref[0, :, sl[h]]) for h in hs]
        tick()
        p = [jnp.exp(s[h] - jnp.max(s[h], axis=1, keepdims=True)) for h in hs]
        tick()
        p = [p[h] / jnp.sum(p[h], axis=1, keepdims=True) for h in hs]
        for h in hs:
            o_scr[r, sl[h]] = _dot(p[h].astype(BF16), v_ref[0, :, sl[h]]).astype(BF16)
        tick(len(fillers))

    for piece in q_pieces(halves[0]):
        piece()
    fillers.extend(q_pieces(halves[1]))
    attention(halves[0])
    fillers.extend(xo_pieces(halves[0]))
    attention(halves[1])
    for piece in xo_pieces(halves[1]):
        piece()
    x2 = x2_ref[0]

    xn = _rms(x2, gmoe_ref[...])
    xh, xm, _ = _split3(xn)
    wh, wm, _ = _split3(wr_ref[...])
    logits = _dot_nt(wh, xh) + (_dot_nt(wh, xm) + _dot_nt(wm, xh)) + br_ref[...]

    e_iota = lax.broadcasted_iota(jnp.int32, (E, ts), 0)
    work = logits
    tops, idxs, hots = [], [], []
    for _ in range(TOP_K):
        m = jnp.max(work, axis=0, keepdims=True)
        i = jnp.min(jnp.where(work == m, e_iota, E), axis=0, keepdims=True)
        hot = e_iota == i
        work = jnp.where(hot, -jnp.inf, work)
        tops.append(m); idxs.append(i); hots.append(hot)
    ex = [jnp.exp(t - tops[0]) for t in tops]
    tot = ex[0] + ex[1] + ex[2] + ex[3]
    gate_ref[...] = jnp.concatenate([e / tot for e in ex], axis=0)
    idx_ref[...] = jnp.concatenate(idxs, axis=0)

    hot_all = jnp.where(hots[0] | hots[1] | hots[2] | hots[3], 1.0, 0.0)
    r = lax.broadcasted_iota(jnp.int32, (ts, ts), 0)
    c = lax.broadcasted_iota(jnp.int32, (ts, ts), 1)
    before = jnp.where(r < c, 1.0, 0.0).astype(BF16)
    base = cnt_scr[:, 0:1] + _dot(hot_all.astype(BF16), before)
    ranks = [jnp.sum(jnp.where(hot, base, 0.0), axis=0, keepdims=True) for hot in hots]
    rank_ref[...] = jnp.concatenate(ranks, axis=0).astype(jnp.int32)
    cnt_new = cnt_scr[...] + jnp.sum(hot_all, axis=1, keepdims=True)
    cnt_scr[...] = cnt_new
    cnt_ref[...] = cnt_new
    _store_row_tiles(xn_ref, xn)


def _xattn_router(x, k, v, g_xattn, w_q, w_xo, g_moe, w_router, b_router):
    B, S, D = x.shape
    M = k.shape[1]
    E = w_router.shape[1]
    ts = min(XATTN_TOKENS, S)
    assert S % ts == 0
    nj = S // ts
    T = B * S
    const = lambda *shape: pl.BlockSpec(shape, lambda b, j: (0,) * len(shape))
    tok = lambda rows: pl.BlockSpec((rows, ts), lambda b, j: (0, b * nj + j))
    return pl.pallas_call(
        _xattn_kernel,
        out_shape=(
            jax.ShapeDtypeStruct((B, S, D), F32),
            jax.ShapeDtypeStruct((T * ROW_TILE, LANES), F32),
            jax.ShapeDtypeStruct((TOP_K, T), jnp.int32),
            jax.ShapeDtypeStruct((TOP_K, T), F32),
            jax.ShapeDtypeStruct((TOP_K, T), jnp.int32),
            jax.ShapeDtypeStruct((E, LANES), F32),
        ),
        grid=(B, nj),
        in_specs=[
            pl.BlockSpec((1, ts, D), lambda b, j: (b, j, 0)),
            const(1, D), const(D, D),
            pl.BlockSpec((1, M, D), lambda b, j: (b, 0, 0)),
            pl.BlockSpec((1, M, D), lambda b, j: (b, 0, 0)),
            const(D, D), const(1, D), const(E, D), const(E, 1),
        ],
        out_specs=(
            pl.BlockSpec((1, ts, D), lambda b, j: (b, j, 0)),
            pl.BlockSpec((ts * ROW_TILE, LANES), lambda b, j: (b * nj + j, 0)),
            tok(TOP_K), tok(TOP_K), tok(TOP_K),
            const(E, LANES),
        ),
        scratch_shapes=[pltpu.VMEM((ts, D), BF16), pltpu.VMEM((ts, D), BF16), pltpu.VMEM((ts, D), BF16),
                        pltpu.VMEM((E, LANES), F32)],
        compiler_params=pltpu.CompilerParams(
            dimension_semantics=("arbitrary", "arbitrary"), vmem_limit_bytes=VMEM_LIMIT),
        name="xattn_router",
    )(x, g_xattn.reshape(1, D), w_q.astype(BF16), k, v, w_xo.astype(BF16), g_moe.reshape(1, D),
      w_router.T, b_router.reshape(E, 1))


def _dispatch_kernel(dest_ref, cnt_ref, pend_ref, xn_ref, xs_ref, zero_scr, sem, zsem, *, n_experts):
    tt = xn_ref.shape[0] // ROW_TILE
    R = zero_scr.shape[0] // ROW_TILE

    @pl.when(pl.program_id(0) == 0)
    def _():
        zero_scr[...] = jnp.zeros_like(zero_scr)

        def fill_copy(e):
            start = pl.multiple_of((pend_ref[e] - R) * ROW_TILE, R * ROW_TILE)
            return pltpu.make_async_copy(zero_scr, xs_ref.at[pl.ds(start, R * ROW_TILE)], zsem)

        def start_fill(e, carry):
            @pl.when(cnt_ref[e] > 0)
            def _():
                fill_copy(e).start()
            return carry

        def wait_fill(e, carry):
            @pl.when(cnt_ref[e] > 0)
            def _():
                fill_copy(e).wait()
            return carry

        lax.fori_loop(0, n_experts, start_fill, 0)
        lax.fori_loop(0, n_experts, wait_fill, 0)

    def issue(grp, carry):
        t0 = pl.multiple_of(grp * ROW_UNROLL, ROW_UNROLL)
        for kk in range(TOP_K):
            for u in range(ROW_UNROLL):
                dst = pl.multiple_of(dest_ref[0, 0, kk * tt + t0 + u] * ROW_TILE, ROW_TILE)
                row = pl.multiple_of((t0 + u) * ROW_TILE, ROW_TILE)
                pltpu.make_async_copy(xn_ref.at[pl.ds(row, ROW_TILE)], xs_ref.at[pl.ds(dst, ROW_TILE)],
                                      sem).start(priority=u % 2)
        return carry

    lax.fori_loop(0, tt // ROW_UNROLL, issue, 0)
    for kk in range(TOP_K):
        pltpu.make_async_copy(xn_ref, xs_ref.at[pl.ds(0, tt * ROW_TILE)], sem).wait()


def _dispatch(xn, dest, counts, pad_end, n_slots):
    T = dest.shape[1]
    tt = min(DISPATCH_TOKENS, T)
    assert T % tt == 0
    E = counts.shape[0]
    dest = dest.reshape(TOP_K, T // tt, tt).transpose(1, 0, 2).reshape(T // tt, 1, TOP_K * tt)
    return pl.pallas_call(
        functools.partial(_dispatch_kernel, n_experts=E),
        out_shape=jax.ShapeDtypeStruct((n_slots * ROW_TILE, LANES), F32),
        grid=(T // tt,),
        in_specs=[
            pl.BlockSpec((1, 1, TOP_K * tt), lambda i: (i, 0, 0), memory_space=pltpu.SMEM),
            pl.BlockSpec(memory_space=pltpu.SMEM),
            pl.BlockSpec(memory_space=pltpu.SMEM),
            pl.BlockSpec((tt * ROW_TILE, LANES), lambda i: (i, 0)),
        ],
        out_specs=pl.BlockSpec(memory_space=pl.ANY),
        scratch_shapes=[pltpu.VMEM((MOE_ROWS * ROW_TILE, LANES), F32), pltpu.SemaphoreType.DMA,
                        pltpu.SemaphoreType.DMA],
        compiler_params=pltpu.CompilerParams(
            dimension_semantics=("arbitrary",), vmem_limit_bytes=VMEM_LIMIT, has_side_effects=True),
        name="dispatch",
    )(dest, counts, pad_end, xn)


def _expert_kernel(be_ref, nu_ref, reg_ref, nxt_ref, xs_ref, wg_ref, bg_ref, wu_ref, bu_ref, wd_ref, bd_ref, y_ref,
                   wf_scr, wg_scr, wu_scr, wd_scr, sem):
    j = pl.program_id(0)
    active = j < nu_ref[0]
    w_hbm = (wg_ref, wu_ref, wd_ref)

    def fetch(e, slot):
        return [pltpu.make_async_copy(w_hbm[i].at[e], wf_scr.at[slot, i], sem.at[slot, i]) for i in range(3)]

    @pl.when(active & ((j == 0) | (be_ref[j] != be_ref[jnp.maximum(j - 1, 0)])))
    def _():
        slot = reg_ref[j] % 2

        @pl.when(j == 0)
        def _():
            for cp in fetch(be_ref[0], 0):
                cp.start()

        for cp in fetch(be_ref[j], slot):
            cp.wait()
        wg_scr[...] = wf_scr[slot, 0].astype(BF16)
        wu_scr[...] = wf_scr[slot, 1].astype(BF16)
        wd_scr[...] = wf_scr[slot, 2].astype(BF16)

        @pl.when(nxt_ref[j] >= 0)
        def _():
            for cp in fetch(nxt_ref[j], 1 - slot):
                cp.start()

    @pl.when(active)
    def _():
        xb = _load_row_tiles(xs_ref, xs_ref.shape[0] // ROW_TILE).astype(BF16)
        g = jnp.minimum(_dot(xb, wg_scr[...]) + bg_ref[0], SWIGLU_LIMIT)
        u = jnp.clip(_dot(xb, wu_scr[...]) + bu_ref[0], -SWIGLU_LIMIT, SWIGLU_LIMIT)
        hdn = (u + 1.0) * (g * jax.nn.sigmoid(SWIGLU_ALPHA * g))
        _store_row_tiles(y_ref, _dot(hdn.astype(BF16), wd_scr[...]) + bd_ref[0])


def _experts(xs, block_e, n_used, region, next_e, w_gate, b_gate, w_up, b_up, w_down, b_down):
    E, D, F = w_gate.shape
    assert D == ROW_TILE * LANES and F == D, "the two weight staging slots hold (D, F) and (F, D) alike"
    n_slots = xs.shape[0] // ROW_TILE
    R = MOE_ROWS
    n_blocks = n_slots // R
    blk = lambda j, be, nu, rg, nx: (jnp.minimum(j, nu[0] - 1), 0)
    wsel = lambda j, be, nu, rg, nx: (be[jnp.minimum(j, nu[0] - 1)], 0, 0)
    hbm = pl.BlockSpec(memory_space=pl.ANY)
    grid_spec = pltpu.PrefetchScalarGridSpec(
        num_scalar_prefetch=4,
        grid=(n_blocks,),
        in_specs=[
            pl.BlockSpec((R * ROW_TILE, LANES), blk),
            hbm, pl.BlockSpec((1, 1, F), wsel),
            hbm, pl.BlockSpec((1, 1, F), wsel),
            hbm, pl.BlockSpec((1, 1, D), wsel),
        ],
        out_specs=pl.BlockSpec((R * ROW_TILE, LANES), blk),
        scratch_shapes=[pltpu.VMEM((2, 3, D, F), F32), pltpu.VMEM((D, F), BF16), pltpu.VMEM((D, F), BF16),
                        pltpu.VMEM((F, D), BF16), pltpu.SemaphoreType.DMA((2, 3))],
    )
    return pl.pallas_call(
        _expert_kernel,
        out_shape=jax.ShapeDtypeStruct((n_slots * ROW_TILE, LANES), F32),
        grid_spec=grid_spec,
        compiler_params=pltpu.CompilerParams(dimension_semantics=("arbitrary",), vmem_limit_bytes=VMEM_LIMIT),
        name="experts",
    )(block_e, n_used, region, next_e, xs, w_gate, b_gate.reshape(E, 1, F), w_up, b_up.reshape(E, 1, F),
      w_down, b_down.reshape(E, 1, D))


def _combine_kernel(dcur_ref, dnext_ref, x_ref, gate_ref, gfin_ref, ys_ref, o_ref, buf, sem):
    tc = x_ref.shape[0]
    i = pl.program_id(0)
    slot = i % 2

    def issue_all(d_ref, into):
        def issue(grp, carry):
            t0 = pl.multiple_of(grp * ROW_UNROLL, ROW_UNROLL)
            for kk in range(TOP_K):
                for u in range(ROW_UNROLL):
                    src = pl.multiple_of(d_ref[kk, t0 + u] * ROW_TILE, ROW_TILE)
                    row = pl.multiple_of((t0 + u) * ROW_TILE, ROW_TILE)
                    pltpu.make_async_copy(ys_ref.at[pl.ds(src, ROW_TILE)], buf.at[into, kk, pl.ds(row, ROW_TILE)],
                                          sem.at[into]).start(priority=u % 2)
            return carry

        lax.fori_loop(0, tc // ROW_UNROLL, issue, 0)

    def wait_slot(s):
        for kk in range(TOP_K):
            pltpu.make_async_copy(ys_ref.at[pl.ds(0, tc * ROW_TILE)], buf.at[s, kk], sem.at[s]).wait()

    @pl.when(i == 0)
    def _():
        issue_all(dcur_ref, 0)

    wait_slot(slot)
    other = 1 - slot
    for t in range(tc):
        for kk in range(TOP_K):
            src = pl.multiple_of(dnext_ref[kk, t] * ROW_TILE, ROW_TILE)
            pltpu.make_async_copy(ys_ref.at[pl.ds(src, ROW_TILE)], buf.at[other, kk, pl.ds(t * ROW_TILE, ROW_TILE)],
                                  sem.at[other]).start(priority=kk % 2)
    for r0 in range(0, tc, COMBINE_ROWS):
        rows = slice(r0, r0 + COMBINE_ROWS)
        acc = x_ref[rows, :]
        for kk in range(TOP_K):
            acc = acc + gate_ref[rows, kk:kk + 1] * _load_row_tiles(buf.at[slot, kk], COMBINE_ROWS, r0)
        o_ref[rows, :] = _rms(acc, gfin_ref[...])

    @pl.when(i == pl.num_programs(0) - 1)
    def _():
        wait_slot(other)


def _combine(x2, gate_t, dest, ys, g_final):
    T, D = x2.shape
    tc = min(COMBINE_TOKENS, T)
    assert T % tc == 0
    nb = T // tc
    return pl.pallas_call(
        _combine_kernel,
        out_shape=jax.ShapeDtypeStruct((T, D), F32),
        grid=(nb,),
        in_specs=[
            pl.BlockSpec((TOP_K, tc), lambda i: (0, i), memory_space=pltpu.SMEM),
            pl.BlockSpec((TOP_K, tc), lambda i: (0, jnp.minimum(i + 1, nb - 1)), memory_space=pltpu.SMEM),
            pl.BlockSpec((tc, D), lambda i: (i, 0)),
            pl.BlockSpec((tc, TOP_K), lambda i: (i, 0)),
            pl.BlockSpec((1, D), lambda i: (0, 0)),
            pl.BlockSpec(memory_space=pl.ANY),
        ],
        out_specs=pl.BlockSpec((tc, D), lambda i: (i, 0)),
        scratch_shapes=[pltpu.VMEM((2, TOP_K, tc * ROW_TILE, LANES), F32), pltpu.SemaphoreType.DMA((2,))],
        compiler_params=pltpu.CompilerParams(dimension_semantics=("arbitrary",), vmem_limit_bytes=VMEM_LIMIT),
        name="combine",
    )(dest, dest, x2, gate_t, g_final.reshape(1, D), ys)


def _moe(x2, xn, idx, gate, rank, cnt, w_gate, b_gate, w_up, b_up, w_down, b_down, g_final):
    T = x2.shape[0]
    E = w_gate.shape[0]
    R = MOE_ROWS
    n_blocks = -(-T * TOP_K // R) + E
    n_slots = n_blocks * R
    counts = cnt[:, 0].astype(jnp.int32)
    padded = (counts + R - 1) // R * R
    pad_end = jnp.cumsum(padded)
    pad_start = pad_end - padded
    dest = rank
    for e in range(E):
        dest = dest + jnp.where(idx == e, pad_start[e], 0)
    n_used = (pad_end[-1:] // R).astype(jnp.int32)
    block_e = jnp.minimum(jnp.sum(jnp.arange(n_blocks)[:, None] * R >= pad_end[None, :], axis=1), E - 1)
    block_e = block_e.astype(jnp.int32)
    blocks = jnp.arange(n_blocks)
    first = (blocks < n_used[0]) & ((blocks == 0) | (block_e != jnp.roll(block_e, 1)))
    region = (jnp.cumsum(first) - 1).astype(jnp.int32)
    later = jnp.where((counts[None, :] > 0) & (jnp.arange(E)[None, :] > block_e[:, None]), jnp.arange(E)[None, :], E)
    next_e = jnp.min(later, axis=1)
    next_e = jnp.where(next_e == E, -1, next_e).astype(jnp.int32)

    xs = _dispatch(xn, dest, counts, pad_end.astype(jnp.int32), n_slots)
    ys = _experts(xs, block_e, n_used, region, next_e, w_gate, b_gate, w_up, b_up, w_down, b_down)
    return _combine(x2, gate.T, dest, ys, g_final)


def kernel(x, mem, g_mix, w_in, b_gates, g_mlstm_head, w_spatial, b_spatial, g_gmlp_v, g_gmlp_out, w_out,
           g_xattn, g_mem, w_q, w_kv, w_xo, g_moe, w_router, b_router, w_gate, b_gate, w_up, b_up,
           w_down, b_down, g_final):
    B, S, D = x.shape
    assert g_mix.shape[0] == 1, "the combine kernel fuses the closing norm, so exactly one layer is supported"
    x1 = _mixer(x, g_mix[0], w_in[0], b_gates[0], g_mlstm_head[0], w_spatial[0], b_spatial[0],
                g_gmlp_v[0], g_gmlp_out[0], w_out[0])
    k, v = _kv_proj(mem, g_mem[0], w_kv[0])
    x2, xn, idx, gate, rank, cnt = _xattn_router(x1, k, v, g_xattn[0], w_q[0], w_xo[0], g_moe[0],
                                                 w_router[0], b_router[0])
    out = _moe(x2.reshape(B * S, D), xn, idx, gate, rank, cnt, w_gate[0], b_gate[0], w_up[0], b_up[0],
               w_down[0], b_down[0], g_final)
    return out.reshape(B, S, D)
```

```python
import functools

import jax
import jax.numpy as jnp
from jax import lax
from jax.experimental import pallas as pl
from jax.experimental.pallas import tpu as pltpu

F32 = jnp.float32
BF16 = jnp.bfloat16

MLSTM_HEADS = 4
MLSTM_DK = 64
MLSTM_DV = 128
CHUNK = 128
GMLP_GROUPS = 4
GMLP_DG = 128
XATTN_HEADS = 4
TOP_K = 4
GATE_SOFTCAP = 15.0
SWIGLU_LIMIT = 7.0
SWIGLU_ALPHA = 1.702
NORM_EPS = 1e-6

LANES = 128
ROW_TILE = 8

MIX_TOKENS = 512
PROJ_PIECE = 512
XATTN_TOKENS = 512
MOE_ROWS = 512
DISPATCH_TOKENS = 2048
COMBINE_TOKENS = 256
COMBINE_ROWS = 32
ROW_UNROLL = 16
VMEM_LIMIT = 52 * 1024 * 1024


def _rms(x, g):
    return x * lax.rsqrt(jnp.mean(x * x, axis=-1, keepdims=True) + NORM_EPS) * g


def _gelu_tanh(x):
    return 0.5 * x * (1.0 + jnp.tanh(0.7978845608028654 * (x + 0.044715 * x * x * x)))


def _dot(a, b):
    return jnp.dot(a, b, preferred_element_type=F32)


def _dot_nt(a, b):
    return lax.dot_general(a, b, (((1,), (1,)), ((), ())), preferred_element_type=F32)


def _load_row_tiles(ref, rows, first=0):
    return jnp.concatenate(
        [ref[pl.ds(first * ROW_TILE + j, rows, stride=ROW_TILE), :] for j in range(ROW_TILE)], axis=1)


def _store_row_tiles(ref, val):
    for j in range(ROW_TILE):
        ref[pl.ds(j, val.shape[0], stride=ROW_TILE), :] = val[:, j * LANES:(j + 1) * LANES]


def _split3(x):
    hi = x.astype(BF16)
    r1 = x - hi.astype(F32)
    mid = r1.astype(BF16)
    lo = (r1 - mid.astype(F32)).astype(BF16)
    return hi, mid, lo


def _mixer_kernel(x_ref, gmix_ref, win_ref, bg_ref, ghead_ref, wsp_ref, bsp_ref, gzv_ref, gzo_ref, wout_ref,
                  o_ref, xnb_scr, proj_scr, y_scr, ct_scr, m_scr, *, n_chunks, cols):
    qk0, v0, og0, gu0, gv0, gt0 = cols
    H, G, L = MLSTM_HEADS, GMLP_GROUPS, CHUNK

    @pl.when(pl.program_id(1) == 0)
    def _():
        ct_scr[...] = jnp.zeros_like(ct_scr)
        m_scr[...] = jnp.zeros_like(m_scr)

    row = lax.broadcasted_iota(jnp.int32, (L, L), 0)
    col = lax.broadcasted_iota(jnp.int32, (L, L), 1)
    causal = row >= col
    tril = jnp.where(causal, 1.0, 0.0).astype(BF16)
    triu = jnp.where(row <= col, 1.0, 0.0).astype(BF16)
    lane = lax.broadcasted_iota(jnp.int32, (L, LANES), 1)
    ones_col = jnp.where(lane == 0, 1.0, 0.0).astype(F32)

    xnb_scr[...] = _rms(x_ref[0], gmix_ref[...]).astype(BF16)

    def in_pieces(c):
        r = slice(c * L, (c + 1) * L)
        for p in range(0, proj_scr.shape[1], PROJ_PIECE):
            cs = slice(p, min(p + PROJ_PIECE, proj_scr.shape[1]))
            yield lambda cs=cs: proj_scr.__setitem__((r, cs), _dot(xnb_scr[r, :], win_ref[:, cs]))

    def out_pieces(c):
        r = slice(c * L, (c + 1) * L)
        for p in range(0, o_ref.shape[2], PROJ_PIECE):
            cs = slice(p, p + PROJ_PIECE)
            yield lambda cs=cs: o_ref.__setitem__((0, r, cs), x_ref[0, r, cs] + _dot(y_scr[r, :], wout_ref[:, cs]))

    fillers = []

    def tick(n=1):
        for _ in range(min(n, len(fillers))):
            fillers.pop(0)()

    for piece in in_pieces(0):
        piece()

    for c in range(n_chunks):
        rows = slice(c * L, (c + 1) * L)
        if c + 1 < n_chunks:
            fillers.extend(in_pieces(c + 1))
        if c > 0:
            fillers.extend(out_pieces(c - 1))
        pre = proj_scr[rows, gt0:gt0 + LANES] + bg_ref[...]
        capped = GATE_SOFTCAP * jnp.tanh(pre * (1.0 / GATE_SOFTCAP))
        log_sig = jnp.minimum(capped, 0.0) - jnp.log1p(jnp.exp(-jnp.abs(capped)))
        lg = jnp.where(lane < H, capped, log_sig)
        lg_t = lg.T[0:ROW_TILE, :]
        bcol = sum(_dot(tril, p) for p in _split3(lg))
        brow = sum(_dot(p, triu) for p in _split3(lg_t))

        hs = range(H)
        b_c = [bcol[:, H + h:H + h + 1] for h in hs]
        i_c = [lg[:, h:h + 1] for h in hs]
        b_r = [brow[H + h:H + h + 1, :] for h in hs]
        i_r = [lg_t[h:h + 1, :] for h in hs]
        m_prev = [m_scr[h, 0:1, 0:1] for h in hs]
        qk = [proj_scr[rows, qk0 + h * LANES:qk0 + (h + 1) * LANES] for h in hs]
        q = [jnp.where(lane < MLSTM_DK, qk[h] * (MLSTM_DK ** -0.5), 0.0) for h in hs]
        k = [jnp.where(lane < MLSTM_DK, pltpu.roll(qk[h], MLSTM_DK, axis=1), 0.0) for h in hs]
        qb = [q[h].astype(BF16) for h in hs]
        kb = [k[h].astype(BF16) for h in hs]
        tick()
        vaug = [jnp.concatenate([proj_scr[rows, v0 + h * MLSTM_DV:v0 + (h + 1) * MLSTM_DV], ones_col], axis=1)
                for h in hs]

        d = [jnp.where(causal, b_c[h] - b_r[h] + i_r[h], -jnp.inf) for h in hs]
        tick()
        inter = [b_c[h] + m_prev[h] for h in hs]
        m_t = [jnp.maximum(inter[h], jnp.max(d[h], axis=1, keepdims=True)) for h in hs]
        tick()
        w_intra = [jnp.exp(d[h] - m_t[h]) for h in hs]
        tick()
        w_inter = [jnp.exp(inter[h] - m_t[h]) for h in hs]
        s = [_dot_nt(qb[h], kb[h]) * w_intra[h] for h in hs]
        tick()
        ct = [ct_scr[h] for h in hs]
        na = [_dot(s[h].astype(BF16), vaug[h].astype(BF16)) + w_inter[h] * _dot(qb[h], ct[h].astype(BF16))
              for h in hs]
        hh = [na[h][:, :MLSTM_DV] / jnp.maximum(jnp.abs(na[h][:, MLSTM_DV:MLSTM_DV + 1]), jnp.exp(-m_t[h]))
              for h in hs]

        b_last = [b_c[h][L - 1:L, :] for h in hs]
        g_c = [b_last[h] - b_c[h] + i_c[h] for h in hs]
        m_new = [jnp.maximum(b_last[h] + m_prev[h], jnp.max(g_c[h], axis=0, keepdims=True)) for h in hs]
        wk = [jnp.exp(g_c[h] - m_new[h]) for h in hs]
        decay = [jnp.exp(b_last[h] + m_prev[h] - m_new[h]) for h in hs]
        tick()
        for h in hs:
            ct_scr[h] = decay[h] * ct[h] + _dot(k[h].T.astype(BF16), (wk[h] * vaug[h]).astype(BF16))
            m_scr[h] = jnp.broadcast_to(m_new[h], m_scr.shape[1:])

        hn = [_rms(hh[h], ghead_ref[:, h * MLSTM_DV:(h + 1) * MLSTM_DV]) for h in hs]
        tick()
        for h in hs:
            og = proj_scr[rows, og0 + h * MLSTM_DV:og0 + (h + 1) * MLSTM_DV]
            y_scr[rows, h * MLSTM_DV:(h + 1) * MLSTM_DV] = (jax.nn.sigmoid(og) * hn[h]).astype(BF16)

        gs = range(G)
        sl = [slice(g * GMLP_DG, (g + 1) * GMLP_DG) for g in gs]
        z = [_rms(_gelu_tanh(proj_scr[rows, gv0 + g * GMLP_DG:gv0 + (g + 1) * GMLP_DG]), gzv_ref[:, sl[g]]) for g in gs]
        mixed = [_dot(jnp.where(causal, wsp_ref[g], 0.0).astype(BF16), z[g].astype(BF16)) + bsp_ref[g] for g in gs]
        u = [_gelu_tanh(proj_scr[rows, gu0 + g * GMLP_DG:gu0 + (g + 1) * GMLP_DG]) for g in gs]
        tick()
        yg = [_rms(u[g] * mixed[g], gzo_ref[:, sl[g]]) for g in gs]
        tick()
        for g in gs:
            y_scr[rows, H * MLSTM_DV + g * GMLP_DG:H * MLSTM_DV + (g + 1) * GMLP_DG] = yg[g].astype(BF16)
        tick(len(fillers))
    for piece in out_pieces(n_chunks - 1):
        piece()


def _mixer(x, g_mix, w_in, b_gates, g_head, w_spatial, b_spatial, g_zv, g_zo, w_out):
    B, S, D = x.shape
    H, G, L = MLSTM_HEADS, GMLP_GROUPS, CHUNK
    qw, vw, gw = H * MLSTM_DK, H * MLSTM_DV, G * GMLP_DG
    ts = min(MIX_TOKENS, S)
    assert S % ts == 0 and ts % L == 0

    assert 2 * MLSTM_DK == LANES
    c = 0
    w_q = w_in[:, c:c + qw].reshape(D, H, MLSTM_DK); c += qw
    w_k = w_in[:, c:c + qw].reshape(D, H, MLSTM_DK); c += qw
    w_qk = jnp.concatenate([w_q, w_k], axis=2).reshape(D, H * LANES)
    w_v = w_in[:, c:c + vw]; c += vw
    w_o = w_in[:, c:c + vw]; c += vw
    w_g = jnp.pad(w_in[:, c:c + 2 * H], ((0, 0), (0, LANES - 2 * H))); c += 2 * H
    w_gu = w_in[:, c:c + gw]; c += gw
    w_gv = w_in[:, c:c + gw]; c += gw
    w_pack = jnp.concatenate([w_qk, w_v, w_o, w_gu, w_gv, w_g], axis=1).astype(BF16)
    hp = H * LANES
    cols = (0, hp, hp + vw, hp + 2 * vw, hp + 2 * vw + gw, hp + 2 * vw + 2 * gw)
    n_cols = w_pack.shape[1]
    bg = jnp.pad(b_gates, (0, LANES - 2 * H)).reshape(1, LANES)
    bsp = jnp.broadcast_to(b_spatial[:, :, None], (G, L, GMLP_DG))

    const = lambda *shape: pl.BlockSpec(shape, lambda b, j: (0,) * len(shape))
    return pl.pallas_call(
        functools.partial(_mixer_kernel, n_chunks=ts // L, cols=cols),
        out_shape=jax.ShapeDtypeStruct((B, S, D), F32),
        grid=(B, S // ts),
        in_specs=[
            pl.BlockSpec((1, ts, D), lambda b, j: (b, j, 0)),
            const(1, D), const(D, n_cols), const(1, LANES), const(1, vw),
            const(G, L, L), const(G, L, GMLP_DG), const(1, gw), const(1, gw), const(vw + gw, D),
        ],
        out_specs=pl.BlockSpec((1, ts, D), lambda b, j: (b, j, 0)),
        scratch_shapes=[
            pltpu.VMEM((ts, D), BF16),
            pltpu.VMEM((ts, n_cols), F32),
            pltpu.VMEM((ts, vw + gw), BF16),
            pltpu.VMEM((H, LANES, 2 * MLSTM_DV), F32),
            pltpu.VMEM((H, ROW_TILE, LANES), F32),
        ],
        compiler_params=pltpu.CompilerParams(
            dimension_semantics=("arbitrary", "arbitrary"), vmem_limit_bytes=VMEM_LIMIT),
        name="mixer",
    )(x, g_mix.reshape(1, D), w_pack, bg, g_head.reshape(1, vw), w_spatial, bsp,
      g_zv.reshape(1, gw), g_zo.reshape(1, gw), w_out.astype(BF16))


def _kv_kernel(mem_ref, g_ref, w_ref, k_ref, v_ref):
    D = mem_ref.shape[-1]
    kv = _dot(_rms(mem_ref[0], g_ref[...]).astype(BF16), w_ref[...])
    k_ref[0] = kv[:, :D].astype(BF16)
    v_ref[0] = kv[:, D:].astype(BF16)


def _kv_proj(mem, g_mem, w_kv):
    B, M, D = mem.shape
    return pl.pallas_call(
        _kv_kernel,
        out_shape=(jax.ShapeDtypeStruct((B, M, D), BF16), jax.ShapeDtypeStruct((B, M, D), BF16)),
        grid=(B,),
        in_specs=[
            pl.BlockSpec((1, M, D), lambda b: (b, 0, 0)),
            pl.BlockSpec((1, D), lambda b: (0, 0)),
            pl.BlockSpec((D, 2 * D), lambda b: (0, 0)),
        ],
        out_specs=(pl.BlockSpec((1, M, D), lambda b: (b, 0, 0)), pl.BlockSpec((1, M, D), lambda b: (b, 0, 0))),
        compiler_params=pltpu.CompilerParams(dimension_semantics=("arbitrary",), vmem_limit_bytes=VMEM_LIMIT),
        name="kv_proj",
    )(mem, g_mem.reshape(1, D), w_kv.astype(BF16))


def _xattn_kernel(x_ref, gx_ref, wq_ref, k_ref, v_ref, wxo_ref, gmoe_ref, wr_ref, br_ref,
                  x2_ref, xn_ref, idx_ref, gate_ref, rank_ref, cnt_ref, xnb_scr, q_scr, o_scr, cnt_scr):
    ts, D = x_ref.shape[1], x_ref.shape[2]
    hd = D // XATTN_HEADS
    E = wr_ref.shape[0]

    @pl.when((pl.program_id(0) == 0) & (pl.program_id(1) == 0))
    def _():
        cnt_scr[...] = jnp.zeros_like(cnt_scr)

    xnb_scr[...] = _rms(x_ref[0], gx_ref[...]).astype(BF16)

    half = ts // 2
    halves = [slice(0, half), slice(half, ts)]
    pieces = [slice(p, p + PROJ_PIECE) for p in range(0, D, PROJ_PIECE)]

    def q_pieces(r):
        for cs in pieces:
            yield lambda cs=cs: q_scr.__setitem__(
                (r, cs), (_dot(xnb_scr[r, :], wq_ref[:, cs]) * (hd ** -0.5)).astype(BF16))

    def xo_pieces(r):
        for cs in pieces:
            yield lambda cs=cs: x2_ref.__setitem__(
                (0, r, cs), x_ref[0, r, cs] + _dot(o_scr[r, :], wxo_ref[:, cs]))

    fillers = []

    def tick(n=1):
        for _ in range(min(n, len(fillers))):
            fillers.pop(0)()

    def attention(r):
        hs = range(XATTN_HEADS)
        sl = [slice(h * hd, (h + 1) * hd) for h in hs]
        s = [_dot_nt(q_scr[r, sl[h]], k_ref[0, :, sl[h]]) for h in hs]
        tick()
        p = [jnp.exp(s[h] - jnp.max(s[h], axis=1, keepdims=True)) for h in hs]
        tick()
        p = [p[h] / jnp.sum(p[h], axis=1, keepdims=True) for h in hs]
        for h in hs:
            o_scr[r, sl[h]] = _dot(p[h].astype(BF16), v_ref[0, :, sl[h]]).astype(BF16)
        tick(len(fillers))

    for piece in q_pieces(halves[0]):
        piece()
    fillers.extend(q_pieces(halves[1]))
    attention(halves[0])
    fillers.extend(xo_pieces(halves[0]))
    attention(halves[1])
    for piece in xo_pieces(halves[1]):
        piece()
    x2 = x2_ref[0]

    xn = _rms(x2, gmoe_ref[...])
    xh, xm, _ = _split3(xn)
    wh, wm, _ = _split3(wr_ref[...])
    logits = _dot_nt(wh, xh) + (_dot_nt(wh, xm) + _dot_nt(wm, xh)) + br_ref[...]

    e_iota = lax.broadcasted_iota(jnp.int32, (E, ts), 0)
    work = logits
    tops, idxs, hots = [], [], []
    for _ in range(TOP_K):
        m = jnp.max(work, axis=0, keepdims=True)
        i = jnp.min(jnp.where(work == m, e_iota, E), axis=0, keepdims=True)
        hot = e_iota == i
        work = jnp.where(hot, -jnp.inf, work)
        tops.append(m); idxs.append(i); hots.append(hot)
    ex = [jnp.exp(t - tops[0]) for t in tops]
    tot = ex[0] + ex[1] + ex[2] + ex[3]
    gate_ref[...] = jnp.concatenate([e / tot for e in ex], axis=0)
    idx_ref[...] = jnp.concatenate(idxs, axis=0)

    hot_all = jnp.where(hots[0] | hots[1] | hots[2] | hots[3], 1.0, 0.0)
    r = lax.broadcasted_iota(jnp.int32, (ts, ts), 0)
    c = lax.broadcasted_iota(jnp.int32, (ts, ts), 1)
    before = jnp.where(r < c, 1.0, 0.0).astype(BF16)
    base = cnt_scr[:, 0:1] + _dot(hot_all.astype(BF16), before)
    ranks = [jnp.sum(jnp.where(hot, base, 0.0), axis=0, keepdims=True) for hot in hots]
    rank_ref[...] = jnp.concatenate(ranks, axis=0).astype(jnp.int32)
    cnt_new = cnt_scr[...] + jnp.sum(hot_all, axis=1, keepdims=True)
    cnt_scr[...] = cnt_new
    cnt_ref[...] = cnt_new
    _store_row_tiles(xn_ref, xn)


def _xattn_router(x, k, v, g_xattn, w_q, w_xo, g_moe, w_router, b_router):
    B, S, D = x.shape
    M = k.shape[1]
    E = w_router.shape[1]
    ts = min(XATTN_TOKENS, S)
    assert S % ts == 0
    nj = S // ts
    T = B * S
    const = lambda *shape: pl.BlockSpec(shape, lambda b, j: (0,) * len(shape))
    tok = lambda rows: pl.BlockSpec((rows, ts), lambda b, j: (0, b * nj + j))
    return pl.pallas_call(
        _xattn_kernel,
        out_shape=(
            jax.ShapeDtypeStruct((B, S, D), F32),
            jax.ShapeDtypeStruct((T * ROW_TILE, LANES), F32),
            jax.ShapeDtypeStruct((TOP_K, T), jnp.int32),
            jax.ShapeDtypeStruct((TOP_K, T), F32),
            jax.ShapeDtypeStruct((TOP_K, T), jnp.int32),
            jax.ShapeDtypeStruct((E, LANES), F32),
        ),
        grid=(B, nj),
        in_specs=[
            pl.BlockSpec((1, ts, D), lambda b, j: (b, j, 0)),
            const(1, D), const(D, D),
            pl.BlockSpec((1, M, D), lambda b, j: (b, 0, 0)),
            pl.BlockSpec((1, M, D), lambda b, j: (b, 0, 0)),
            const(D, D), const(1, D), const(E, D), const(E, 1),
        ],
        out_specs=(
            pl.BlockSpec((1, ts, D), lambda b, j: (b, j, 0)),
            pl.BlockSpec((ts * ROW_TILE, LANES), lambda b, j: (b * nj + j, 0)),
            tok(TOP_K), tok(TOP_K), tok(TOP_K),
            const(E, LANES),
        ),
        scratch_shapes=[pltpu.VMEM((ts, D), BF16), pltpu.VMEM((ts, D), BF16), pltpu.VMEM((ts, D), BF16),
                        pltpu.VMEM((E, LANES), F32)],
        compiler_params=pltpu.CompilerParams(
            dimension_semantics=("arbitrary", "arbitrary"), vmem_limit_bytes=VMEM_LIMIT),
        name="xattn_router",
    )(x, g_xattn.reshape(1, D), w_q.astype(BF16), k, v, w_xo.astype(BF16), g_moe.reshape(1, D),
      w_router.T, b_router.reshape(E, 1))


def _dispatch_kernel(dest_ref, fill_ref, xn_ref, xs_ref, zero_scr, sem, zsem):
    tt = xn_ref.shape[0] // ROW_TILE
    R = zero_scr.shape[0] // ROW_TILE

    @pl.when(pl.program_id(0) == 0)
    def _():
        zero_scr[...] = jnp.zeros_like(zero_scr)

        def fill_copy(i):
            start = pl.multiple_of(fill_ref[i] * (R * ROW_TILE), R * ROW_TILE)
            return pltpu.make_async_copy(zero_scr, xs_ref.at[pl.ds(start, R * ROW_TILE)], zsem)

        def start_fill(i, carry):
            @pl.when(fill_ref[i] >= 0)
            def _():
                fill_copy(i).start()
            return carry

        def wait_fill(i, carry):
            @pl.when(fill_ref[i] >= 0)
            def _():
                fill_copy(i).wait()
            return carry

        lax.fori_loop(0, fill_ref.shape[0], start_fill, 0)
        lax.fori_loop(0, fill_ref.shape[0], wait_fill, 0)

    def issue(grp, carry):
        t0 = pl.multiple_of(grp * ROW_UNROLL, ROW_UNROLL)
        for kk in range(TOP_K):
            for u in range(ROW_UNROLL):
                dst = pl.multiple_of(dest_ref[0, 0, kk * tt + t0 + u] * ROW_TILE, ROW_TILE)
                row = pl.multiple_of((t0 + u) * ROW_TILE, ROW_TILE)
                pltpu.make_async_copy(xn_ref.at[pl.ds(row, ROW_TILE)], xs_ref.at[pl.ds(dst, ROW_TILE)],
                                      sem).start(priority=u % 2)
        return carry

    lax.fori_loop(0, tt // ROW_UNROLL, issue, 0)
    for kk in range(TOP_K):
        pltpu.make_async_copy(xn_ref, xs_ref.at[pl.ds(0, tt * ROW_TILE)], sem).wait()


def _dispatch(xn, dest, fill_blocks, n_slots):
    T = dest.shape[1]
    tt = min(DISPATCH_TOKENS, T)
    assert T % tt == 0
    dest = dest.reshape(TOP_K, T // tt, tt).transpose(1, 0, 2).reshape(T // tt, 1, TOP_K * tt)
    return pl.pallas_call(
        _dispatch_kernel,
        out_shape=jax.ShapeDtypeStruct((n_slots * ROW_TILE, LANES), F32),
        grid=(T // tt,),
        in_specs=[
            pl.BlockSpec((1, 1, TOP_K * tt), lambda i: (i, 0, 0), memory_space=pltpu.SMEM),
            pl.BlockSpec(memory_space=pltpu.SMEM),
            pl.BlockSpec((tt * ROW_TILE, LANES), lambda i: (i, 0)),
        ],
        out_specs=pl.BlockSpec(memory_space=pl.ANY),
        scratch_shapes=[pltpu.VMEM((MOE_ROWS * ROW_TILE, LANES), F32), pltpu.SemaphoreType.DMA,
                        pltpu.SemaphoreType.DMA],
        compiler_params=pltpu.CompilerParams(
            dimension_semantics=("arbitrary",), vmem_limit_bytes=VMEM_LIMIT, has_side_effects=True),
        name="dispatch",
    )(dest, fill_blocks, xn)


def _expert_kernel(be_ref, nu_ref, reg_ref, nxt_ref, xs_ref, wg_ref, bg_ref, wu_ref, bu_ref, wd_ref, bd_ref, y_ref,
                   wf_scr, wg_scr, wu_scr, wd_scr, sem):
    j = pl.program_id(0)
    active = j < nu_ref[0]
    w_hbm = (wg_ref, wu_ref, wd_ref)

    def fetch(e, slot):
        return [pltpu.make_async_copy(w_hbm[i].at[e], wf_scr.at[slot, i], sem.at[slot, i]) for i in range(3)]

    @pl.when(active & ((j == 0) | (be_ref[j] != be_ref[jnp.maximum(j - 1, 0)])))
    def _():
        slot = reg_ref[j] % 2

        @pl.when(j == 0)
        def _():
            for cp in fetch(be_ref[0], 0):
                cp.start()

        for cp in fetch(be_ref[j], slot):
            cp.wait()
        wg_scr[...] = wf_scr[slot, 0].astype(BF16)
        wu_scr[...] = wf_scr[slot, 1].astype(BF16)
        wd_scr[...] = wf_scr[slot, 2].astype(BF16)

        @pl.when(nxt_ref[j] >= 0)
        def _():
            for cp in fetch(nxt_ref[j], 1 - slot):
                cp.start()

    @pl.when(jnp.logical_not(active))
    def _():
        y_ref[...] = jnp.zeros_like(y_ref)

    @pl.when(active)
    def _():
        xb = _load_row_tiles(xs_ref, xs_ref.shape[0] // ROW_TILE).astype(BF16)
        g = jnp.minimum(_dot(xb, wg_scr[...]) + bg_ref[0], SWIGLU_LIMIT)
        u = jnp.clip(_dot(xb, wu_scr[...]) + bu_ref[0], -SWIGLU_LIMIT, SWIGLU_LIMIT)
        hdn = (u + 1.0) * (g * jax.nn.sigmoid(SWIGLU_ALPHA * g))
        _store_row_tiles(y_ref, _dot(hdn.astype(BF16), wd_scr[...]) + bd_ref[0])


def _experts(xs, block_e, n_used, region, next_e, w_gate, b_gate, w_up, b_up, w_down, b_down):
    E, D, F = w_gate.shape
    assert D == ROW_TILE * LANES and F == D, "the two weight staging slots hold (D, F) and (F, D) alike"
    n_slots = xs.shape[0] // ROW_TILE
    R = MOE_ROWS
    n_blocks = n_slots // R
    blk = lambda j, be, nu, rg, nx: (jnp.minimum(j, nu[0] - 1), 0)
    wsel = lambda j, be, nu, rg, nx: (be[jnp.minimum(j, nu[0] - 1)], 0, 0)
    hbm = pl.BlockSpec(memory_space=pl.ANY)
    grid_spec = pltpu.PrefetchScalarGridSpec(
        num_scalar_prefetch=4,
        grid=(n_blocks,),
        in_specs=[
            pl.BlockSpec((R * ROW_TILE, LANES), blk),
            hbm, pl.BlockSpec((1, 1, F), wsel),
            hbm, pl.BlockSpec((1, 1, F), wsel),
            hbm, pl.BlockSpec((1, 1, D), wsel),
        ],
        out_specs=pl.BlockSpec((R * ROW_TILE, LANES), lambda j, be, nu, rg, nx: (j, 0)),
        scratch_shapes=[pltpu.VMEM((2, 3, D, F), F32), pltpu.VMEM((D, F), BF16), pltpu.VMEM((D, F), BF16),
                        pltpu.VMEM((F, D), BF16), pltpu.SemaphoreType.DMA((2, 3))],
    )
    return pl.pallas_call(
        _expert_kernel,
        out_shape=jax.ShapeDtypeStruct((n_slots * ROW_TILE, LANES), F32),
        grid_spec=grid_spec,
        compiler_params=pltpu.CompilerParams(dimension_semantics=("arbitrary",), vmem_limit_bytes=VMEM_LIMIT),
        name="experts",
    )(block_e, n_used, region, next_e, xs, w_gate, b_gate.reshape(E, 1, F), w_up, b_up.reshape(E, 1, F),
      w_down, b_down.reshape(E, 1, D))


def _combine_kernel(dcur_ref, dnext_ref, x_ref, gate_ref, gfin_ref, ys_ref, o_ref, buf, sem):
    tc = x_ref.shape[0]
    i = pl.program_id(0)
    slot = i % 2

    def issue_all(d_ref, into):
        def issue(grp, carry):
            t0 = pl.multiple_of(grp * ROW_UNROLL, ROW_UNROLL)
            for kk in range(TOP_K):
                for u in range(ROW_UNROLL):
                    src = pl.multiple_of(d_ref[kk, t0 + u] * ROW_TILE, ROW_TILE)
                    row = pl.multiple_of((t0 + u) * ROW_TILE, ROW_TILE)
                    pltpu.make_async_copy(ys_ref.at[pl.ds(src, ROW_TILE)], buf.at[into, kk, pl.ds(row, ROW_TILE)],
                                          sem.at[into]).start(priority=u % 2)
            return carry

        lax.fori_loop(0, tc // ROW_UNROLL, issue, 0)

    def wait_slot(s):
        for kk in range(TOP_K):
            pltpu.make_async_copy(ys_ref.at[pl.ds(0, tc * ROW_TILE)], buf.at[s, kk], sem.at[s]).wait()

    @pl.when(i == 0)
    def _():
        issue_all(dcur_ref, 0)

    wait_slot(slot)
    other = 1 - slot
    for t in range(tc):
        for kk in range(TOP_K):
            src = pl.multiple_of(dnext_ref[kk, t] * ROW_TILE, ROW_TILE)
            pltpu.make_async_copy(ys_ref.at[pl.ds(src, ROW_TILE)], buf.at[other, kk, pl.ds(t * ROW_TILE, ROW_TILE)],
                                  sem.at[other]).start(priority=kk % 2)
    for r0 in range(0, tc, COMBINE_ROWS):
        rows = slice(r0, r0 + COMBINE_ROWS)
        acc = x_ref[rows, :]
        for kk in range(TOP_K):
            acc = acc + gate_ref[rows, kk:kk + 1] * _load_row_tiles(buf.at[slot, kk], COMBINE_ROWS, r0)
        o_ref[rows, :] = _rms(acc, gfin_ref[...])

    @pl.when(i == pl.num_programs(0) - 1)
    def _():
        wait_slot(other)


def _combine(x2, gate_t, dest, ys, g_final):
    T, D = x2.shape
    tc = min(COMBINE_TOKENS, T)
    assert T % tc == 0
    nb = T // tc
    return pl.pallas_call(
        _combine_kernel,
        out_shape=jax.ShapeDtypeStruct((T, D), F32),
        grid=(nb,),
        in_specs=[
            pl.BlockSpec((TOP_K, tc), lambda i: (0, i), memory_space=pltpu.SMEM),
            pl.BlockSpec((TOP_K, tc), lambda i: (0, jnp.minimum(i + 1, nb - 1)), memory_space=pltpu.SMEM),
            pl.BlockSpec((tc, D), lambda i: (i, 0)),
            pl.BlockSpec((tc, TOP_K), lambda i: (i, 0)),
            pl.BlockSpec((1, D), lambda i: (0, 0)),
            pl.BlockSpec(memory_space=pl.ANY),
        ],
        out_specs=pl.BlockSpec((tc, D), lambda i: (i, 0)),
        scratch_shapes=[pltpu.VMEM((2, TOP_K, tc * ROW_TILE, LANES), F32), pltpu.SemaphoreType.DMA((2,))],
        compiler_params=pltpu.CompilerParams(dimension_semantics=("arbitrary",), vmem_limit_bytes=VMEM_LIMIT),
        name="combine",
    )(dest, dest, x2, gate_t, g_final.reshape(1, D), ys)


def _moe(x2, xn, idx, gate, rank, cnt, w_gate, b_gate, w_up, b_up, w_down, b_down, g_final):
    T = x2.shape[0]
    E = w_gate.shape[0]
    R = MOE_ROWS
    n_blocks = -(-T * TOP_K // R) + E
    n_slots = n_blocks * R
    counts = cnt[:, 0].astype(jnp.int32)
    padded = (counts + R - 1) // R * R
    pad_end = jnp.cumsum(padded)
    pad_start = pad_end - padded
    dest = rank
    for e in range(E):
        dest = dest + jnp.where(idx == e, pad_start[e], 0)
    n_used = (pad_end[-1:] // R).astype(jnp.int32)
    block_e = jnp.minimum(jnp.sum(jnp.arange(n_blocks)[:, None] * R >= pad_end[None, :], axis=1), E - 1)
    block_e = block_e.astype(jnp.int32)
    blocks = jnp.arange(n_blocks)
    first = (blocks < n_used[0]) & ((blocks == 0) | (block_e != jnp.roll(block_e, 1)))
    region = (jnp.cumsum(first) - 1).astype(jnp.int32)
    later = jnp.where((counts[None, :] > 0) & (jnp.arange(E)[None, :] > block_e[:, None]), jnp.arange(E)[None, :], E)
    next_e = jnp.min(later, axis=1)
    next_e = jnp.where(next_e == E, -1, next_e).astype(jnp.int32)

    tails = jnp.where(counts > 0, pad_end // R - 1, -1)
    unused = n_used[0] + jnp.arange(E)
    fill_blocks = jnp.concatenate([tails, jnp.where(unused < n_blocks, unused, -1)]).astype(jnp.int32)
    xs = _dispatch(xn, dest, fill_blocks, n_slots)
    ys = _experts(xs, block_e, n_used, region, next_e, w_gate, b_gate, w_up, b_up, w_down, b_down)
    return _combine(x2, gate.T, dest, ys, g_final)


def kernel(x, mem, g_mix, w_in, b_gates, g_mlstm_head, w_spatial, b_spatial, g_gmlp_v, g_gmlp_out, w_out,
           g_xattn, g_mem, w_q, w_kv, w_xo, g_moe, w_router, b_router, w_gate, b_gate, w_up, b_up,
           w_down, b_down, g_final):
    B, S, D = x.shape
    assert g_mix.shape[0] == 1, "the combine kernel fuses the closing norm, so exactly one layer is supported"
    x1 = _mixer(x, g_mix[0], w_in[0], b_gates[0], g_mlstm_head[0], w_spatial[0], b_spatial[0],
                g_gmlp_v[0], g_gmlp_out[0], w_out[0])
    k, v = _kv_proj(mem, g_mem[0], w_kv[0])
    x2, xn, idx, gate, rank, cnt = _xattn_router(x1, k, v, g_xattn[0], w_q[0], w_xo[0], g_moe[0],
                                                 w_router[0], b_router[0])
    out = _moe(x2.reshape(B * S, D), xn, idx, gate, rank, cnt, w_gate[0], b_gate[0], w_up[0], b_up[0],
               w_down[0], b_down[0], g_final)
    return out.reshape(B, S, D)
```

```python
import functools

import jax
import jax.numpy as jnp
from jax import lax
from jax.experimental import pallas as pl
from jax.experimental.pallas import tpu as pltpu

F32 = jnp.float32
BF16 = jnp.bfloat16

MLSTM_HEADS = 4
MLSTM_DK = 64
MLSTM_DV = 128
CHUNK = 128
GMLP_GROUPS = 4
GMLP_DG = 128
XATTN_HEADS = 4
TOP_K = 4
GATE_SOFTCAP = 15.0
SWIGLU_LIMIT = 7.0
SWIGLU_ALPHA = 1.702
NORM_EPS = 1e-6

LANES = 128
ROW_TILE = 8

MIX_TOKENS = 512
PROJ_PIECE = 512
XATTN_TOKENS = 512
MOE_ROWS = 512
DISPATCH_TOKENS = 2048
COMBINE_TOKENS = 256
COMBINE_ROWS = 32
ROW_UNROLL = 16
VMEM_LIMIT = 52 * 1024 * 1024


def _rms(x, g):
    return x * lax.rsqrt(jnp.mean(x * x, axis=-1, keepdims=True) + NORM_EPS) * g


def _gelu_tanh(x):
    return 0.5 * x * (1.0 + jnp.tanh(0.7978845608028654 * (x + 0.044715 * x * x * x)))


def _dot(a, b):
    return jnp.dot(a, b, preferred_element_type=F32)


def _dot_nt(a, b):
    return lax.dot_general(a, b, (((1,), (1,)), ((), ())), preferred_element_type=F32)


def _load_row_tiles(ref, rows, first=0):
    return jnp.concatenate(
        [ref[pl.ds(first * ROW_TILE + j, rows, stride=ROW_TILE), :] for j in range(ROW_TILE)], axis=1)


def _store_row_tiles(ref, val):
    for j in range(ROW_TILE):
        ref[pl.ds(j, val.shape[0], stride=ROW_TILE), :] = val[:, j * LANES:(j + 1) * LANES]


def _split3(x):
    hi = x.astype(BF16)
    r1 = x - hi.astype(F32)
    mid = r1.astype(BF16)
    lo = (r1 - mid.astype(F32)).astype(BF16)
    return hi, mid, lo


def _mixer_kernel(x_ref, gmix_ref, win_ref, bg_ref, ghead_ref, wsp_ref, bsp_ref, gzv_ref, gzo_ref, wout_ref,
                  o_ref, xnb_scr, proj_scr, y_scr, ct_scr, m_scr, *, n_chunks, cols):
    qk0, v0, og0, gu0, gv0, gt0 = cols
    H, G, L = MLSTM_HEADS, GMLP_GROUPS, CHUNK

    @pl.when(pl.program_id(1) == 0)
    def _():
        ct_scr[...] = jnp.zeros_like(ct_scr)
        m_scr[...] = jnp.zeros_like(m_scr)

    row = lax.broadcasted_iota(jnp.int32, (L, L), 0)
    col = lax.broadcasted_iota(jnp.int32, (L, L), 1)
    causal = row >= col
    tril = jnp.where(causal, 1.0, 0.0).astype(BF16)
    triu = jnp.where(row <= col, 1.0, 0.0).astype(BF16)
    lane = lax.broadcasted_iota(jnp.int32, (L, LANES), 1)
    ones_col = jnp.where(lane == 0, 1.0, 0.0).astype(F32)

    xnb_scr[...] = _rms(x_ref[0], gmix_ref[...]).astype(BF16)

    def in_pieces(c):
        r = slice(c * L, (c + 1) * L)
        for p in range(0, proj_scr.shape[1], PROJ_PIECE):
            cs = slice(p, min(p + PROJ_PIECE, proj_scr.shape[1]))
            yield lambda cs=cs: proj_scr.__setitem__((r, cs), _dot(xnb_scr[r, :], win_ref[:, cs]))

    def out_pieces(c):
        r = slice(c * L, (c + 1) * L)
        for p in range(0, o_ref.shape[2], PROJ_PIECE):
            cs = slice(p, p + PROJ_PIECE)
            yield lambda cs=cs: o_ref.__setitem__((0, r, cs), x_ref[0, r, cs] + _dot(y_scr[r, :], wout_ref[:, cs]))

    fillers = []

    def tick(n=1):
        for _ in range(min(n, len(fillers))):
            fillers.pop(0)()

    for piece in in_pieces(0):
        piece()

    for c in range(n_chunks):
        rows = slice(c * L, (c + 1) * L)
        if c + 1 < n_chunks:
            fillers.extend(in_pieces(c + 1))
        if c > 0:
            fillers.extend(out_pieces(c - 1))
        pre = proj_scr[rows, gt0:gt0 + LANES] + bg_ref[...]
        capped = GATE_SOFTCAP * jnp.tanh(pre * (1.0 / GATE_SOFTCAP))
        log_sig = jnp.minimum(capped, 0.0) - jnp.log1p(jnp.exp(-jnp.abs(capped)))
        lg = jnp.where(lane < H, capped, log_sig)
        lg_t = lg.T[0:ROW_TILE, :]
        bcol = sum(_dot(tril, p) for p in _split3(lg))
        brow = sum(_dot(p, triu) for p in _split3(lg_t))

        hs = range(H)
        b_c = [bcol[:, H + h:H + h + 1] for h in hs]
        i_c = [lg[:, h:h + 1] for h in hs]
        b_r = [brow[H + h:H + h + 1, :] for h in hs]
        i_r = [lg_t[h:h + 1, :] for h in hs]
        m_prev = [m_scr[h, 0:1, 0:1] for h in hs]
        qk = [proj_scr[rows, qk0 + h * LANES:qk0 + (h + 1) * LANES] for h in hs]
        q = [jnp.where(lane < MLSTM_DK, qk[h] * (MLSTM_DK ** -0.5), 0.0) for h in hs]
        k = [jnp.where(lane < MLSTM_DK, pltpu.roll(qk[h], MLSTM_DK, axis=1), 0.0) for h in hs]
        qb = [q[h].astype(BF16) for h in hs]
        kb = [k[h].astype(BF16) for h in hs]
        tick()
        vaug = [jnp.concatenate([proj_scr[rows, v0 + h * MLSTM_DV:v0 + (h + 1) * MLSTM_DV], ones_col], axis=1)
                for h in hs]

        d = [jnp.where(causal, b_c[h] - b_r[h] + i_r[h], -jnp.inf) for h in hs]
        tick()
        inter = [b_c[h] + m_prev[h] for h in hs]
        m_t = [jnp.maximum(inter[h], jnp.max(d[h], axis=1, keepdims=True)) for h in hs]
        tick()
        w_intra = [jnp.exp(d[h] - m_t[h]) for h in hs]
        tick()
        w_inter = [jnp.exp(inter[h] - m_t[h]) for h in hs]
        s = [_dot_nt(qb[h], kb[h]) * w_intra[h] for h in hs]
        tick()
        ct = [ct_scr[h] for h in hs]
        na = [_dot(s[h].astype(BF16), vaug[h].astype(BF16)) + w_inter[h] * _dot(qb[h], ct[h].astype(BF16))
              for h in hs]
        hh = [na[h][:, :MLSTM_DV] / jnp.maximum(jnp.abs(na[h][:, MLSTM_DV:MLSTM_DV + 1]), jnp.exp(-m_t[h]))
              for h in hs]

        b_last = [b_c[h][L - 1:L, :] for h in hs]
        g_c = [b_last[h] - b_c[h] + i_c[h] for h in hs]
        m_new = [jnp.maximum(b_last[h] + m_prev[h], jnp.max(g_c[h], axis=0, keepdims=True)) for h in hs]
        wk = [jnp.exp(g_c[h] - m_new[h]) for h in hs]
        decay = [jnp.exp(b_last[h] + m_prev[h] - m_new[h]) for h in hs]
        tick()
        for h in hs:
            ct_scr[h] = decay[h] * ct[h] + _dot(k[h].T.astype(BF16), (wk[h] * vaug[h]).astype(BF16))
            m_scr[h] = jnp.broadcast_to(m_new[h], m_scr.shape[1:])

        hn = [_rms(hh[h], ghead_ref[:, h * MLSTM_DV:(h + 1) * MLSTM_DV]) for h in hs]
        tick()
        for h in hs:
            og = proj_scr[rows, og0 + h * MLSTM_DV:og0 + (h + 1) * MLSTM_DV]
            y_scr[rows, h * MLSTM_DV:(h + 1) * MLSTM_DV] = (jax.nn.sigmoid(og) * hn[h]).astype(BF16)

        gs = range(G)
        sl = [slice(g * GMLP_DG, (g + 1) * GMLP_DG) for g in gs]
        z = [_rms(_gelu_tanh(proj_scr[rows, gv0 + g * GMLP_DG:gv0 + (g + 1) * GMLP_DG]), gzv_ref[:, sl[g]]) for g in gs]
        mixed = [_dot(jnp.where(causal, wsp_ref[g], 0.0).astype(BF16), z[g].astype(BF16)) + bsp_ref[g] for g in gs]
        u = [_gelu_tanh(proj_scr[rows, gu0 + g * GMLP_DG:gu0 + (g + 1) * GMLP_DG]) for g in gs]
        tick()
        yg = [_rms(u[g] * mixed[g], gzo_ref[:, sl[g]]) for g in gs]
        tick()
        for g in gs:
            y_scr[rows, H * MLSTM_DV + g * GMLP_DG:H * MLSTM_DV + (g + 1) * GMLP_DG] = yg[g].astype(BF16)
        tick(len(fillers))
    for piece in out_pieces(n_chunks - 1):
        piece()


def _mixer(x, g_mix, w_in, b_gates, g_head, w_spatial, b_spatial, g_zv, g_zo, w_out):
    B, S, D = x.shape
    H, G, L = MLSTM_HEADS, GMLP_GROUPS, CHUNK
    qw, vw, gw = H * MLSTM_DK, H * MLSTM_DV, G * GMLP_DG
    ts = min(MIX_TOKENS, S)
    assert S % ts == 0 and ts % L == 0

    assert 2 * MLSTM_DK == LANES
    c = 0
    w_q = w_in[:, c:c + qw].reshape(D, H, MLSTM_DK); c += qw
    w_k = w_in[:, c:c + qw].reshape(D, H, MLSTM_DK); c += qw
    w_qk = jnp.concatenate([w_q, w_k], axis=2).reshape(D, H * LANES)
    w_v = w_in[:, c:c + vw]; c += vw
    w_o = w_in[:, c:c + vw]; c += vw
    w_g = jnp.pad(w_in[:, c:c + 2 * H], ((0, 0), (0, LANES - 2 * H))); c += 2 * H
    w_gu = w_in[:, c:c + gw]; c += gw
    w_gv = w_in[:, c:c + gw]; c += gw
    w_pack = jnp.concatenate([w_qk, w_v, w_o, w_gu, w_gv, w_g], axis=1).astype(BF16)
    hp = H * LANES
    cols = (0, hp, hp + vw, hp + 2 * vw, hp + 2 * vw + gw, hp + 2 * vw + 2 * gw)
    n_cols = w_pack.shape[1]
    bg = jnp.pad(b_gates, (0, LANES - 2 * H)).reshape(1, LANES)
    bsp = jnp.broadcast_to(b_spatial[:, :, None], (G, L, GMLP_DG))

    const = lambda *shape: pl.BlockSpec(shape, lambda b, j: (0,) * len(shape))
    return pl.pallas_call(
        functools.partial(_mixer_kernel, n_chunks=ts // L, cols=cols),
        out_shape=jax.ShapeDtypeStruct((B, S, D), F32),
        grid=(B, S // ts),
        in_specs=[
            pl.BlockSpec((1, ts, D), lambda b, j: (b, j, 0)),
            const(1, D), const(D, n_cols), const(1, LANES), const(1, vw),
            const(G, L, L), const(G, L, GMLP_DG), const(1, gw), const(1, gw), const(vw + gw, D),
        ],
        out_specs=pl.BlockSpec((1, ts, D), lambda b, j: (b, j, 0)),
        scratch_shapes=[
            pltpu.VMEM((ts, D), BF16),
            pltpu.VMEM((ts, n_cols), F32),
            pltpu.VMEM((ts, vw + gw), BF16),
            pltpu.VMEM((H, LANES, 2 * MLSTM_DV), F32),
            pltpu.VMEM((H, ROW_TILE, LANES), F32),
        ],
        compiler_params=pltpu.CompilerParams(
            dimension_semantics=("arbitrary", "arbitrary"), vmem_limit_bytes=VMEM_LIMIT),
        name="mixer",
    )(x, g_mix.reshape(1, D), w_pack, bg, g_head.reshape(1, vw), w_spatial, bsp,
      g_zv.reshape(1, gw), g_zo.reshape(1, gw), w_out.astype(BF16))


def _kv_kernel(mem_ref, g_ref, w_ref, k_ref, v_ref):
    D = mem_ref.shape[-1]
    kv = _dot(_rms(mem_ref[0], g_ref[...]).astype(BF16), w_ref[...])
    k_ref[0] = kv[:, :D].astype(BF16)
    v_ref[0] = kv[:, D:].astype(BF16)


def _kv_proj(mem, g_mem, w_kv):
    B, M, D = mem.shape
    return pl.pallas_call(
        _kv_kernel,
        out_shape=(jax.ShapeDtypeStruct((B, M, D), BF16), jax.ShapeDtypeStruct((B, M, D), BF16)),
        grid=(B,),
        in_specs=[
            pl.BlockSpec((1, M, D), lambda b: (b, 0, 0)),
            pl.BlockSpec((1, D), lambda b: (0, 0)),
            pl.BlockSpec((D, 2 * D), lambda b: (0, 0)),
        ],
        out_specs=(pl.BlockSpec((1, M, D), lambda b: (b, 0, 0)), pl.BlockSpec((1, M, D), lambda b: (b, 0, 0))),
        compiler_params=pltpu.CompilerParams(dimension_semantics=("arbitrary",), vmem_limit_bytes=VMEM_LIMIT),
        name="kv_proj",
    )(mem, g_mem.reshape(1, D), w_kv.astype(BF16))


def _xattn_kernel(x_ref, gx_ref, wq_ref, k_ref, v_ref, wxo_ref, gmoe_ref, wr_ref, br_ref,
                  x2_ref, xn_ref, idx_ref, gate_ref, rank_ref, cnt_ref, xnb_scr, q_scr, o_scr, cnt_scr):
    ts, D = x_ref.shape[1], x_ref.shape[2]
    hd = D // XATTN_HEADS
    E = wr_ref.shape[0]

    @pl.when((pl.program_id(0) == 0) & (pl.program_id(1) == 0))
    def _():
        cnt_scr[...] = jnp.zeros_like(cnt_scr)

    xnb_scr[...] = _rms(x_ref[0], gx_ref[...]).astype(BF16)

    half = ts // 2
    halves = [slice(0, half), slice(half, ts)]
    pieces = [slice(p, p + PROJ_PIECE) for p in range(0, D, PROJ_PIECE)]

    def q_pieces(r):
        for cs in pieces:
            yield lambda cs=cs: q_scr.__setitem__(
                (r, cs), (_dot(xnb_scr[r, :], wq_ref[:, cs]) * (hd ** -0.5)).astype(BF16))

    def xo_pieces(r):
        for cs in pieces:
            yield lambda cs=cs: x2_ref.__setitem__(
                (0, r, cs), x_ref[0, r, cs] + _dot(o_scr[r, :], wxo_ref[:, cs]))

    fillers = []

    def tick(n=1):
        for _ in range(min(n, len(fillers))):
            fillers.pop(0)()

    def attention(r):
        hs = range(XATTN_HEADS)
        sl = [slice(h * hd, (h + 1) * hd) for h in hs]
        s = [_dot_nt(q_scr[r, sl[h]], k_ref[0, :, sl[h]]) for h in hs]
        tick()
        p = [jnp.exp(s[h] - jnp.max(s[h], axis=1, keepdims=True)) for h in hs]
        tick()
        p = [p[h] / jnp.sum(p[h], axis=1, keepdims=True) for h in hs]
        for h in hs:
            o_scr[r, sl[h]] = _dot(p[h].astype(BF16), v_ref[0, :, sl[h]]).astype(BF16)
        tick(len(fillers))

    for piece in q_pieces(halves[0]):
        piece()
    fillers.extend(q_pieces(halves[1]))
    attention(halves[0])
    fillers.extend(xo_pieces(halves[0]))
    attention(halves[1])
    for piece in xo_pieces(halves[1]):
        piece()
    x2 = x2_ref[0]

    xn = _rms(x2, gmoe_ref[...])
    xh, xm, _ = _split3(xn)
    wh, wm, _ = _split3(wr_ref[...])
    logits = _dot_nt(wh, xh) + (_dot_nt(wh, xm) + _dot_nt(wm, xh)) + br_ref[...]

    e_iota = lax.broadcasted_iota(jnp.int32, (E, ts), 0)
    work = logits
    tops, idxs, hots = [], [], []
    for _ in range(TOP_K):
        m = jnp.max(work, axis=0, keepdims=True)
        i = jnp.min(jnp.where(work == m, e_iota, E), axis=0, keepdims=True)
        hot = e_iota == i
        work = jnp.where(hot, -jnp.inf, work)
        tops.append(m); idxs.append(i); hots.append(hot)
    ex = [jnp.exp(t - tops[0]) for t in tops]
    tot = ex[0] + ex[1] + ex[2] + ex[3]
    gate_ref[...] = jnp.concatenate([e / tot for e in ex], axis=0)
    idx_ref[...] = jnp.concatenate(idxs, axis=0)

    hot_all = jnp.where(hots[0] | hots[1] | hots[2] | hots[3], 1.0, 0.0)
    r = lax.broadcasted_iota(jnp.int32, (ts, ts), 0)
    c = lax.broadcasted_iota(jnp.int32, (ts, ts), 1)
    before = jnp.where(r < c, 1.0, 0.0).astype(BF16)
    base = cnt_scr[:, 0:1] + _dot(hot_all.astype(BF16), before)
    ranks = [jnp.sum(jnp.where(hot, base, 0.0), axis=0, keepdims=True) for hot in hots]
    rank_ref[...] = jnp.concatenate(ranks, axis=0).astype(jnp.int32)
    cnt_new = cnt_scr[...] + jnp.sum(hot_all, axis=1, keepdims=True)
    cnt_scr[...] = cnt_new
    cnt_ref[...] = cnt_new
    _store_row_tiles(xn_ref, xn)


def _xattn_router(x, k, v, g_xattn, w_q, w_xo, g_moe, w_router, b_router):
    B, S, D = x.shape
    M = k.shape[1]
    E = w_router.shape[1]
    ts = min(XATTN_TOKENS, S)
    assert S % ts == 0
    nj = S // ts
    T = B * S
    const = lambda *shape: pl.BlockSpec(shape, lambda b, j: (0,) * len(shape))
    tok = lambda rows: pl.BlockSpec((rows, ts), lambda b, j: (0, b * nj + j))
    return pl.pallas_call(
        _xattn_kernel,
        out_shape=(
            jax.ShapeDtypeStruct((B, S, D), F32),
            jax.ShapeDtypeStruct((T * ROW_TILE, LANES), F32),
            jax.ShapeDtypeStruct((TOP_K, T), jnp.int32),
            jax.ShapeDtypeStruct((TOP_K, T), F32),
            jax.ShapeDtypeStruct((TOP_K, T), jnp.int32),
            jax.ShapeDtypeStruct((E, LANES), F32),
        ),
        grid=(B, nj),
        in_specs=[
            pl.BlockSpec((1, ts, D), lambda b, j: (b, j, 0)),
            const(1, D), const(D, D),
            pl.BlockSpec((1, M, D), lambda b, j: (b, 0, 0)),
            pl.BlockSpec((1, M, D), lambda b, j: (b, 0, 0)),
            const(D, D), const(1, D), const(E, D), const(E, 1),
        ],
        out_specs=(
            pl.BlockSpec((1, ts, D), lambda b, j: (b, j, 0)),
            pl.BlockSpec((ts * ROW_TILE, LANES), lambda b, j: (b * nj + j, 0)),
            tok(TOP_K), tok(TOP_K), tok(TOP_K),
            const(E, LANES),
        ),
        scratch_shapes=[pltpu.VMEM((ts, D), BF16), pltpu.VMEM((ts, D), BF16), pltpu.VMEM((ts, D), BF16),
                        pltpu.VMEM((E, LANES), F32)],
        compiler_params=pltpu.CompilerParams(
            dimension_semantics=("arbitrary", "arbitrary"), vmem_limit_bytes=VMEM_LIMIT),
        name="xattn_router",
    )(x, g_xattn.reshape(1, D), w_q.astype(BF16), k, v, w_xo.astype(BF16), g_moe.reshape(1, D),
      w_router.T, b_router.reshape(E, 1))


def _dispatch_kernel(dest_ref, fill_ref, xn_ref, xs_ref, zero_scr, sem, zsem):
    tt = dest_ref.shape[2] // TOP_K
    R = zero_scr.shape[0] // ROW_TILE

    @pl.when(pl.program_id(0) == 0)
    def _():
        zero_scr[...] = jnp.zeros_like(zero_scr)

        def fill_copy(i):
            start = pl.multiple_of(fill_ref[i] * (R * ROW_TILE), R * ROW_TILE)
            return pltpu.make_async_copy(zero_scr, xs_ref.at[pl.ds(start, R * ROW_TILE)], zsem)

        def start_fill(i, carry):
            @pl.when(fill_ref[i] >= 0)
            def _():
                fill_copy(i).start()
            return carry

        def wait_fill(i, carry):
            @pl.when(fill_ref[i] >= 0)
            def _():
                fill_copy(i).wait()
            return carry

        lax.fori_loop(0, fill_ref.shape[0], start_fill, 0)
        lax.fori_loop(0, fill_ref.shape[0], wait_fill, 0)

    i = pl.program_id(0)
    base = i * tt

    def issue(grp, carry):
        t0 = pl.multiple_of(grp * ROW_UNROLL, ROW_UNROLL)
        for kk in range(TOP_K):
            for u in range(ROW_UNROLL):
                dst = pl.multiple_of(dest_ref[0, 0, kk * tt + t0 + u] * ROW_TILE, ROW_TILE)
                row = pl.multiple_of((base + t0 + u) * ROW_TILE, ROW_TILE)
                pltpu.make_async_copy(xn_ref.at[pl.ds(row, ROW_TILE)], xs_ref.at[pl.ds(dst, ROW_TILE)],
                                      sem.at[i % 2]).start(priority=u % 2)
        return carry

    lax.fori_loop(0, tt // ROW_UNROLL, issue, 0)

    def wait_step(s):
        for kk in range(TOP_K):
            pltpu.make_async_copy(xs_ref.at[pl.ds(0, tt * ROW_TILE)], xs_ref.at[pl.ds(0, tt * ROW_TILE)],
                                  sem.at[s]).wait()

    @pl.when(i > 0)
    def _():
        wait_step((i - 1) % 2)

    @pl.when(i == pl.num_programs(0) - 1)
    def _():
        wait_step(i % 2)


def _dispatch(xn, dest, fill_blocks, n_slots):
    T = dest.shape[1]
    tt = min(DISPATCH_TOKENS, T)
    assert T % tt == 0
    dest = dest.reshape(TOP_K, T // tt, tt).transpose(1, 0, 2).reshape(T // tt, 1, TOP_K * tt)
    return pl.pallas_call(
        _dispatch_kernel,
        out_shape=jax.ShapeDtypeStruct((n_slots * ROW_TILE, LANES), F32),
        grid=(T // tt,),
        in_specs=[
            pl.BlockSpec((1, 1, TOP_K * tt), lambda i: (i, 0, 0), memory_space=pltpu.SMEM),
            pl.BlockSpec(memory_space=pltpu.SMEM),
            pl.BlockSpec(memory_space=pl.ANY),
        ],
        out_specs=pl.BlockSpec(memory_space=pl.ANY),
        scratch_shapes=[pltpu.VMEM((MOE_ROWS * ROW_TILE, LANES), F32), pltpu.SemaphoreType.DMA((2,)),
                        pltpu.SemaphoreType.DMA],
        compiler_params=pltpu.CompilerParams(
            dimension_semantics=("arbitrary",), vmem_limit_bytes=VMEM_LIMIT, has_side_effects=True),
        name="dispatch",
    )(dest, fill_blocks, xn)


def _expert_kernel(be_ref, nu_ref, reg_ref, nxt_ref, xs_ref, wg_ref, bg_ref, wu_ref, bu_ref, wd_ref, bd_ref, y_ref,
                   wf_scr, wg_scr, wu_scr, wd_scr, sem):
    j = pl.program_id(0)
    active = j < nu_ref[0]
    w_hbm = (wg_ref, wu_ref, wd_ref)

    def fetch(e, slot):
        return [pltpu.make_async_copy(w_hbm[i].at[e], wf_scr.at[slot, i], sem.at[slot, i]) for i in range(3)]

    @pl.when(active & ((j == 0) | (be_ref[j] != be_ref[jnp.maximum(j - 1, 0)])))
    def _():
        slot = reg_ref[j] % 2

        @pl.when(j == 0)
        def _():
            for cp in fetch(be_ref[0], 0):
                cp.start()

        for cp in fetch(be_ref[j], slot):
            cp.wait()
        wg_scr[...] = wf_scr[slot, 0].astype(BF16)
        wu_scr[...] = wf_scr[slot, 1].astype(BF16)
        wd_scr[...] = wf_scr[slot, 2].astype(BF16)

        @pl.when(nxt_ref[j] >= 0)
        def _():
            for cp in fetch(nxt_ref[j], 1 - slot):
                cp.start()

    @pl.when(jnp.logical_not(active))
    def _():
        y_ref[...] = jnp.zeros_like(y_ref)

    @pl.when(active)
    def _():
        xb = _load_row_tiles(xs_ref, xs_ref.shape[0] // ROW_TILE).astype(BF16)
        g = jnp.minimum(_dot(xb, wg_scr[...]) + bg_ref[0], SWIGLU_LIMIT)
        u = jnp.clip(_dot(xb, wu_scr[...]) + bu_ref[0], -SWIGLU_LIMIT, SWIGLU_LIMIT)
        hdn = (u + 1.0) * (g * jax.nn.sigmoid(SWIGLU_ALPHA * g))
        _store_row_tiles(y_ref, _dot(hdn.astype(BF16), wd_scr[...]) + bd_ref[0])


def _experts(xs, block_e, n_used, region, next_e, w_gate, b_gate, w_up, b_up, w_down, b_down):
    E, D, F = w_gate.shape
    assert D == ROW_TILE * LANES and F == D, "the two weight staging slots hold (D, F) and (F, D) alike"
    n_slots = xs.shape[0] // ROW_TILE
    R = MOE_ROWS
    n_blocks = n_slots // R
    blk = lambda j, be, nu, rg, nx: (jnp.minimum(j, nu[0] - 1), 0)
    wsel = lambda j, be, nu, rg, nx: (be[jnp.minimum(j, nu[0] - 1)], 0, 0)
    hbm = pl.BlockSpec(memory_space=pl.ANY)
    grid_spec = pltpu.PrefetchScalarGridSpec(
        num_scalar_prefetch=4,
        grid=(n_blocks,),
        in_specs=[
            pl.BlockSpec((R * ROW_TILE, LANES), blk),
            hbm, pl.BlockSpec((1, 1, F), wsel),
            hbm, pl.BlockSpec((1, 1, F), wsel),
            hbm, pl.BlockSpec((1, 1, D), wsel),
        ],
        out_specs=pl.BlockSpec((R * ROW_TILE, LANES), lambda j, be, nu, rg, nx: (j, 0)),
        scratch_shapes=[pltpu.VMEM((2, 3, D, F), F32), pltpu.VMEM((D, F), BF16), pltpu.VMEM((D, F), BF16),
                        pltpu.VMEM((F, D), BF16), pltpu.SemaphoreType.DMA((2, 3))],
    )
    return pl.pallas_call(
        _expert_kernel,
        out_shape=jax.ShapeDtypeStruct((n_slots * ROW_TILE, LANES), F32),
        grid_spec=grid_spec,
        compiler_params=pltpu.CompilerParams(dimension_semantics=("arbitrary",), vmem_limit_bytes=VMEM_LIMIT),
        name="experts",
    )(block_e, n_used, region, next_e, xs, w_gate, b_gate.reshape(E, 1, F), w_up, b_up.reshape(E, 1, F),
      w_down, b_down.reshape(E, 1, D))


def _combine_kernel(dcur_ref, dnext_ref, x_ref, gate_ref, gfin_ref, ys_ref, o_ref, buf, sem):
    tc = x_ref.shape[0]
    i = pl.program_id(0)
    slot = i % 2

    def issue_all(d_ref, into):
        def issue(grp, carry):
            t0 = pl.multiple_of(grp * ROW_UNROLL, ROW_UNROLL)
            for kk in range(TOP_K):
                for u in range(ROW_UNROLL):
                    src = pl.multiple_of(d_ref[kk, t0 + u] * ROW_TILE, ROW_TILE)
                    row = pl.multiple_of((t0 + u) * ROW_TILE, ROW_TILE)
                    pltpu.make_async_copy(ys_ref.at[pl.ds(src, ROW_TILE)], buf.at[into, kk, pl.ds(row, ROW_TILE)],
                                          sem.at[into]).start(priority=u % 2)
            return carry

        lax.fori_loop(0, tc // ROW_UNROLL, issue, 0)

    def wait_slot(s):
        for kk in range(TOP_K):
            pltpu.make_async_copy(ys_ref.at[pl.ds(0, tc * ROW_TILE)], buf.at[s, kk], sem.at[s]).wait()

    @pl.when(i == 0)
    def _():
        issue_all(dcur_ref, 0)

    wait_slot(slot)
    other = 1 - slot
    for t in range(tc):
        for kk in range(TOP_K):
            src = pl.multiple_of(dnext_ref[kk, t] * ROW_TILE, ROW_TILE)
            pltpu.make_async_copy(ys_ref.at[pl.ds(src, ROW_TILE)], buf.at[other, kk, pl.ds(t * ROW_TILE, ROW_TILE)],
                                  sem.at[other]).start(priority=kk % 2)
    for r0 in range(0, tc, COMBINE_ROWS):
        rows = slice(r0, r0 + COMBINE_ROWS)
        acc = x_ref[rows, :]
        for kk in range(TOP_K):
            acc = acc + gate_ref[rows, kk:kk + 1] * _load_row_tiles(buf.at[slot, kk], COMBINE_ROWS, r0)
        o_ref[rows, :] = _rms(acc, gfin_ref[...])

    @pl.when(i == pl.num_programs(0) - 1)
    def _():
        wait_slot(other)


def _combine(x2, gate_t, dest, ys, g_final):
    T, D = x2.shape
    tc = min(COMBINE_TOKENS, T)
    assert T % tc == 0
    nb = T // tc
    return pl.pallas_call(
        _combine_kernel,
        out_shape=jax.ShapeDtypeStruct((T, D), F32),
        grid=(nb,),
        in_specs=[
            pl.BlockSpec((TOP_K, tc), lambda i: (0, i), memory_space=pltpu.SMEM),
            pl.BlockSpec((TOP_K, tc), lambda i: (0, jnp.minimum(i + 1, nb - 1)), memory_space=pltpu.SMEM),
            pl.BlockSpec((tc, D), lambda i: (i, 0)),
            pl.BlockSpec((tc, TOP_K), lambda i: (i, 0)),
            pl.BlockSpec((1, D), lambda i: (0, 0)),
            pl.BlockSpec(memory_space=pl.ANY),
        ],
        out_specs=pl.BlockSpec((tc, D), lambda i: (i, 0)),
        scratch_shapes=[pltpu.VMEM((2, TOP_K, tc * ROW_TILE, LANES), F32), pltpu.SemaphoreType.DMA((2,))],
        compiler_params=pltpu.CompilerParams(dimension_semantics=("arbitrary",), vmem_limit_bytes=VMEM_LIMIT),
        name="combine",
    )(dest, dest, x2, gate_t, g_final.reshape(1, D), ys)


def _moe(x2, xn, idx, gate, rank, cnt, w_gate, b_gate, w_up, b_up, w_down, b_down, g_final):
    T = x2.shape[0]
    E = w_gate.shape[0]
    R = MOE_ROWS
    n_blocks = -(-T * TOP_K // R) + E
    n_slots = n_blocks * R
    counts = cnt[:, 0].astype(jnp.int32)
    padded = (counts + R - 1) // R * R
    pad_end = jnp.cumsum(padded)
    pad_start = pad_end - padded
    dest = rank
    for e in range(E):
        dest = dest + jnp.where(idx == e, pad_start[e], 0)
    n_used = (pad_end[-1:] // R).astype(jnp.int32)
    block_e = jnp.minimum(jnp.sum(jnp.arange(n_blocks)[:, None] * R >= pad_end[None, :], axis=1), E - 1)
    block_e = block_e.astype(jnp.int32)
    blocks = jnp.arange(n_blocks)
    first = (blocks < n_used[0]) & ((blocks == 0) | (block_e != jnp.roll(block_e, 1)))
    region = (jnp.cumsum(first) - 1).astype(jnp.int32)
    later = jnp.where((counts[None, :] > 0) & (jnp.arange(E)[None, :] > block_e[:, None]), jnp.arange(E)[None, :], E)
    next_e = jnp.min(later, axis=1)
    next_e = jnp.where(next_e == E, -1, next_e).astype(jnp.int32)

    tails = jnp.where(counts > 0, pad_end // R - 1, -1)
    unused = n_used[0] + jnp.arange(E)
    fill_blocks = jnp.concatenate([tails, jnp.where(unused < n_blocks, unused, -1)]).astype(jnp.int32)
    xs = _dispatch(xn, dest, fill_blocks, n_slots)
    ys = _experts(xs, block_e, n_used, region, next_e, w_gate, b_gate, w_up, b_up, w_down, b_down)
    return _combine(x2, gate.T, dest, ys, g_final)


def kernel(x, mem, g_mix, w_in, b_gates, g_mlstm_head, w_spatial, b_spatial, g_gmlp_v, g_gmlp_out, w_out,
           g_xattn, g_mem, w_q, w_kv, w_xo, g_moe, w_router, b_router, w_gate, b_gate, w_up, b_up,
           w_down, b_down, g_final):
    B, S, D = x.shape
    assert g_mix.shape[0] == 1, "the combine kernel fuses the closing norm, so exactly one layer is supported"
    x1 = _mixer(x, g_mix[0], w_in[0], b_gates[0], g_mlstm_head[0], w_spatial[0], b_spatial[0],
                g_gmlp_v[0], g_gmlp_out[0], w_out[0])
    k, v = _kv_proj(mem, g_mem[0], w_kv[0])
    x2, xn, idx, gate, rank, cnt = _xattn_router(x1, k, v, g_xattn[0], w_q[0], w_xo[0], g_moe[0],
                                                 w_router[0], b_router[0])
    out = _moe(x2.reshape(B * S, D), xn, idx, gate, rank, cnt, w_gate[0], b_gate[0], w_up[0], b_up[0],
               w_down[0], b_down[0], g_final)
    return out.reshape(B, S, D)
```
